```python
import jax, jax.numpy as jnp
from jax import lax
import numpy as np

D_MODEL = 1024
BATCH = 16
SEQ = 2048
DEPTH = 1
DEC_BATCH = 16
DEC_SEQ = 32
PAST_LEN = 4096

CHUNK = 64
D_MIX = D_MODEL
D_MLSTM = D_MIX // 2
D_RET = D_MIX - D_MLSTM
H_M = 4
H_R = 4
DH_M = D_MLSTM // H_M
DH_R = D_RET // H_R
D_FF = ((8 * D_MODEL // 3 + 255) // 256) * 256
D_IN = 4 * D_MLSTM + 4 * D_RET + 2 * H_M
ROPE_BASE = 10000.0
EPS = 1e-6
GN_EPS = 1e-5
LOG_GAMMA = np.log(1.0 - 2.0 ** (-5.0 - np.arange(H_R))).astype(np.float32)

kernel_name = "hymba_mlstm_retention_macaron_stream_step"


def rms_norm(x, g):
    xf = x.astype(jnp.float32)
    y = xf * lax.rsqrt(jnp.mean(xf * xf, axis=-1, keepdims=True) + EPS)
    return (y * g.astype(jnp.float32)).astype(x.dtype)


def swiglu(x, wg, wu, wd):
    return (jax.nn.silu(x @ wg) * (x @ wu)) @ wd


def heads(a, n_heads):
    b, s, _ = a.shape
    return a.reshape(b, s, n_heads, -1).transpose(0, 2, 1, 3)


def head_group_norm(h, g):
    mu = jnp.mean(h, axis=-1, keepdims=True)
    var = jnp.mean(jnp.square(h - mu), axis=-1, keepdims=True)
    hn = (h - mu) * lax.rsqrt(var + GN_EPS)
    b, nh, s, d = h.shape
    return hn.transpose(0, 2, 1, 3).reshape(b, s, nh * d) * g.astype(jnp.float32)


def rope(x, pos):
    half = x.shape[-1] // 2
    inv = ROPE_BASE ** (-jnp.arange(half, dtype=jnp.float32) / half)
    ang = pos.astype(jnp.float32)[:, None] * inv[None, :]
    cos, sin = jnp.cos(ang), jnp.sin(ang)
    x1, x2 = x[..., :half], x[..., half:]
    return jnp.concatenate([x1 * cos - x2 * sin, x1 * sin + x2 * cos], axis=-1)


def mlstm_chunk(carry, inp):
    C, n, m = carry
    q, k, v, ig, lf = inp
    L = q.shape[2]
    b = jnp.cumsum(lf, axis=-1)
    causal = jnp.tril(jnp.ones((L, L), dtype=bool))
    logw = jnp.where(causal, b[..., :, None] - b[..., None, :] + ig[..., None, :], -jnp.inf)
    log_prev = b + m[..., None]
    m_t = jnp.maximum(log_prev, jnp.max(logw, axis=-1))
    w = jnp.exp(logw - m_t[..., None])
    a = jnp.exp(log_prev - m_t)
    s = jnp.einsum('bhtd,bhsd->bhts', q, k) * w
    num = a[..., None] * jnp.einsum('bhtd,bhde->bhte', q, C) + jnp.einsum('bhts,bhse->bhte', s, v)
    den = a * jnp.einsum('bhtd,bhd->bht', q, n) + jnp.sum(s, axis=-1)
    h = num / jnp.maximum(jnp.abs(den), jnp.exp(-m_t))[..., None]
    m_last = m_t[..., -1]
    w_last = jnp.exp(b[..., -1:] - b + ig - m_last[..., None])
    a_last = jnp.exp(b[..., -1] + m - m_last)
    C_new = a_last[..., None, None] * C + jnp.einsum('bhs,bhsd,bhse->bhde', w_last, k, v)
    n_new = a_last[..., None] * n + jnp.einsum('bhs,bhsd->bhd', w_last, k)
    return (C_new, n_new, m_last), h


def retention_chunk(R, inp):
    q, k, v = inp
    L = q.shape[2]
    lg = jnp.asarray(LOG_GAMMA)
    idx = jnp.arange(L, dtype=jnp.float32)
    diff = idx[:, None] - idx[None, :]
    decay = jnp.where(diff >= 0, jnp.exp(lg[:, None, None] * jnp.maximum(diff, 0.0)), 0.0)
    scores = jnp.einsum('bhtd,bhsd->bhts', q, k) * decay
    inner = jnp.einsum('bhts,bhse->bhte', scores, v)
    cross = jnp.exp(lg[:, None] * (idx + 1.0))
    out = inner + cross[:, :, None] * jnp.einsum('bhtd,bhde->bhte', q, R)
    tail = jnp.exp(lg[:, None] * (L - 1.0 - idx))
    R_new = jnp.exp(lg * L)[:, None, None] * R + jnp.einsum('hs,bhsd,bhse->bhde', tail, k, v)
    return R_new, out


def chunked_scan(step, state, seqs, L):
    def to_blocks(a):
        bsz, nh, s = a.shape[:3]
        return jnp.moveaxis(a.reshape(bsz, nh, s // L, L, *a.shape[3:]), 2, 0)
    state, out = lax.scan(step, state, tuple(to_blocks(a) for a in seqs))
    out = jnp.moveaxis(out, 0, 2)
    bsz, nh, nc, l, d = out.shape
    return state, out.reshape(bsz, nh, nc * l, d)


def hybrid_mixer(hn, pos, C, n, m, R, w_in, b_gates, g_gn_mlstm, g_gn_ret, w_out):
    f32 = jnp.float32
    S = hn.shape[1]
    L = min(CHUNK, S)
    proj = jnp.einsum('bsd,de->bse', hn, w_in).astype(f32)
    sizes = [D_MLSTM] * 4 + [D_RET] * 4 + [H_M, H_M]
    offs = np.cumsum(sizes)[:-1].tolist()
    mq, mk, mv, mo, rq, rk, rv, rg, i_pre, f_pre = jnp.split(proj, offs, axis=-1)
    bg = b_gates.astype(f32)
    q = heads(mq, H_M)
    k = heads(mk, H_M) * (DH_M ** -0.5)
    v = heads(mv, H_M)
    ig = jnp.transpose(i_pre + bg[:H_M], (0, 2, 1))
    lf = jnp.transpose(jax.nn.log_sigmoid(f_pre + bg[H_M:]), (0, 2, 1))
    (C_new, n_new, m_new), h_m = chunked_scan(mlstm_chunk, (C, n, m), (q, k, v, ig, lf), L)
    y_m = jax.nn.sigmoid(mo) * head_group_norm(h_m, g_gn_mlstm)
    qr = rope(heads(rq, H_R), pos)
    kr = rope(heads(rk, H_R), pos) * (DH_R ** -0.5)
    vr = heads(rv, H_R)
    R_new, h_r = chunked_scan(retention_chunk, R, (qr, kr, vr), L)
    y_r = jax.nn.silu(rg) * head_group_norm(h_r, g_gn_ret)
    y = jnp.concatenate([y_m, y_r], axis=-1).astype(hn.dtype) @ w_out
    return y, (C_new, n_new, m_new, R_new)


def trunk_layer(x, pos, C, n, m, R, g_ffn1, w_ffn1_gate, w_ffn1_up, w_ffn1_down, g_mix, w_in, b_gates,
                g_gn_mlstm, g_gn_ret, w_out, g_ffn2, w_ffn2_gate, w_ffn2_up, w_ffn2_down):
    x = x + 0.5 * swiglu(rms_norm(x, g_ffn1), w_ffn1_gate, w_ffn1_up, w_ffn1_down)
    y, st = hybrid_mixer(rms_norm(x, g_mix), pos, C, n, m, R, w_in, b_gates, g_gn_mlstm, g_gn_ret, w_out)
    x = x + y
    x = x + 0.5 * swiglu(rms_norm(x, g_ffn2), w_ffn2_gate, w_ffn2_up, w_ffn2_down)
    return x, st


def setup_inputs(seed: int = 0) -> dict:
    key = jax.random.key(seed)
    ks = jax.random.split(key, 24)
    nrm = jax.random.normal
    gain = lambda k, d: 1.0 + 0.02 * nrm(k, (DEPTH, d), jnp.float32)
    b_gates = jnp.concatenate([
        0.1 * nrm(ks[10], (DEPTH, H_M), jnp.float32),
        jnp.linspace(3.0, 6.0, H_M, dtype=jnp.float32)[None, :] + 0.1 * nrm(ks[11], (DEPTH, H_M), jnp.float32)], axis=-1)
    return {
        "x_prompt": nrm(ks[0], (BATCH, SEQ, D_MODEL), jnp.float32),
        "x_sample": nrm(ks[1], (DEC_BATCH, DEC_SEQ, D_MODEL), jnp.float32),
        "state_mlstm_C": 0.1 * nrm(ks[2], (DEPTH, DEC_BATCH, H_M, DH_M, DH_M), jnp.float32),
        "state_mlstm_n": 0.1 * nrm(ks[3], (DEPTH, DEC_BATCH, H_M, DH_M), jnp.float32),
        "state_mlstm_m": 0.5 * nrm(ks[4], (DEPTH, DEC_BATCH, H_M), jnp.float32),
        "state_ret_R": 0.1 * nrm(ks[5], (DEPTH, DEC_BATCH, H_R, DH_R, DH_R), jnp.float32),
        "g_ffn1": gain(ks[6], D_MODEL),
        "w_ffn1_gate": nrm(ks[7], (DEPTH, D_MODEL, D_FF), jnp.float32) * D_MODEL ** -0.5,
        "w_ffn1_up": nrm(ks[8], (DEPTH, D_MODEL, D_FF), jnp.float32) * D_MODEL ** -0.5,
        "w_ffn1_down": nrm(ks[9], (DEPTH, D_FF, D_MODEL), jnp.float32) * D_FF ** -0.5,
        "g_mix": gain(ks[12], D_MODEL),
        "w_in": nrm(ks[13], (DEPTH, D_MODEL, D_IN), jnp.float32) * D_MODEL ** -0.5,
        "b_gates": b_gates,
        "g_gn_mlstm": gain(ks[14], D_MLSTM),
        "g_gn_ret": gain(ks[15], D_RET),
        "w_out": nrm(ks[16], (DEPTH, D_MIX, D_MODEL), jnp.float32) * D_MIX ** -0.5,
        "g_ffn2": gain(ks[17], D_MODEL),
        "w_ffn2_gate": nrm(ks[18], (DEPTH, D_MODEL, D_FF), jnp.float32) * D_MODEL ** -0.5,
        "w_ffn2_up": nrm(ks[19], (DEPTH, D_MODEL, D_FF), jnp.float32) * D_MODEL ** -0.5,
        "w_ffn2_down": nrm(ks[20], (DEPTH, D_FF, D_MODEL), jnp.float32) * D_FF ** -0.5,
        "g_final": 1.0 + 0.02 * nrm(ks[21], (D_MODEL,), jnp.float32),
    }


def reference(x_prompt, x_sample, state_mlstm_C, state_mlstm_n, state_mlstm_m, state_ret_R,
              g_ffn1, w_ffn1_gate, w_ffn1_up, w_ffn1_down, g_mix, w_in, b_gates, g_gn_mlstm, g_gn_ret,
              w_out, g_ffn2, w_ffn2_gate, w_ffn2_up, w_ffn2_down, g_final):
    f32 = jnp.float32
    Bp, Sp = x_prompt.shape[0], x_prompt.shape[1]
    pos_p = jnp.arange(Sp)
    pos_s = PAST_LEN + jnp.arange(x_sample.shape[1])
    hp, hs = x_prompt, x_sample
    new_p, new_s = [], []
    for l in range(DEPTH):
        w = (g_ffn1[l], w_ffn1_gate[l], w_ffn1_up[l], w_ffn1_down[l], g_mix[l], w_in[l], b_gates[l],
             g_gn_mlstm[l], g_gn_ret[l], w_out[l], g_ffn2[l], w_ffn2_gate[l], w_ffn2_up[l], w_ffn2_down[l])
        hp, sp = trunk_layer(hp, pos_p,
                             jnp.zeros((Bp, H_M, DH_M, DH_M), f32), jnp.zeros((Bp, H_M, DH_M), f32),
                             jnp.zeros((Bp, H_M), f32), jnp.zeros((Bp, H_R, DH_R, DH_R), f32), *w)
        hs, ss = trunk_layer(hs, pos_s,
                             state_mlstm_C[l].astype(f32), state_mlstm_n[l].astype(f32),
                             state_mlstm_m[l].astype(f32), state_ret_R[l].astype(f32), *w)
        new_p.append(sp)
        new_s.append(ss)
    y_prompt = rms_norm(hp, g_final)
    y_sample = rms_norm(hs, g_final)
    C_p = jnp.stack([s[0] for s in new_p])
    n_p = jnp.stack([s[1] for s in new_p])
    m_p = jnp.stack([s[2] for s in new_p])
    R_p = jnp.stack([s[3] for s in new_p])
    C_s = jnp.stack([s[0] for s in new_s])
    n_s = jnp.stack([s[1] for s in new_s])
    m_s = jnp.stack([s[2] for s in new_s])
    R_s = jnp.stack([s[3] for s in new_s])
    return (y_prompt, y_sample, C_p, n_p, m_p, R_p, C_s, n_s, m_s, R_s)
```

```python
import functools

import numpy as np
import jax
import jax.numpy as jnp
from jax import lax
from jax.experimental import pallas as pl
from jax.experimental.pallas import tpu as pltpu

F32 = jnp.float32
BF16 = jnp.bfloat16

D_MODEL = 1024
N_HEADS = 4
D_HEAD = 128
D_GROUP = N_HEADS * D_HEAD
D_FF = 2816
PAST_LEN = 4096
ROPE_BASE = 10000.0
EPS = 1e-6
GN_EPS = 1e-5
LOG_GAMMA = np.log(1.0 - 2.0 ** (-5.0 - np.arange(N_HEADS))).astype(np.float32)

LANES = 128
MXU_DIM = 256
FF_CHUNK = MXU_DIM
N_FF_CHUNKS = D_FF // FF_CHUNK
CHUNK = 128
TOKEN_TILE = 512
N_QKV = 8 * D_GROUP
GATE_COLS = 2 * LANES
VMEM_LIMIT = 58 * 1024 * 1024


def _rms_norm(x, g):
    ms = jnp.mean(x * x, axis=-1, keepdims=True)
    return x * lax.rsqrt(ms + EPS) * g


def _swiglu_into(acc_ref, xn, wgu_ref, wd_ref):
    for c in range(N_FF_CHUNKS):
        gu = jnp.dot(xn, wgu_ref[c], preferred_element_type=F32)
        g = gu[:, :FF_CHUNK]
        u = gu[:, FF_CHUNK:]
        a = (g * jax.nn.sigmoid(g) * u).astype(BF16)
        d = jnp.dot(a, wd_ref[c], preferred_element_type=F32)
        if c == 0:
            acc_ref[...] = d
        else:
            acc_ref[...] += d


def _rope_heads(p, cos2, sin2):
    outs = []
    for h in range(N_HEADS):
        xh = p[:, h * D_HEAD:(h + 1) * D_HEAD]
        outs.append(xh * cos2 + pltpu.roll(xh, D_HEAD // 2, axis=1) * sin2)
    return jnp.concatenate(outs, axis=1)


def _ffn_in_kernel(x_ref, g1_ref, wgu_ref, wd_ref, gmix_ref, win_ref, wgate_ref, cos_ref, sin_ref,
                   x1_ref, qkv_ref, gates_ref, acc_ref):
    x = x_ref[...]
    xn = _rms_norm(x, g1_ref[...]).astype(BF16)
    _swiglu_into(acc_ref, xn, wgu_ref, wd_ref)
    x1 = x + 0.5 * acc_ref[...]
    x1_ref[...] = x1
    hn = _rms_norm(x1, gmix_ref[...]).astype(BF16)
    cos2 = cos_ref[...]
    sin2 = sin_ref[...]
    key_scale = D_HEAD ** -0.5
    for j in range(8):
        cols = slice(j * D_GROUP, (j + 1) * D_GROUP)
        p = jnp.dot(hn, win_ref[:, cols], preferred_element_type=F32)
        if j in (4, 5):
            p = _rope_heads(p, cos2, sin2)
        if j in (1, 5):
            p = p * key_scale
        qkv_ref[:, cols] = p.astype(BF16)
    gates_ref[...] = jnp.dot(hn, wgate_ref[...], preferred_element_type=F32)


def _ffn_out_kernel(x1_ref, y_ref, wout_ref, g2_ref, wgu_ref, wd_ref, gfin_ref, out_ref, acc_ref, x2_ref):
    x2 = x1_ref[...] + jnp.dot(y_ref[...], wout_ref[...], preferred_element_type=F32)
    x2_ref[...] = x2
    xn = _rms_norm(x2, g2_ref[...]).astype(BF16)
    _swiglu_into(acc_ref, xn, wgu_ref, wd_ref)
    x3 = x2_ref[...] + 0.5 * acc_ref[...]
    out_ref[...] = _rms_norm(x3, gfin_ref[...])


def _const_spec(shape):
    nd = len(shape)
    return pl.BlockSpec(shape, lambda *_: (0,) * nd, pipeline_mode=pl.Buffered(1))


def _ffn_in(x, g1, wgu, wd, gmix, win, wgate, cos2, sin2):
    n_tok = x.shape[0]
    tm = TOKEN_TILE
    pos_tiles = cos2.shape[0] // tm
    row = lambda i: (i, 0)
    pos = lambda i: (i % pos_tiles, 0)
    return pl.pallas_call(
        _ffn_in_kernel,
        grid=(n_tok // tm,),
        in_specs=[
            pl.BlockSpec((tm, D_MODEL), row),
            _const_spec((1, D_MODEL)),
            _const_spec(wgu.shape),
            _const_spec(wd.shape),
            _const_spec((1, D_MODEL)),
            _const_spec(win.shape),
            _const_spec(wgate.shape),
            pl.BlockSpec((tm, LANES), pos),
            pl.BlockSpec((tm, LANES), pos),
        ],
        out_specs=[
            pl.BlockSpec((tm, D_MODEL), row),
            pl.BlockSpec((tm, N_QKV), row),
            pl.BlockSpec((tm, GATE_COLS), row),
        ],
        out_shape=[
            jax.ShapeDtypeStruct((n_tok, D_MODEL), F32),
            jax.ShapeDtypeStruct((n_tok, N_QKV), BF16),
            jax.ShapeDtypeStruct((n_tok, GATE_COLS), F32),
        ],
        scratch_shapes=[pltpu.VMEM((tm, D_MODEL), F32)],
        compiler_params=pltpu.CompilerParams(
            dimension_semantics=("arbitrary",), vmem_limit_bytes=VMEM_LIMIT),
        name="ffn_in",
    )(x, g1, wgu, wd, gmix, win, wgate, cos2, sin2)


def _ffn_out(x1, y, wout, g2, wgu, wd, gfin):
    n_tok = x1.shape[0]
    tm = TOKEN_TILE
    row = lambda i: (i, 0)
    return pl.pallas_call(
        _ffn_out_kernel,
        grid=(n_tok // tm,),
        in_specs=[
            pl.BlockSpec((tm, D_MODEL), row),
            pl.BlockSpec((tm, D_MODEL), row),
            _const_spec(wout.shape),
            _const_spec((1, D_MODEL)),
            _const_spec(wgu.shape),
            _const_spec(wd.shape),
            _const_spec((1, D_MODEL)),
        ],
        out_specs=pl.BlockSpec((tm, D_MODEL), row),
        out_shape=jax.ShapeDtypeStruct((n_tok, D_MODEL), F32),
        scratch_shapes=[pltpu.VMEM((tm, D_MODEL), F32), pltpu.VMEM((tm, D_MODEL), F32)],
        compiler_params=pltpu.CompilerParams(
            dimension_semantics=("arbitrary",), vmem_limit_bytes=VMEM_LIMIT),
        name="ffn_out",
    )(x1, y, wout, g2, wgu, wd, gfin)


def _scan_rows(x, op, fill, row_idx):
    d = 1
    while d < x.shape[0]:
        shifted = pltpu.roll(x, d, axis=0)
        x = op(x, jnp.where(row_idx >= d, shifted, fill))
        d *= 2
    return x


def _log_sigmoid(x):
    return jnp.minimum(x, 0.0) - jnp.log1p(jnp.exp(-jnp.abs(x)))


def _group_norm(h, g):
    mu = jnp.mean(h, axis=-1, keepdims=True)
    d = h - mu
    var = jnp.mean(d * d, axis=-1, keepdims=True)
    return d * lax.rsqrt(var + GN_EPS) * g


_NT = (((1,), (1,)), ((), ()))
_TN = (((0,), (0,)), ((), ()))


def _mixer_kernel(qkv_ref, gates_ref, c0_ref, n0_ref, m0_ref, r0_ref, bi_ref, bf_ref, gnm_ref, gnr_ref,
                  decay_ref, cross_ref, tail_ref, rtot_ref,
                  y_ref, c_ref, n_ref, m_ref, r_ref, *, n_chunks, valid_len):
    L = CHUNK

    @pl.when(pl.program_id(1) == 0)
    def _():
        c_ref[...] = c0_ref[...]
        n_ref[...] = n0_ref[...]
        m_ref[...] = m0_ref[...]
        r_ref[...] = r0_ref[...]

    row_idx = lax.broadcasted_iota(jnp.int32, (L, LANES), 0)
    col_idx = lax.broadcasted_iota(jnp.int32, (L, LANES), 1)
    causal = row_idx >= col_idx
    valid = row_idx < valid_len
    neg_inf = jnp.float32(-jnp.inf)

    def chunk_step(c, carry):
        rows = pl.ds(pl.multiple_of(c * L, L), L)
        ig = gates_ref[0, rows, 0:LANES] + bi_ref[...]
        lf = _log_sigmoid(gates_ref[0, rows, LANES:2 * LANES] + bf_ref[...])
        lf = jnp.where(valid, lf, 0.0)
        b = _scan_rows(lf, jnp.add, 0.0, row_idx)
        u = jnp.where(valid, ig - b, neg_inf)
        m_prev = m_ref[0]
        big_m = jnp.maximum(_scan_rows(u, jnp.maximum, neg_inf, row_idx), m_prev)
        a = jnp.exp(m_prev - big_m)
        m_t = b + big_m
        floor = jnp.exp(-m_t)
        m_last = big_m[L - 1:L, :]
        a_last = a[L - 1:L, :]
        w_last = jnp.exp(u - m_last)
        u_t = u.T
        m_ref[0] = m_t[L - 1:L, :]

        for h in range(N_HEADS):
            hc = slice(h * D_HEAD, (h + 1) * D_HEAD)
            q = qkv_ref[0, rows, 0 * D_GROUP + h * D_HEAD:0 * D_GROUP + (h + 1) * D_HEAD]
            k = qkv_ref[0, rows, 1 * D_GROUP + h * D_HEAD:1 * D_GROUP + (h + 1) * D_HEAD]
            v = qkv_ref[0, rows, 2 * D_GROUP + h * D_HEAD:2 * D_GROUP + (h + 1) * D_HEAD]
            o = qkv_ref[0, rows, 3 * D_GROUP + h * D_HEAD:3 * D_GROUP + (h + 1) * D_HEAD]
            qk = lax.dot_general(q, k, _NT, preferred_element_type=F32)
            w = jnp.where(causal, jnp.exp(u_t[h:h + 1, :] - big_m[:, h:h + 1]), 0.0)
            s = qk * w
            c_old = c_ref[0, h]
            n_old = n_ref[0, h:h + 1, :]
            a_col = a[:, h:h + 1]
            num = a_col * jnp.dot(q, c_old.astype(BF16), preferred_element_type=F32) \
                + jnp.dot(s.astype(BF16), v, preferred_element_type=F32)
            den = a_col * jnp.sum(q.astype(F32) * n_old, axis=-1, keepdims=True) \
                + jnp.sum(s, axis=-1, keepdims=True)
            hh = num / jnp.maximum(jnp.abs(den), floor[:, h:h + 1])
            kw = k.astype(F32) * w_last[:, h:h + 1]
            al = a_last[:, h:h + 1]
            c_ref[0, h] = al * c_old + lax.dot_general(kw.astype(BF16), v, _TN, preferred_element_type=F32)
            n_ref[0, h:h + 1, :] = al * n_old + jnp.sum(kw, axis=0, keepdims=True)
            y = jax.nn.sigmoid(o.astype(F32)) * _group_norm(hh, gnm_ref[:, hc])
            y_ref[0, rows, hc] = y.astype(BF16)

        for h in range(N_HEADS):
            hc = slice(h * D_HEAD, (h + 1) * D_HEAD)
            q = qkv_ref[0, rows, 4 * D_GROUP + h * D_HEAD:4 * D_GROUP + (h + 1) * D_HEAD]
            k = qkv_ref[0, rows, 5 * D_GROUP + h * D_HEAD:5 * D_GROUP + (h + 1) * D_HEAD]
            v = qkv_ref[0, rows, 6 * D_GROUP + h * D_HEAD:6 * D_GROUP + (h + 1) * D_HEAD]
            g = qkv_ref[0, rows, 7 * D_GROUP + h * D_HEAD:7 * D_GROUP + (h + 1) * D_HEAD]
            scores = lax.dot_general(q, k, _NT, preferred_element_type=F32) * decay_ref[h]
            r_old = r_ref[0, h]
            out = jnp.dot(scores.astype(BF16), v, preferred_element_type=F32) \
                + cross_ref[:, h:h + 1] * jnp.dot(q, r_old.astype(BF16), preferred_element_type=F32)
            kt = (k.astype(F32) * tail_ref[:, h:h + 1]).astype(BF16)
            r_ref[0, h] = rtot_ref[:, h:h + 1] * r_old + lax.dot_general(kt, v, _TN, preferred_element_type=F32)
            gf = g.astype(F32)
            y = gf * jax.nn.sigmoid(gf) * _group_norm(out, gnr_ref[:, hc])
            y_ref[0, rows, D_GROUP + h * D_HEAD:D_GROUP + (h + 1) * D_HEAD] = y.astype(BF16)
        return carry

    lax.fori_loop(0, n_chunks, chunk_step, 0)


def _mixer(qkv, gates, c0, n0, m0, r0, bi, bf, gnm, gnr, valid_len, seq_tile):
    bsz, seq = qkv.shape[0], qkv.shape[1]
    n_chunks = seq_tile // CHUNK
    idx = np.arange(CHUNK, dtype=np.float32)
    diff = idx[:, None] - idx[None, :]
    lg = LOG_GAMMA
    decay = np.where(diff >= 0, np.exp(lg[:, None, None] * np.maximum(diff, 0.0)), 0.0).astype(np.float32)
    cross = np.zeros((CHUNK, LANES), np.float32)
    cross[:, :N_HEADS] = np.exp(lg[None, :] * (idx[:, None] + 1.0))
    tail = np.zeros((CHUNK, LANES), np.float32)
    tail[:, :N_HEADS] = np.where(idx[:, None] < valid_len, np.exp(lg[None, :] * (valid_len - 1.0 - idx[:, None])), 0.0)
    rtot = np.zeros((1, LANES), np.float32)
    rtot[0, :N_HEADS] = np.exp(lg * np.float32(valid_len))

    seq_blk = lambda b, j: (b, j, 0)
    per_b4 = lambda b, j: (b, 0, 0, 0)
    per_b3 = lambda b, j: (b, 0, 0)
    const2 = lambda b, j: (0, 0)
    const3 = lambda b, j: (0, 0, 0)
    kern = functools.partial(_mixer_kernel, n_chunks=n_chunks, valid_len=valid_len)
    return pl.pallas_call(
        kern,
        grid=(bsz, seq // seq_tile),
        in_specs=[
            pl.BlockSpec((1, seq_tile, N_QKV), seq_blk),
            pl.BlockSpec((1, seq_tile, GATE_COLS), seq_blk),
            pl.BlockSpec((1, N_HEADS, D_HEAD, D_HEAD), per_b4),
            pl.BlockSpec((1, N_HEADS, D_HEAD), per_b3),
            pl.BlockSpec((1, 1, LANES), per_b3),
            pl.BlockSpec((1, N_HEADS, D_HEAD, D_HEAD), per_b4),
            pl.BlockSpec((1, LANES), const2),
            pl.BlockSpec((1, LANES), const2),
            pl.BlockSpec((1, D_GROUP), const2),
            pl.BlockSpec((1, D_GROUP), const2),
            pl.BlockSpec((N_HEADS, CHUNK, CHUNK), const3),
            pl.BlockSpec((CHUNK, LANES), const2),
            pl.BlockSpec((CHUNK, LANES), const2),
            pl.BlockSpec((1, LANES), const2),
        ],
        out_specs=[
            pl.BlockSpec((1, seq_tile, 2 * D_GROUP), seq_blk),
            pl.BlockSpec((1, N_HEADS, D_HEAD, D_HEAD), per_b4),
            pl.BlockSpec((1, N_HEADS, D_HEAD), per_b3),
            pl.BlockSpec((1, 1, LANES), per_b3),
            pl.BlockSpec((1, N_HEADS, D_HEAD, D_HEAD), per_b4),
        ],
        out_shape=[
            jax.ShapeDtypeStruct((bsz, seq, 2 * D_GROUP), BF16),
            jax.ShapeDtypeStruct((bsz, N_HEADS, D_HEAD, D_HEAD), F32),
            jax.ShapeDtypeStruct((bsz, N_HEADS, D_HEAD), F32),
            jax.ShapeDtypeStruct((bsz, 1, LANES), F32),
            jax.ShapeDtypeStruct((bsz, N_HEADS, D_HEAD, D_HEAD), F32),
        ],
        compiler_params=pltpu.CompilerParams(
            dimension_semantics=("arbitrary", "arbitrary"), vmem_limit_bytes=VMEM_LIMIT),
        name="mixer",
    )(qkv, gates, c0, n0, m0, r0, bi, bf, gnm, gnr,
      jnp.asarray(decay), jnp.asarray(cross), jnp.asarray(tail), jnp.asarray(rtot))


def _rope_tables(pos):
    half = D_HEAD // 2
    inv = ROPE_BASE ** (-jnp.arange(half, dtype=F32) / half)
    ang = pos.astype(F32)[:, None] * inv[None, :]
    cos, sin = jnp.cos(ang), jnp.sin(ang)
    return jnp.concatenate([cos, cos], axis=-1), jnp.concatenate([-sin, sin], axis=-1)


def _pad_lanes(a):
    return jnp.pad(a, [(0, 0)] * (a.ndim - 1) + [(0, LANES - a.shape[-1])])


def _ffn_weights(wg, wu, wd):
    wg = wg.astype(BF16).reshape(D_MODEL, N_FF_CHUNKS, FF_CHUNK)
    wu = wu.astype(BF16).reshape(D_MODEL, N_FF_CHUNKS, FF_CHUNK)
    wgu = jnp.transpose(jnp.concatenate([wg, wu], axis=-1), (1, 0, 2))
    return wgu, wd.astype(BF16).reshape(N_FF_CHUNKS, FF_CHUNK, D_MODEL)


def _layer(x, pos_tab, state, weights, valid_len, seq_tile):
    (g1, wgu1, wd1, gmix, win, wgate, bi, bf, gnm, gnr, wout, g2, wgu2, wd2, gfin) = weights
    bsz, seq, _ = x.shape
    c0, n0, m0, r0 = state
    cos2, sin2 = pos_tab
    x1, qkv, gates = _ffn_in(x.reshape(bsz * seq, D_MODEL), g1, wgu1, wd1, gmix, win, wgate, cos2, sin2)
    qkv = qkv.reshape(bsz, seq, N_QKV)
    gates = gates.reshape(bsz, seq, GATE_COLS)
    if seq % seq_tile:
        pad = seq_tile - seq % seq_tile
        qkv = jnp.pad(qkv, ((0, 0), (0, pad), (0, 0)))
        gates = jnp.pad(gates, ((0, 0), (0, pad), (0, 0)))
    y, c_new, n_new, m_new, r_new = _mixer(qkv, gates, c0, n0, _pad_lanes(m0)[:, None, :], r0,
                                           bi, bf, gnm, gnr, valid_len, seq_tile)
    y = y[:, :seq].reshape(bsz * seq, D_MODEL)
    out = _ffn_out(x1, y, wout, g2, wgu2, wd2, gfin)
    return out.reshape(bsz, seq, D_MODEL), (c_new, n_new, m_new[:, 0, :N_HEADS], r_new)


def kernel(x_prompt, x_sample, state_mlstm_C, state_mlstm_n, state_mlstm_m, state_ret_R,
           g_ffn1, w_ffn1_gate, w_ffn1_up, w_ffn1_down, g_mix, w_in, b_gates, g_gn_mlstm, g_gn_ret,
           w_out, g_ffn2, w_ffn2_gate, w_ffn2_up, w_ffn2_down, g_final):
    depth = g_ffn1.shape[0]
    bp, sp, _ = x_prompt.shape
    bs, ss, _ = x_sample.shape
    assert depth == 1, "the final norm is fused into the layer's last kernel"
    assert ss <= CHUNK and TOKEN_TILE % ss == 0 and sp % (4 * CHUNK) == 0
    pos_p = _rope_tables(jnp.arange(sp))
    cos_s, sin_s = _rope_tables(PAST_LEN + jnp.arange(ss))
    pos_s = (jnp.tile(cos_s, (TOKEN_TILE // ss, 1)), jnp.tile(sin_s, (TOKEN_TILE // ss, 1)))

    hp, hs = x_prompt, x_sample
    new_p, new_s = [], []
    for l in range(depth):
        wgu1, wd1 = _ffn_weights(w_ffn1_gate[l], w_ffn1_up[l], w_ffn1_down[l])
        wgu2, wd2 = _ffn_weights(w_ffn2_gate[l], w_ffn2_up[l], w_ffn2_down[l])
        win = w_in[l][:, :N_QKV].astype(BF16)
        wg_cols = w_in[l][:, N_QKV:].astype(BF16)
        wgate = jnp.concatenate([_pad_lanes(wg_cols[:, :N_HEADS]), _pad_lanes(wg_cols[:, N_HEADS:])], axis=-1)
        bi = _pad_lanes(b_gates[l][None, :N_HEADS].astype(F32))
        bf = _pad_lanes(b_gates[l][None, N_HEADS:].astype(F32))
        weights = (g_ffn1[l][None, :], wgu1, wd1, g_mix[l][None, :], win, wgate, bi, bf,
                   g_gn_mlstm[l][None, :].astype(F32), g_gn_ret[l][None, :].astype(F32),
                   w_out[l].astype(BF16), g_ffn2[l][None, :], wgu2, wd2, g_final[None, :])
        zero_state = (jnp.zeros((bp, N_HEADS, D_HEAD, D_HEAD), F32), jnp.zeros((bp, N_HEADS, D_HEAD), F32),
                      jnp.zeros((bp, N_HEADS), F32), jnp.zeros((bp, N_HEADS, D_HEAD, D_HEAD), F32))
        hp, st_p = _layer(hp, pos_p, zero_state, weights, CHUNK, 4 * CHUNK)
        carried = (state_mlstm_C[l].astype(F32), state_mlstm_n[l].astype(F32),
                   state_mlstm_m[l].astype(F32), state_ret_R[l].astype(F32))
        hs, st_s = _layer(hs, pos_s, carried, weights, min(ss, CHUNK), CHUNK)
        new_p.append(st_p)
        new_s.append(st_s)
    stack = lambda sts, i: jnp.stack([s[i] for s in sts])
    return (hp, hs,
            stack(new_p, 0), stack(new_p, 1), stack(new_p, 2), stack(new_p, 3),
            stack(new_s, 0), stack(new_s, 1), stack(new_s, 2), stack(new_s, 3))
```

```python
import functools

import numpy as np
import jax
import jax.numpy as jnp
from jax import lax
from jax.experimental import pallas as pl
from jax.experimental.pallas import tpu as pltpu

F32 = jnp.float32
BF16 = jnp.bfloat16

D_MODEL = 1024
N_HEADS = 4
D_HEAD = 128
D_GROUP = N_HEADS * D_HEAD
D_FF = 2816
PAST_LEN = 4096
ROPE_BASE = 10000.0
EPS = 1e-6
GN_EPS = 1e-5
LOG_GAMMA = np.log(1.0 - 2.0 ** (-5.0 - np.arange(N_HEADS))).astype(np.float32)

LANES = 128
MXU_DIM = 256
FF_CHUNK = MXU_DIM
N_FF_CHUNKS = D_FF // FF_CHUNK
CHUNK = 128
TOKEN_TILE = 512
N_QKV = 8 * D_GROUP
GATE_COLS = 2 * LANES
VMEM_LIMIT = 58 * 1024 * 1024


def _rms_norm(x, g):
    ms = jnp.mean(x * x, axis=-1, keepdims=True)
    return x * lax.rsqrt(ms + EPS) * g


def _swiglu_into(acc_ref, xn, wgu_ref, wd_ref):
    for c in range(N_FF_CHUNKS):
        gu = jnp.dot(xn, wgu_ref[c], preferred_element_type=F32)
        g = gu[:, :FF_CHUNK]
        u = gu[:, FF_CHUNK:]
        a = (g * jax.nn.sigmoid(g) * u).astype(BF16)
        d = jnp.dot(a, wd_ref[c], preferred_element_type=F32)
        if c == 0:
            acc_ref[...] = d
        else:
            acc_ref[...] += d


def _rope_heads(p, cos2, sin2):
    outs = []
    for h in range(N_HEADS):
        xh = p[:, h * D_HEAD:(h + 1) * D_HEAD]
        outs.append(xh * cos2 + pltpu.roll(xh, D_HEAD // 2, axis=1) * sin2)
    return jnp.concatenate(outs, axis=1)


def _ffn_in_kernel(x_ref, g1_ref, wgu_ref, wd_ref, gmix_ref, win_ref, wgate_ref, cos_ref, sin_ref,
                   x1_ref, qkv_ref, gates_ref, acc_ref):
    x = x_ref[...]
    xn = _rms_norm(x, g1_ref[...]).astype(BF16)
    _swiglu_into(acc_ref, xn, wgu_ref, wd_ref)
    x1 = x + 0.5 * acc_ref[...]
    x1_ref[...] = x1
    hn = _rms_norm(x1, gmix_ref[...]).astype(BF16)
    cos2 = cos_ref[...]
    sin2 = sin_ref[...]
    key_scale = D_HEAD ** -0.5
    for j in range(8):
        cols = slice(j * D_GROUP, (j + 1) * D_GROUP)
        p = jnp.dot(hn, win_ref[:, cols], preferred_element_type=F32)
        if j in (4, 5):
            p = _rope_heads(p, cos2, sin2)
        if j in (1, 5):
            p = p * key_scale
        qkv_ref[:, cols] = p.astype(BF16)
    gates_ref[...] = jnp.dot(hn, wgate_ref[...], preferred_element_type=F32)


def _ffn_out_kernel(x1_ref, y_ref, wout_ref, g2_ref, wgu_ref, wd_ref, gfin_ref, out_ref, acc_ref, x2_ref):
    x2 = x1_ref[...] + jnp.dot(y_ref[...], wout_ref[...], preferred_element_type=F32)
    x2_ref[...] = x2
    xn = _rms_norm(x2, g2_ref[...]).astype(BF16)
    _swiglu_into(acc_ref, xn, wgu_ref, wd_ref)
    x3 = x2_ref[...] + 0.5 * acc_ref[...]
    out_ref[...] = _rms_norm(x3, gfin_ref[...])


def _const_spec(shape):
    nd = len(shape)
    return pl.BlockSpec(shape, lambda *_: (0,) * nd, pipeline_mode=pl.Buffered(1))


def _ffn_in(x, g1, wgu, wd, gmix, win, wgate, cos2, sin2):
    n_tok = x.shape[0]
    tm = TOKEN_TILE
    pos_tiles = cos2.shape[0] // tm
    row = lambda i: (i, 0)
    pos = lambda i: (i % pos_tiles, 0)
    return pl.pallas_call(
        _ffn_in_kernel,
        grid=(n_tok // tm,),
        in_specs=[
            pl.BlockSpec((tm, D_MODEL), row),
            _const_spec((1, D_MODEL)),
            _const_spec(wgu.shape),
            _const_spec(wd.shape),
            _const_spec((1, D_MODEL)),
            _const_spec(win.shape),
            _const_spec(wgate.shape),
            pl.BlockSpec((tm, LANES), pos),
            pl.BlockSpec((tm, LANES), pos),
        ],
        out_specs=[
            pl.BlockSpec((tm, D_MODEL), row),
            pl.BlockSpec((tm, N_QKV), row),
            pl.BlockSpec((tm, GATE_COLS), row),
        ],
        out_shape=[
            jax.ShapeDtypeStruct((n_tok, D_MODEL), F32),
            jax.ShapeDtypeStruct((n_tok, N_QKV), BF16),
            jax.ShapeDtypeStruct((n_tok, GATE_COLS), F32),
        ],
        scratch_shapes=[pltpu.VMEM((tm, D_MODEL), F32)],
        compiler_params=pltpu.CompilerParams(
            dimension_semantics=("arbitrary",), vmem_limit_bytes=VMEM_LIMIT),
        name="ffn_in",
    )(x, g1, wgu, wd, gmix, win, wgate, cos2, sin2)


def _ffn_out(x1, y, wout, g2, wgu, wd, gfin):
    n_tok = x1.shape[0]
    tm = TOKEN_TILE
    row = lambda i: (i, 0)
    return pl.pallas_call(
        _ffn_out_kernel,
        grid=(n_tok // tm,),
        in_specs=[
            pl.BlockSpec((tm, D_MODEL), row),
            pl.BlockSpec((tm, D_MODEL), row),
            _const_spec(wout.shape),
            _const_spec((1, D_MODEL)),
            _const_spec(wgu.shape),
            _const_spec(wd.shape),
            _const_spec((1, D_MODEL)),
        ],
        out_specs=pl.BlockSpec((tm, D_MODEL), row),
        out_shape=jax.ShapeDtypeStruct((n_tok, D_MODEL), F32),
        scratch_shapes=[pltpu.VMEM((tm, D_MODEL), F32), pltpu.VMEM((tm, D_MODEL), F32)],
        compiler_params=pltpu.CompilerParams(
            dimension_semantics=("arbitrary",), vmem_limit_bytes=VMEM_LIMIT),
        name="ffn_out",
    )(x1, y, wout, g2, wgu, wd, gfin)


def _scan_rows(x, op, fill, row_idx):
    d = 1
    while d < x.shape[0]:
        shifted = pltpu.roll(x, d, axis=0)
        x = op(x, jnp.where(row_idx >= d, shifted, fill))
        d *= 2
    return x


def _log_sigmoid(x):
    return jnp.minimum(x, 0.0) - jnp.log1p(jnp.exp(-jnp.abs(x)))


def _group_norm(h, g):
    mu = jnp.mean(h, axis=-1, keepdims=True)
    d = h - mu
    var = jnp.mean(d * d, axis=-1, keepdims=True)
    return d * lax.rsqrt(var + GN_EPS) * g


def _split3(x):
    hi = x.astype(BF16)
    r = x - hi.astype(F32)
    mid = r.astype(BF16)
    lo = (r - mid.astype(F32)).astype(BF16)
    return jnp.concatenate([hi, mid, lo], axis=1)


_NT = (((1,), (1,)), ((), ()))
_TN = (((0,), (0,)), ((), ()))
MASKED_U = -1e30
N_SCAN = 3


def _mixer_kernel(qkv_ref, gates_ref, s0_ref, m0_ref, r0_ref, bi_ref, bf_ref, gnm_ref, gnr_ref,
                  decay_ref, cross_ref, tail_ref, sel_ref,
                  y_ref, s_ref, m_ref, r_ref, *, n_chunks, valid_len, ret_total):
    L = CHUNK

    @pl.when(pl.program_id(1) == 0)
    def _():
        s_ref[...] = s0_ref[...]
        m_ref[...] = m0_ref[...]
        r_ref[...] = r0_ref[...]

    row_idx = lax.broadcasted_iota(jnp.int32, (L, LANES), 0)
    col_idx = lax.broadcasted_iota(jnp.int32, (L, LANES), 1)
    causal = row_idx >= col_idx
    valid = row_idx < valid_len
    ones_blk = jnp.ones((L, D_HEAD), BF16)

    def chunk_step(c, carry):
        rows = pl.ds(pl.multiple_of(c * L, L), L)
        ig = gates_ref[0, rows, 0:LANES] + bi_ref[...]
        lf = _log_sigmoid(gates_ref[0, rows, LANES:2 * LANES] + bf_ref[...])
        lf = jnp.where(valid, lf, 0.0)
        b = _scan_rows(lf, jnp.add, 0.0, row_idx)
        u = jnp.where(valid, ig - b, MASKED_U)
        cm = _scan_rows(u, jnp.maximum, MASKED_U, row_idx)
        packed = jnp.where(col_idx < N_HEADS, cm, jnp.where(col_idx < 2 * N_HEADS, b, u))
        packed = jnp.where(col_idx < N_SCAN * N_HEADS, packed, 0.0)
        rep = jnp.dot(_split3(packed), sel_ref[...], preferred_element_type=F32)
        u_t = u.T

        def head_slices(group, h):
            base = group * 4 * D_GROUP + h * D_HEAD
            return [qkv_ref[0, rows, base + j * D_GROUP:base + j * D_GROUP + D_HEAD] for j in range(4)]

        qk_m, inter_m, qk_r, inter_r = [], [], [], []
        for h in range(N_HEADS):
            q, k, _, _ = head_slices(0, h)
            qk_m.append(lax.dot_general(q, k, _NT, preferred_element_type=F32))
            inter_m.append(jnp.dot(q, s_ref[0, h].astype(BF16), preferred_element_type=F32))
        for h in range(N_HEADS):
            q, k, _, _ = head_slices(1, h)
            qk_r.append(lax.dot_general(q, k, _NT, preferred_element_type=F32))
            inter_r.append(jnp.dot(q, r_ref[0, h].astype(BF16), preferred_element_type=F32))

        p_m, kw_m, gate_m, p_r, kt_r = [], [], [], [], []
        for h in range(N_HEADS):
            cm_b = rep[:, (0 * N_HEADS + h) * LANES:(0 * N_HEADS + h + 1) * LANES]
            b_b = rep[:, (1 * N_HEADS + h) * LANES:(1 * N_HEADS + h + 1) * LANES]
            u_b = rep[:, (2 * N_HEADS + h) * LANES:(2 * N_HEADS + h + 1) * LANES]
            m_prev = m_ref[0, h:h + 1, :]
            big_m = jnp.maximum(cm_b, m_prev)
            a_b = jnp.exp(m_prev - big_m)
            m_t = b_b + big_m
            m_last = big_m[L - 1:L, :]
            m_ref[0, h:h + 1, :] = m_t[L - 1:L, :]
            w = jnp.where(causal, jnp.exp(u_t[h:h + 1, :] - big_m), 0.0)
            p_m.append((qk_m[h] * w).astype(BF16))
            k = head_slices(0, h)[1]
            kw_m.append((k.astype(F32) * jnp.exp(u_b - m_last)).astype(BF16))
            gate_m.append((a_b, m_t))
        for h in range(N_HEADS):
            p_r.append((qk_r[h] * decay_ref[h]).astype(BF16))
            k = head_slices(1, h)[1]
            kt_r.append((k.astype(F32) * tail_ref[h]).astype(BF16))

        intra_m, upd_m, intra_r, upd_r = [], [], [], []
        for h in range(N_HEADS):
            v = head_slices(0, h)[2]
            v_aug = jnp.concatenate([v, ones_blk], axis=1)
            intra_m.append(jnp.dot(p_m[h], v_aug, preferred_element_type=F32))
            upd_m.append(lax.dot_general(kw_m[h], v_aug, _TN, preferred_element_type=F32))
        for h in range(N_HEADS):
            v = head_slices(1, h)[2]
            intra_r.append(jnp.dot(p_r[h], v, preferred_element_type=F32))
            upd_r.append(lax.dot_general(kt_r[h], v, _TN, preferred_element_type=F32))

        for h in range(N_HEADS):
            hc = slice(h * D_HEAD, (h + 1) * D_HEAD)
            a_b, m_t = gate_m[h]
            a_last = a_b[L - 1:L, :]
            num = a_b * inter_m[h][:, :D_HEAD] + intra_m[h][:, :D_HEAD]
            den = a_b * inter_m[h][:, D_HEAD:] + intra_m[h][:, D_HEAD:]
            hh = num / jnp.maximum(jnp.abs(den), jnp.exp(-m_t))
            s_ref[0, h] = jnp.concatenate([a_last, a_last], axis=1) * s_ref[0, h] + upd_m[h]
            o = head_slices(0, h)[3]
            y = jax.nn.sigmoid(o.astype(F32)) * _group_norm(hh, gnm_ref[:, hc])
            y_ref[0, rows, hc] = y.astype(BF16)
        for h in range(N_HEADS):
            hc = slice(h * D_HEAD, (h + 1) * D_HEAD)
            out = intra_r[h] + cross_ref[h] * inter_r[h]
            r_ref[0, h] = ret_total[h] * r_ref[0, h] + upd_r[h]
            gf = head_slices(1, h)[3].astype(F32)
            y = gf * jax.nn.sigmoid(gf) * _group_norm(out, gnr_ref[:, hc])
            y_ref[0, rows, D_GROUP + h * D_HEAD:D_GROUP + (h + 1) * D_HEAD] = y.astype(BF16)
        return carry

    lax.fori_loop(0, n_chunks, chunk_step, 0)


def _mixer(qkv, gates, s0, m0, r0, bi, bf, gnm, gnr, valid_len, seq_tile):
    bsz, seq = qkv.shape[0], qkv.shape[1]
    n_chunks = seq_tile // CHUNK
    idx = np.arange(CHUNK, dtype=np.float32)
    diff = idx[:, None] - idx[None, :]
    lg = LOG_GAMMA
    decay = np.where(diff >= 0, np.exp(lg[:, None, None] * np.maximum(diff, 0.0)), 0.0).astype(np.float32)
    rep = lambda col: np.ascontiguousarray(np.broadcast_to(col.T[:, :, None], (N_HEADS, CHUNK, LANES))).astype(np.float32)
    cross = rep(np.exp(lg[None, :] * (idx[:, None] + 1.0)))
    tail = rep(np.where(idx[:, None] < valid_len, np.exp(lg[None, :] * (valid_len - 1.0 - idx[:, None])), 0.0))
    ret_total = tuple(float(np.exp(np.float32(lg_h) * np.float32(valid_len))) for lg_h in lg)
    sel = np.zeros((3, LANES, N_SCAN * N_HEADS, LANES), np.float32)
    for j in range(N_SCAN * N_HEADS):
        sel[:, j, j, :] = 1.0
    sel = sel.reshape(3 * LANES, N_SCAN * N_HEADS * LANES)

    seq_blk = lambda b, j: (b, j, 0)
    per_b4 = lambda b, j: (b, 0, 0, 0)
    per_b3 = lambda b, j: (b, 0, 0)
    const2 = lambda b, j: (0, 0)
    const3 = lambda b, j: (0, 0, 0)
    kern = functools.partial(_mixer_kernel, n_chunks=n_chunks, valid_len=valid_len, ret_total=ret_total)
    return pl.pallas_call(
        kern,
        grid=(bsz, seq // seq_tile),
        in_specs=[
            pl.BlockSpec((1, seq_tile, N_QKV), seq_blk),
            pl.BlockSpec((1, seq_tile, GATE_COLS), seq_blk),
            pl.BlockSpec((1, N_HEADS, D_HEAD, 2 * D_HEAD), per_b4),
            pl.BlockSpec((1, N_HEADS, LANES), per_b3),
            pl.BlockSpec((1, N_HEADS, D_HEAD, D_HEAD), per_b4),
            pl.BlockSpec((1, LANES), const2),
            pl.BlockSpec((1, LANES), const2),
            pl.BlockSpec((1, D_GROUP), const2),
            pl.BlockSpec((1, D_GROUP), const2),
            pl.BlockSpec((N_HEADS, CHUNK, CHUNK), const3),
            pl.BlockSpec((N_HEADS, CHUNK, LANES), const3),
            pl.BlockSpec((N_HEADS, CHUNK, LANES), const3),
            pl.BlockSpec(sel.shape, const2),
        ],
        out_specs=[
            pl.BlockSpec((1, seq_tile, 2 * D_GROUP), seq_blk),
            pl.BlockSpec((1, N_HEADS, D_HEAD, 2 * D_HEAD), per_b4),
            pl.BlockSpec((1, N_HEADS, LANES), per_b3),
            pl.BlockSpec((1, N_HEADS, D_HEAD, D_HEAD), per_b4),
        ],
        out_shape=[
            jax.ShapeDtypeStruct((bsz, seq, 2 * D_GROUP), BF16),
            jax.ShapeDtypeStruct((bsz, N_HEADS, D_HEAD, 2 * D_HEAD), F32),
            jax.ShapeDtypeStruct((bsz, N_HEADS, LANES), F32),
            jax.ShapeDtypeStruct((bsz, N_HEADS, D_HEAD, D_HEAD), F32),
        ],
        compiler_params=pltpu.CompilerParams(
            dimension_semantics=("arbitrary", "arbitrary"), vmem_limit_bytes=VMEM_LIMIT),
        name="mixer",
    )(qkv, gates, s0, m0, r0, bi, bf, gnm, gnr,
      jnp.asarray(decay), jnp.asarray(cross), jnp.asarray(tail), jnp.asarray(sel, dtype=BF16))


def _rope_tables(pos):
    half = D_HEAD // 2
    inv = ROPE_BASE ** (-jnp.arange(half, dtype=F32) / half)
    ang = pos.astype(F32)[:, None] * inv[None, :]
    cos, sin = jnp.cos(ang), jnp.sin(ang)
    return jnp.concatenate([cos, cos], axis=-1), jnp.concatenate([-sin, sin], axis=-1)


def _pad_lanes(a):
    return jnp.pad(a, [(0, 0)] * (a.ndim - 1) + [(0, LANES - a.shape[-1])])


def _ffn_weights(wg, wu, wd):
    wg = wg.astype(BF16).reshape(D_MODEL, N_FF_CHUNKS, FF_CHUNK)
    wu = wu.astype(BF16).reshape(D_MODEL, N_FF_CHUNKS, FF_CHUNK)
    wgu = jnp.transpose(jnp.concatenate([wg, wu], axis=-1), (1, 0, 2))
    return wgu, wd.astype(BF16).reshape(N_FF_CHUNKS, FF_CHUNK, D_MODEL)


def _layer(x, pos_tab, state, weights, valid_len, seq_tile):
    (g1, wgu1, wd1, gmix, win, wgate, bi, bf, gnm, gnr, wout, g2, wgu2, wd2, gfin) = weights
    bsz, seq, _ = x.shape
    c0, n0, m0, r0 = state
    cos2, sin2 = pos_tab
    x1, qkv, gates = _ffn_in(x.reshape(bsz * seq, D_MODEL), g1, wgu1, wd1, gmix, win, wgate, cos2, sin2)
    qkv = qkv.reshape(bsz, seq, N_QKV)
    gates = gates.reshape(bsz, seq, GATE_COLS)
    if seq % seq_tile:
        pad = seq_tile - seq % seq_tile
        qkv = jnp.pad(qkv, ((0, 0), (0, pad), (0, 0)))
        gates = jnp.pad(gates, ((0, 0), (0, pad), (0, 0)))
    s0 = jnp.concatenate([c0, jnp.broadcast_to(n0[..., None], c0.shape)], axis=-1)
    m0 = jnp.broadcast_to(m0[..., None], (bsz, N_HEADS, LANES))
    y, s_new, m_new, r_new = _mixer(qkv, gates, s0, m0, r0, bi, bf, gnm, gnr, valid_len, seq_tile)
    y = y[:, :seq].reshape(bsz * seq, D_MODEL)
    out = _ffn_out(x1, y, wout, g2, wgu2, wd2, gfin)
    new_state = (s_new[..., :D_HEAD], s_new[..., D_HEAD], m_new[..., 0], r_new)
    return out.reshape(bsz, seq, D_MODEL), new_state


def kernel(x_prompt, x_sample, state_mlstm_C, state_mlstm_n, state_mlstm_m, state_ret_R,
           g_ffn1, w_ffn1_gate, w_ffn1_up, w_ffn1_down, g_mix, w_in, b_gates, g_gn_mlstm, g_gn_ret,
           w_out, g_ffn2, w_ffn2_gate, w_ffn2_up, w_ffn2_down, g_final):
    depth = g_ffn1.shape[0]
    bp, sp, _ = x_prompt.shape
    bs, ss, _ = x_sample.shape
    assert depth == 1, "the final norm is fused into the layer's last kernel"
    assert ss <= CHUNK and TOKEN_TILE % ss == 0 and sp % (4 * CHUNK) == 0
    pos_p = _rope_tables(jnp.arange(sp))
    cos_s, sin_s = _rope_tables(PAST_LEN + jnp.arange(ss))
    pos_s = (jnp.tile(cos_s, (TOKEN_TILE // ss, 1)), jnp.tile(sin_s, (TOKEN_TILE // ss, 1)))

    hp, hs = x_prompt, x_sample
    new_p, new_s = [], []
    for l in range(depth):
        wgu1, wd1 = _ffn_weights(w_ffn1_gate[l], w_ffn1_up[l], w_ffn1_down[l])
        wgu2, wd2 = _ffn_weights(w_ffn2_gate[l], w_ffn2_up[l], w_ffn2_down[l])
        win = w_in[l][:, :N_QKV].astype(BF16)
        wg_cols = w_in[l][:, N_QKV:].astype(BF16)
        rep3 = lambda a: _pad_lanes(jnp.tile(a, (1, N_SCAN)))
        wgate = jnp.concatenate([rep3(wg_cols[:, :N_HEADS]), rep3(wg_cols[:, N_HEADS:])], axis=-1)
        bi = rep3(b_gates[l][None, :N_HEADS].astype(F32))
        bf = rep3(b_gates[l][None, N_HEADS:].astype(F32))
        weights = (g_ffn1[l][None, :], wgu1, wd1, g_mix[l][None, :], win, wgate, bi, bf,
                   g_gn_mlstm[l][None, :].astype(F32), g_gn_ret[l][None, :].astype(F32),
                   w_out[l].astype(BF16), g_ffn2[l][None, :], wgu2, wd2, g_final[None, :])
        zero_state = (jnp.zeros((bp, N_HEADS, D_HEAD, D_HEAD), F32), jnp.zeros((bp, N_HEADS, D_HEAD), F32),
                      jnp.zeros((bp, N_HEADS), F32), jnp.zeros((bp, N_HEADS, D_HEAD, D_HEAD), F32))
        hp, st_p = _layer(hp, pos_p, zero_state, weights, CHUNK, 4 * CHUNK)
        carried = (state_mlstm_C[l].astype(F32), state_mlstm_n[l].astype(F32),
                   state_mlstm_m[l].astype(F32), state_ret_R[l].astype(F32))
        hs, st_s = _layer(hs, pos_s, carried, weights, min(ss, CHUNK), CHUNK)
        new_p.append(st_p)
        new_s.append(st_s)
    stack = lambda sts, i: jnp.stack([s[i] for s in sts])
    return (hp, hs,
            stack(new_p, 0), stack(new_p, 1), stack(new_p, 2), stack(new_p, 3),
            stack(new_s, 0), stack(new_s, 1), stack(new_s, 2), stack(new_s, 3))
```

```python
import functools

import numpy as np
import jax
import jax.numpy as jnp
from jax import lax
from jax.experimental import pallas as pl
from jax.experimental.pallas import tpu as pltpu

F32 = jnp.float32
BF16 = jnp.bfloat16

D_MODEL = 1024
N_HEADS = 4
D_HEAD = 128
D_GROUP = N_HEADS * D_HEAD
D_FF = 2816
PAST_LEN = 4096
ROPE_BASE = 10000.0
EPS = 1e-6
GN_EPS = 1e-5
LOG_GAMMA = np.log(1.0 - 2.0 ** (-5.0 - np.arange(N_HEADS))).astype(np.float32)

LANES = 128
MXU_DIM = 256
FF_CHUNK = MXU_DIM
N_FF_CHUNKS = D_FF // FF_CHUNK
CHUNK = 128
TOKEN_TILE = 512
ROW_SPLIT = 1
N_QKV = 8 * D_GROUP
GATE_ROWS = 16
SCAN_ROWS = 16
N_SCAN = 3
MASKED_U = -1e30
VMEM_LIMIT = 58 * 1024 * 1024

_NT = (((1,), (1,)), ((), ()))
_TN = (((0,), (0,)), ((), ()))


def _row_parts(n_rows):
    sub = n_rows // ROW_SPLIT
    return [slice(r * sub, (r + 1) * sub) for r in range(ROW_SPLIT)]


def _rms_norm(x, g):
    ms = jnp.mean(x * x, axis=-1, keepdims=True)
    return x * lax.rsqrt(ms + EPS) * g


def _swiglu_into(acc_ref, xn, wg_ref, wu_ref, wd_ref):
    for c in range(N_FF_CHUNKS):
        cols = slice(c * FF_CHUNK, (c + 1) * FF_CHUNK)
        g = jnp.dot(xn, wg_ref[:, cols], preferred_element_type=F32)
        u = jnp.dot(xn, wu_ref[:, cols], preferred_element_type=F32)
        a = (g * jax.nn.sigmoid(g) * u).astype(BF16)
        d = jnp.dot(a, wd_ref[cols, :], preferred_element_type=F32)
        if c == 0:
            acc_ref[...] = d
        else:
            acc_ref[...] += d


def _rope_heads(p, cos2, sin2):
    outs = []
    for h in range(N_HEADS):
        xh = p[:, h * D_HEAD:(h + 1) * D_HEAD]
        outs.append(xh * cos2 + pltpu.roll(xh, D_HEAD // 2, axis=1) * sin2)
    return jnp.concatenate(outs, axis=1)


def _log_sigmoid(x):
    return jnp.minimum(x, 0.0) - jnp.log1p(jnp.exp(-jnp.abs(x)))


def _scan_lanes(x, op, fill, pos, seg_len):
    d = 1
    while d < seg_len:
        x = op(x, jnp.where(pos >= d, pltpu.roll(x, d, axis=1), fill))
        d *= 2
    return x


def _split3(x, axis):
    hi = x.astype(BF16)
    r = x - hi.astype(F32)
    mid = r.astype(BF16)
    lo = (r - mid.astype(F32)).astype(BF16)
    return jnp.concatenate([hi, mid, lo], axis=axis)


def _gate_scans(gt, bi, bf, seg_len, g3_ref, ut_ref, first_blk):
    lane = lax.broadcasted_iota(jnp.int32, (8, LANES), 1)
    sub = lax.broadcasted_iota(jnp.int32, (8, LANES), 0)
    pos = lane & (seg_len - 1)
    for c in range(gt.shape[1] // LANES):
        lanes = slice(c * LANES, (c + 1) * LANES)
        ig = gt[0:8, lanes] + bi
        lf = _log_sigmoid(gt[8:16, lanes] + bf)
        b = _scan_lanes(lf, jnp.add, 0.0, pos, seg_len)
        u = ig - b
        cm = _scan_lanes(u, jnp.maximum, MASKED_U, pos, seg_len)
        packed = jnp.concatenate([jnp.where(sub < N_HEADS, cm, b), jnp.where(sub < N_HEADS, u, 0.0)], axis=0)
        g3_ref[first_blk + c] = _split3(packed, axis=0)
        ut_ref[first_blk + c] = u


def _ffn_in_kernel(x_ref, g1_ref, wg_ref, wu_ref, wd_ref, gmix_ref, win_ref, wgate_ref, bi_ref, bf_ref,
                   cos_ref, sin_ref, x1_ref, qkv_ref, g3_ref, ut_ref, acc_ref, *, seg_len):
    parts = _row_parts(x_ref.shape[0])
    for rows in parts:
        xn = _rms_norm(x_ref[rows, :], g1_ref[...]).astype(BF16)
        _swiglu_into(acc_ref.at[rows, :], xn, wg_ref, wu_ref, wd_ref)
    key_scale = D_HEAD ** -0.5
    for rows in parts:
        x1 = x_ref[rows, :] + 0.5 * acc_ref[rows, :]
        x1_ref[rows, :] = x1
        hn = _rms_norm(x1, gmix_ref[...]).astype(BF16)
        gt = lax.dot_general(wgate_ref[...], hn, _NT, preferred_element_type=F32)
        _gate_scans(gt, bi_ref[...], bf_ref[...], seg_len, g3_ref, ut_ref, rows.start // LANES)
        cos2 = cos_ref[rows, :]
        sin2 = sin_ref[rows, :]
        for j in range(8):
            cols = slice(j * D_GROUP, (j + 1) * D_GROUP)
            p = jnp.dot(hn, win_ref[:, cols], preferred_element_type=F32)
            if j in (4, 5):
                p = _rope_heads(p, cos2, sin2)
            if j in (1, 5):
                p = p * key_scale
            qkv_ref[rows, cols] = p.astype(BF16)


def _ffn_out_kernel(x1_ref, y_ref, wout_ref, g2_ref, wg_ref, wu_ref, wd_ref, gfin_ref, out_ref, acc_ref, x2_ref):
    parts = _row_parts(x1_ref.shape[0])
    for rows in parts:
        x2_ref[rows, :] = x1_ref[rows, :] + jnp.dot(y_ref[rows, :], wout_ref[...], preferred_element_type=F32)
    for rows in parts:
        xn = _rms_norm(x2_ref[rows, :], g2_ref[...]).astype(BF16)
        _swiglu_into(acc_ref.at[rows, :], xn, wg_ref, wu_ref, wd_ref)
    for rows in parts:
        x3 = x2_ref[rows, :] + 0.5 * acc_ref[rows, :]
        out_ref[rows, :] = _rms_norm(x3, gfin_ref[...])


def _const_spec(shape):
    nd = len(shape)
    return pl.BlockSpec(shape, lambda *_: (0,) * nd, pipeline_mode=pl.Buffered(1))


def _ffn_in(x, g1, wg, wu, wd, gmix, win, wgate, bi, bf, cos2, sin2, seg_len):
    n_tok = x.shape[0]
    tm = TOKEN_TILE
    pos_tiles = cos2.shape[0] // tm
    row = lambda i: (i, 0)
    blk = lambda i: (i, 0, 0)
    pos = lambda i: (i % pos_tiles, 0)
    return pl.pallas_call(
        functools.partial(_ffn_in_kernel, seg_len=seg_len),
        grid=(n_tok // tm,),
        in_specs=[
            pl.BlockSpec((tm, D_MODEL), row),
            _const_spec((1, D_MODEL)),
            _const_spec(wg.shape),
            _const_spec(wu.shape),
            _const_spec(wd.shape),
            _const_spec((1, D_MODEL)),
            _const_spec(win.shape),
            _const_spec(wgate.shape),
            _const_spec(bi.shape),
            _const_spec(bf.shape),
            pl.BlockSpec((tm, LANES), pos),
            pl.BlockSpec((tm, LANES), pos),
        ],
        out_specs=[
            pl.BlockSpec((tm, D_MODEL), row),
            pl.BlockSpec((tm, N_QKV), row),
            pl.BlockSpec((tm // LANES, 3 * SCAN_ROWS, LANES), blk),
            pl.BlockSpec((tm // LANES, 8, LANES), blk),
        ],
        out_shape=[
            jax.ShapeDtypeStruct((n_tok, D_MODEL), F32),
            jax.ShapeDtypeStruct((n_tok, N_QKV), BF16),
            jax.ShapeDtypeStruct((n_tok // LANES, 3 * SCAN_ROWS, LANES), BF16),
            jax.ShapeDtypeStruct((n_tok // LANES, 8, LANES), F32),
        ],
        scratch_shapes=[pltpu.VMEM((tm, D_MODEL), F32)],
        compiler_params=pltpu.CompilerParams(
            dimension_semantics=("arbitrary",), vmem_limit_bytes=VMEM_LIMIT),
        name="ffn_in",
    )(x, g1, wg, wu, wd, gmix, win, wgate, bi, bf, cos2, sin2)


def _ffn_out(x1, y, wout, g2, wg, wu, wd, gfin):
    n_tok = x1.shape[0]
    tm = TOKEN_TILE
    row = lambda i: (i, 0)
    return pl.pallas_call(
        _ffn_out_kernel,
        grid=(n_tok // tm,),
        in_specs=[
            pl.BlockSpec((tm, D_MODEL), row),
            pl.BlockSpec((tm, D_MODEL), row),
            _const_spec(wout.shape),
            _const_spec((1, D_MODEL)),
            _const_spec(wg.shape),
            _const_spec(wu.shape),
            _const_spec(wd.shape),
            _const_spec((1, D_MODEL)),
        ],
        out_specs=pl.BlockSpec((tm, D_MODEL), row),
        out_shape=jax.ShapeDtypeStruct((n_tok, D_MODEL), F32),
        scratch_shapes=[pltpu.VMEM((tm, D_MODEL), F32), pltpu.VMEM((tm, D_MODEL), F32)],
        compiler_params=pltpu.CompilerParams(
            dimension_semantics=("arbitrary",), vmem_limit_bytes=VMEM_LIMIT),
        name="ffn_out",
    )(x1, y, wout, g2, wg, wu, wd, gfin)


def _group_norm(h, g):
    mu = jnp.mean(h, axis=-1, keepdims=True)
    d = h - mu
    var = jnp.mean(d * d, axis=-1, keepdims=True)
    return d * lax.rsqrt(var + GN_EPS) * g


def _mixer_kernel(qkv_ref, g3_ref, ut_ref, s0_ref, m0_ref, r0_ref, gnm_ref, gnr_ref,
                  decay_ref, cross_ref, tail_ref, sel_ref,
                  y_ref, s_ref, m_ref, r_ref, *, n_chunks, valid_len, ret_total):
    L = CHUNK
    last = valid_len - 1

    @pl.when(pl.program_id(1) == 0)
    def _():
        s_ref[...] = s0_ref[...]
        m_ref[...] = m0_ref[...]
        r_ref[...] = r0_ref[...]

    row_idx = lax.broadcasted_iota(jnp.int32, (L, LANES), 0)
    col_idx = lax.broadcasted_iota(jnp.int32, (L, LANES), 1)
    causal = row_idx >= col_idx
    if valid_len < L:
        causal = causal & (col_idx < valid_len)
    valid = row_idx < valid_len
    ones_blk = jnp.ones((L, D_HEAD), BF16)

    def chunk_step(c, carry):
        rows = pl.ds(pl.multiple_of(c * L, L), L)
        rep = lax.dot_general(g3_ref[c], sel_ref[...], _TN, preferred_element_type=F32)
        u_t = ut_ref[c]

        def head_slices(group, h):
            base = group * 4 * D_GROUP + h * D_HEAD
            return [qkv_ref[0, rows, base + j * D_GROUP:base + j * D_GROUP + D_HEAD] for j in range(4)]

        qk_m, inter_m, qk_r, inter_r = [], [], [], []
        for h in range(N_HEADS):
            q, k, _, _ = head_slices(0, h)
            qk_m.append(lax.dot_general(q, k, _NT, preferred_element_type=F32))
            inter_m.append(jnp.dot(q, s_ref[0, h].astype(BF16), preferred_element_type=F32))
        for h in range(N_HEADS):
            q, k, _, _ = head_slices(1, h)
            qk_r.append(lax.dot_general(q, k, _NT, preferred_element_type=F32))
            inter_r.append(jnp.dot(q, r_ref[0, h].astype(BF16), preferred_element_type=F32))

        p_m, kw_m, gate_m, p_r, kt_r = [], [], [], [], []
        for h in range(N_HEADS):
            cm_b = rep[:, (0 * N_HEADS + h) * LANES:(0 * N_HEADS + h + 1) * LANES]
            b_b = rep[:, (1 * N_HEADS + h) * LANES:(1 * N_HEADS + h + 1) * LANES]
            u_b = rep[:, (2 * N_HEADS + h) * LANES:(2 * N_HEADS + h + 1) * LANES]
            m_prev = m_ref[0, h:h + 1, :]
            big_m = jnp.maximum(cm_b, m_prev)
            a_b = jnp.exp(m_prev - big_m)
            m_t = b_b + big_m
            m_last = big_m[last:last + 1, :]
            m_ref[0, h:h + 1, :] = m_t[last:last + 1, :]
            w = jnp.where(causal, jnp.exp(u_t[h:h + 1, :] - big_m), 0.0)
            p_m.append((qk_m[h] * w).astype(BF16))
            w_last = jnp.exp(u_b - m_last)
            if valid_len < L:
                w_last = jnp.where(valid, w_last, 0.0)
            k = head_slices(0, h)[1]
            kw_m.append((k.astype(F32) * w_last).astype(BF16))
            gate_m.append((a_b, m_t))
        for h in range(N_HEADS):
            p_r.append((qk_r[h] * decay_ref[h]).astype(BF16))
            k = head_slices(1, h)[1]
            kt_r.append((k.astype(F32) * tail_ref[h]).astype(BF16))

        intra_m, upd_m, intra_r, upd_r = [], [], [], []
        for h in range(N_HEADS):
            v = head_slices(0, h)[2]
            v_aug = jnp.concatenate([v, ones_blk], axis=1)
            intra_m.append(jnp.dot(p_m[h], v_aug, preferred_element_type=F32))
            upd_m.append(lax.dot_general(kw_m[h], v_aug, _TN, preferred_element_type=F32))
        for h in range(N_HEADS):
            v = head_slices(1, h)[2]
            intra_r.append(jnp.dot(p_r[h], v, preferred_element_type=F32))
            upd_r.append(lax.dot_general(kt_r[h], v, _TN, preferred_element_type=F32))

        for h in range(N_HEADS):
            hc = slice(h * D_HEAD, (h + 1) * D_HEAD)
            a_b, m_t = gate_m[h]
            a_last = a_b[last:last + 1, :]
            num = a_b * inter_m[h][:, :D_HEAD] + intra_m[h][:, :D_HEAD]
            den = a_b * inter_m[h][:, D_HEAD:] + intra_m[h][:, D_HEAD:]
            hh = num / jnp.maximum(jnp.abs(den), jnp.exp(-m_t))
            s_ref[0, h] = jnp.concatenate([a_last, a_last], axis=1) * s_ref[0, h] + upd_m[h]
            o = head_slices(0, h)[3]
            y = jax.nn.sigmoid(o.astype(F32)) * _group_norm(hh, gnm_ref[:, hc])
            y_ref[0, rows, hc] = y.astype(BF16)
        for h in range(N_HEADS):
            hc = slice(h * D_HEAD, (h + 1) * D_HEAD)
            out = intra_r[h] + cross_ref[h] * inter_r[h]
            r_ref[0, h] = ret_total[h] * r_ref[0, h] + upd_r[h]
            gf = head_slices(1, h)[3].astype(F32)
            y = gf * jax.nn.sigmoid(gf) * _group_norm(out, gnr_ref[:, hc])
            y_ref[0, rows, D_GROUP + h * D_HEAD:D_GROUP + (h + 1) * D_HEAD] = y.astype(BF16)
        return carry

    lax.fori_loop(0, n_chunks, chunk_step, 0, unroll=2 if n_chunks % 2 == 0 else 1)


def _mixer(qkv, g3, ut, s0, m0, r0, gnm, gnr, valid_len, seq_tile):
    bsz, seq = qkv.shape[0], qkv.shape[1]
    n_chunks = seq_tile // CHUNK
    n_tiles = seq // seq_tile
    idx = np.arange(CHUNK, dtype=np.float32)
    diff = idx[:, None] - idx[None, :]
    lg = LOG_GAMMA
    decay = np.where(diff >= 0, np.exp(lg[:, None, None] * np.maximum(diff, 0.0)), 0.0).astype(np.float32)
    rep = lambda col: np.ascontiguousarray(np.broadcast_to(col.T[:, :, None], (N_HEADS, CHUNK, LANES))).astype(np.float32)
    cross = rep(np.exp(lg[None, :] * (idx[:, None] + 1.0)))
    tail = rep(np.where(idx[:, None] < valid_len, np.exp(lg[None, :] * (valid_len - 1.0 - idx[:, None])), 0.0))
    ret_total = tuple(float(np.exp(np.float32(lg_h) * np.float32(valid_len))) for lg_h in lg)
    sel = np.zeros((3, SCAN_ROWS, N_SCAN * N_HEADS, LANES), np.float32)
    for j in range(N_SCAN * N_HEADS):
        sel[:, j, j, :] = 1.0
    sel = sel.reshape(3 * SCAN_ROWS, N_SCAN * N_HEADS * LANES)

    seq_blk = lambda b, j: (b, j, 0)
    chunk_blk = lambda b, j: (b * n_tiles + j, 0, 0)
    per_b4 = lambda b, j: (b, 0, 0, 0)
    per_b3 = lambda b, j: (b, 0, 0)
    const2 = lambda b, j: (0, 0)
    const3 = lambda b, j: (0, 0, 0)
    kern = functools.partial(_mixer_kernel, n_chunks=n_chunks, valid_len=valid_len, ret_total=ret_total)
    return pl.pallas_call(
        kern,
        grid=(bsz, n_tiles),
        in_specs=[
            pl.BlockSpec((1, seq_tile, N_QKV), seq_blk),
            pl.BlockSpec((n_chunks, 3 * SCAN_ROWS, LANES), chunk_blk),
            pl.BlockSpec((n_chunks, 8, LANES), chunk_blk),
            pl.BlockSpec((1, N_HEADS, D_HEAD, 2 * D_HEAD), per_b4),
            pl.BlockSpec((1, N_HEADS, LANES), per_b3),
            pl.BlockSpec((1, N_HEADS, D_HEAD, D_HEAD), per_b4),
            pl.BlockSpec((1, D_GROUP), const2),
            pl.BlockSpec((1, D_GROUP), const2),
            pl.BlockSpec((N_HEADS, CHUNK, CHUNK), const3),
            pl.BlockSpec((N_HEADS, CHUNK, LANES), const3),
            pl.BlockSpec((N_HEADS, CHUNK, LANES), const3),
            pl.BlockSpec(sel.shape, const2),
        ],
        out_specs=[
            pl.BlockSpec((1, seq_tile, 2 * D_GROUP), seq_blk),
            pl.BlockSpec((1, N_HEADS, D_HEAD, 2 * D_HEAD), per_b4),
            pl.BlockSpec((1, N_HEADS, LANES), per_b3),
            pl.BlockSpec((1, N_HEADS, D_HEAD, D_HEAD), per_b4),
        ],
        out_shape=[
            jax.ShapeDtypeStruct((bsz, seq, 2 * D_GROUP), BF16),
            jax.ShapeDtypeStruct((bsz, N_HEADS, D_HEAD, 2 * D_HEAD), F32),
            jax.ShapeDtypeStruct((bsz, N_HEADS, LANES), F32),
            jax.ShapeDtypeStruct((bsz, N_HEADS, D_HEAD, D_HEAD), F32),
        ],
        compiler_params=pltpu.CompilerParams(
            dimension_semantics=("arbitrary", "arbitrary"), vmem_limit_bytes=VMEM_LIMIT),
        name="mixer",
    )(qkv, g3, ut, s0, m0, r0, gnm, gnr,
      jnp.asarray(decay), jnp.asarray(cross), jnp.asarray(tail), jnp.asarray(sel, dtype=BF16))


def _rope_tables(pos):
    half = D_HEAD // 2
    inv = ROPE_BASE ** (-jnp.arange(half, dtype=F32) / half)
    ang = pos.astype(F32)[:, None] * inv[None, :]
    cos, sin = jnp.cos(ang), jnp.sin(ang)
    return jnp.concatenate([cos, cos], axis=-1), jnp.concatenate([-sin, sin], axis=-1)


def _layer(x, pos_tab, state, weights):
    (g1, wg1, wu1, wd1, gmix, win, wgate, bi, bf, gnm, gnr, wout, g2, wg2, wu2, wd2, gfin) = weights
    bsz, seq, _ = x.shape
    c0, n0, m0, r0 = state
    cos2, sin2 = pos_tab
    seg_len = min(seq, CHUNK)
    x1, qkv, g3, ut = _ffn_in(x.reshape(bsz * seq, D_MODEL), g1, wg1, wu1, wd1, gmix, win, wgate, bi, bf,
                              cos2, sin2, seg_len)
    qkv = qkv.reshape(bsz, seq, N_QKV)
    if seg_len < CHUNK:
        pad = CHUNK - seg_len
        per_blk = LANES // seg_len
        unpack = lambda a: jnp.pad(
            a.reshape(a.shape[0], a.shape[1], per_blk, seg_len).transpose(0, 2, 1, 3).reshape(bsz, a.shape[1], seg_len),
            ((0, 0), (0, 0), (0, pad)))
        g3, ut = unpack(g3), unpack(ut)
        qkv = jnp.pad(qkv, ((0, 0), (0, pad), (0, 0)))
        seq_tile = CHUNK
    else:
        seq_tile = 4 * CHUNK
    s0 = jnp.concatenate([c0, jnp.broadcast_to(n0[..., None], c0.shape)], axis=-1)
    m0 = jnp.broadcast_to(m0[..., None], (bsz, N_HEADS, LANES))
    y, s_new, m_new, r_new = _mixer(qkv, g3, ut, s0, m0, r0, gnm, gnr, seg_len, seq_tile)
    y = y[:, :seq].reshape(bsz * seq, D_MODEL)
    out = _ffn_out(x1, y, wout, g2, wg2, wu2, wd2, gfin)
    new_state = (s_new[..., :D_HEAD], s_new[..., D_HEAD], m_new[..., 0], r_new)
    return out.reshape(bsz, seq, D_MODEL), new_state


def kernel(x_prompt, x_sample, state_mlstm_C, state_mlstm_n, state_mlstm_m, state_ret_R,
           g_ffn1, w_ffn1_gate, w_ffn1_up, w_ffn1_down, g_mix, w_in, b_gates, g_gn_mlstm, g_gn_ret,
           w_out, g_ffn2, w_ffn2_gate, w_ffn2_up, w_ffn2_down, g_final):
    depth = g_ffn1.shape[0]
    bp, sp, _ = x_prompt.shape
    bs, ss, _ = x_sample.shape
    assert depth == 1, "the final norm is fused into the layer's last kernel"
    assert ss <= CHUNK and LANES % ss == 0 and TOKEN_TILE % ss == 0 and sp % (4 * CHUNK) == 0
    pos_p = _rope_tables(jnp.arange(sp))
    cos_s, sin_s = _rope_tables(PAST_LEN + jnp.arange(ss))
    pos_s = (jnp.tile(cos_s, (TOKEN_TILE // ss, 1)), jnp.tile(sin_s, (TOKEN_TILE // ss, 1)))

    hp, hs = x_prompt, x_sample
    new_p, new_s = [], []
    for l in range(depth):
        cast = lambda w: w.astype(BF16)
        wi_t = cast(w_in[l][:, N_QKV:N_QKV + N_HEADS]).T
        wf_t = cast(w_in[l][:, N_QKV + N_HEADS:]).T
        wgate = jnp.concatenate([wi_t, wi_t, wf_t, wf_t], axis=0)
        rows8 = lambda b: jnp.broadcast_to(jnp.tile(b.astype(F32), 2)[:, None], (8, LANES))
        bi, bf = rows8(b_gates[l][:N_HEADS]), rows8(b_gates[l][N_HEADS:])
        weights = (g_ffn1[l][None, :], cast(w_ffn1_gate[l]), cast(w_ffn1_up[l]), cast(w_ffn1_down[l]),
                   g_mix[l][None, :], cast(w_in[l][:, :N_QKV]), wgate, bi, bf,
                   g_gn_mlstm[l][None, :].astype(F32), g_gn_ret[l][None, :].astype(F32),
                   cast(w_out[l]), g_ffn2[l][None, :], cast(w_ffn2_gate[l]), cast(w_ffn2_up[l]),
                   cast(w_ffn2_down[l]), g_final[None, :])
        zero_state = (jnp.zeros((bp, N_HEADS, D_HEAD, D_HEAD), F32), jnp.zeros((bp, N_HEADS, D_HEAD), F32),
                      jnp.zeros((bp, N_HEADS), F32), jnp.zeros((bp, N_HEADS, D_HEAD, D_HEAD), F32))
        hp, st_p = _layer(hp, pos_p, zero_state, weights)
        carried = (state_mlstm_C[l].astype(F32), state_mlstm_n[l].astype(F32),
                   state_mlstm_m[l].astype(F32), state_ret_R[l].astype(F32))
        hs, st_s = _layer(hs, pos_s, carried, weights)
        new_p.append(st_p)
        new_s.append(st_s)
    stack = lambda sts, i: jnp.stack([s[i] for s in sts])
    return (hp, hs,
            stack(new_p, 0), stack(new_p, 1), stack(new_p, 2), stack(new_p, 3),
            stack(new_s, 0), stack(new_s, 1), stack(new_s, 2), stack(new_s, 3))
```

```python
import functools

import numpy as np
import jax
import jax.numpy as jnp
from jax import lax
from jax.experimental import pallas as pl
from jax.experimental.pallas import tpu as pltpu

F32 = jnp.float32
BF16 = jnp.bfloat16

D_MODEL = 1024
N_HEADS = 4
D_HEAD = 128
D_GROUP = N_HEADS * D_HEAD
D_FF = 2816
PAST_LEN = 4096
ROPE_BASE = 10000.0
EPS = 1e-6
GN_EPS = 1e-5
LOG_GAMMA = np.log(1.0 - 2.0 ** (-5.0 - np.arange(N_HEADS))).astype(np.float32)

LANES = 128
MXU_DIM = 256
FF_CHUNK = MXU_DIM
N_FF_CHUNKS = D_FF // FF_CHUNK
CHUNK = 128
TOKEN_TILE = 512
N_QKV = 8 * D_GROUP
GATE_ROWS = 16
SCAN_ROWS = 16
N_SCAN = 3
MASKED_U = -1e30
VMEM_LIMIT = 58 * 1024 * 1024

_NT = (((1,), (1,)), ((), ()))
_TN = (((0,), (0,)), ((), ()))


def _rms_norm(x, g):
    ms = jnp.mean(x * x, axis=-1, keepdims=True)
    return x * lax.rsqrt(ms + EPS) * g


def _swiglu_chunk(acc_ref, xn, wg_ref, wu_ref, wd_ref, c):
    cols = slice(c * FF_CHUNK, (c + 1) * FF_CHUNK)
    g = jnp.dot(xn, wg_ref[:, cols], preferred_element_type=F32)
    u = jnp.dot(xn, wu_ref[:, cols], preferred_element_type=F32)
    a = (g * jax.nn.sigmoid(g) * u).astype(BF16)
    d = jnp.dot(a, wd_ref[cols, :], preferred_element_type=F32)
    if c == 0:
        acc_ref[...] = d
    else:
        acc_ref[...] += d


def _swiglu_into(acc_ref, xn, wg_ref, wu_ref, wd_ref):
    for c in range(N_FF_CHUNKS):
        _swiglu_chunk(acc_ref, xn, wg_ref, wu_ref, wd_ref, c)


def _rope_heads(p, cos2, sin2):
    outs = []
    for h in range(N_HEADS):
        xh = p[:, h * D_HEAD:(h + 1) * D_HEAD]
        outs.append(xh * cos2 + pltpu.roll(xh, D_HEAD // 2, axis=1) * sin2)
    return jnp.concatenate(outs, axis=1)


def _log_sigmoid(x):
    return jnp.minimum(x, 0.0) - jnp.log1p(jnp.exp(-jnp.abs(x)))


def _scan_lanes(x, op, fill, pos, seg_len):
    d = 1
    while d < seg_len:
        x = op(x, jnp.where(pos >= d, pltpu.roll(x, d, axis=1), fill))
        d *= 2
    return x


def _split3(x, axis):
    hi = x.astype(BF16)
    r = x - hi.astype(F32)
    mid = r.astype(BF16)
    lo = (r - mid.astype(F32)).astype(BF16)
    return jnp.concatenate([hi, mid, lo], axis=axis)


def _gate_scans(gt, bi, bf, seg_len, g3_ref, ut_ref):
    lane = lax.broadcasted_iota(jnp.int32, (8, LANES), 1)
    sub = lax.broadcasted_iota(jnp.int32, (8, LANES), 0)
    pos = lane & (seg_len - 1)
    for c in range(gt.shape[1] // LANES):
        lanes = slice(c * LANES, (c + 1) * LANES)
        ig = gt[0:8, lanes] + bi
        lf = _log_sigmoid(gt[8:16, lanes] + bf)
        b = _scan_lanes(lf, jnp.add, 0.0, pos, seg_len)
        u = ig - b
        cm = _scan_lanes(u, jnp.maximum, MASKED_U, pos, seg_len)
        packed = jnp.concatenate([jnp.where(sub < N_HEADS, cm, b), jnp.where(sub < N_HEADS, u, 0.0)], axis=0)
        g3_ref[c] = _split3(packed, axis=0)
        ut_ref[c] = u


def _ffn_in_kernel(x_ref, g1_ref, wg_ref, wu_ref, wd_ref, gmix_ref, win_ref, wgate_ref, bi_ref, bf_ref,
                   cos_ref, sin_ref, x1_ref, qkv_ref, g3_ref, ut_ref, acc_ref, *, seg_len):
    x = x_ref[...]
    xn = _rms_norm(x, g1_ref[...]).astype(BF16)
    _swiglu_into(acc_ref, xn, wg_ref, wu_ref, wd_ref)
    x1 = x + 0.5 * acc_ref[...]
    x1_ref[...] = x1
    hn = _rms_norm(x1, gmix_ref[...]).astype(BF16)
    gt = lax.dot_general(wgate_ref[...], hn, _NT, preferred_element_type=F32)
    _gate_scans(gt, bi_ref[...], bf_ref[...], seg_len, g3_ref, ut_ref)
    cos2 = cos_ref[...]
    sin2 = sin_ref[...]
    key_scale = D_HEAD ** -0.5
    for j in range(8):
        cols = slice(j * D_GROUP, (j + 1) * D_GROUP)
        p = jnp.dot(hn, win_ref[:, cols], preferred_element_type=F32)
        if j in (4, 5):
            p = _rope_heads(p, cos2, sin2)
        if j in (1, 5):
            p = p * key_scale
        qkv_ref[:, cols] = p.astype(BF16)


def _ffn_out_kernel(x1_ref, y_ref, wout_ref, g2_ref, wg_ref, wu_ref, wd_ref, gfin_ref, out_ref, acc_ref, x2_ref):
    x2 = x1_ref[...] + jnp.dot(y_ref[...], wout_ref[...], preferred_element_type=F32)
    x2_ref[...] = x2
    xn = _rms_norm(x2, g2_ref[...]).astype(BF16)
    _swiglu_into(acc_ref, xn, wg_ref, wu_ref, wd_ref)
    x3 = x2_ref[...] + 0.5 * acc_ref[...]
    out_ref[...] = _rms_norm(x3, gfin_ref[...])


def _const_spec(shape):
    nd = len(shape)
    return pl.BlockSpec(shape, lambda *_: (0,) * nd, pipeline_mode=pl.Buffered(1))


def _ffn_in(x, g1, wg, wu, wd, gmix, win, wgate, bi, bf, cos2, sin2, seg_len):
    n_tok = x.shape[0]
    tm = TOKEN_TILE
    pos_tiles = cos2.shape[0] // tm
    row = lambda i: (i, 0)
    blk = lambda i: (i, 0, 0)
    pos = lambda i: (i % pos_tiles, 0)
    return pl.pallas_call(
        functools.partial(_ffn_in_kernel, seg_len=seg_len),
        grid=(n_tok // tm,),
        in_specs=[
            pl.BlockSpec((tm, D_MODEL), row),
            _const_spec((1, D_MODEL)),
            _const_spec(wg.shape),
            _const_spec(wu.shape),
            _const_spec(wd.shape),
            _const_spec((1, D_MODEL)),
            _const_spec(win.shape),
            _const_spec(wgate.shape),
            _const_spec(bi.shape),
            _const_spec(bf.shape),
            pl.BlockSpec((tm, LANES), pos),
            pl.BlockSpec((tm, LANES), pos),
        ],
        out_specs=[
            pl.BlockSpec((tm, D_MODEL), row),
            pl.BlockSpec((tm, N_QKV), row),
            pl.BlockSpec((tm // LANES, 3 * SCAN_ROWS, LANES), blk),
            pl.BlockSpec((tm // LANES, 8, LANES), blk),
        ],
        out_shape=[
            jax.ShapeDtypeStruct((n_tok, D_MODEL), F32),
            jax.ShapeDtypeStruct((n_tok, N_QKV), BF16),
            jax.ShapeDtypeStruct((n_tok // LANES, 3 * SCAN_ROWS, LANES), BF16),
            jax.ShapeDtypeStruct((n_tok // LANES, 8, LANES), F32),
        ],
        scratch_shapes=[pltpu.VMEM((tm, D_MODEL), F32)],
        compiler_params=pltpu.CompilerParams(
            dimension_semantics=("arbitrary",), vmem_limit_bytes=VMEM_LIMIT),
        name="ffn_in",
    )(x, g1, wg, wu, wd, gmix, win, wgate, bi, bf, cos2, sin2)


def _ffn_out(x1, y, wout, g2, wg, wu, wd, gfin):
    n_tok = x1.shape[0]
    tm = TOKEN_TILE
    row = lambda i: (i, 0)
    return pl.pallas_call(
        _ffn_out_kernel,
        grid=(n_tok // tm,),
        in_specs=[
            pl.BlockSpec((tm, D_MODEL), row),
            pl.BlockSpec((tm, D_MODEL), row),
            _const_spec(wout.shape),
            _const_spec((1, D_MODEL)),
            _const_spec(wg.shape),
            _const_spec(wu.shape),
            _const_spec(wd.shape),
            _const_spec((1, D_MODEL)),
        ],
        out_specs=pl.BlockSpec((tm, D_MODEL), row),
        out_shape=jax.ShapeDtypeStruct((n_tok, D_MODEL), F32),
        scratch_shapes=[pltpu.VMEM((tm, D_MODEL), F32), pltpu.VMEM((tm, D_MODEL), F32)],
        compiler_params=pltpu.CompilerParams(
            dimension_semantics=("arbitrary",), vmem_limit_bytes=VMEM_LIMIT),
        name="ffn_out",
    )(x1, y, wout, g2, wg, wu, wd, gfin)


def _group_norm(h, g):
    mu = jnp.mean(h, axis=-1, keepdims=True)
    d = h - mu
    var = jnp.mean(d * d, axis=-1, keepdims=True)
    return d * lax.rsqrt(var + GN_EPS) * g


def _chunk_masks(valid_len):
    row_idx = lax.broadcasted_iota(jnp.int32, (CHUNK, LANES), 0)
    col_idx = lax.broadcasted_iota(jnp.int32, (CHUNK, LANES), 1)
    causal = row_idx >= col_idx
    if valid_len < CHUNK:
        causal = causal & (col_idx < valid_len)
    return causal, row_idx < valid_len, jnp.ones((CHUNK, D_HEAD), BF16)


def _mixer_chunk(rows, qkv_ref, g3c, utc, s_ref, m_ref, r_ref, y_ref, consts, masks, valid_len, ret_total,
                 after_scores=None):
    gnm_ref, gnr_ref, decay_ref, cross_ref, tail_ref, sel_ref = consts
    causal, valid, ones_blk = masks
    L = CHUNK
    last = valid_len - 1
    rep = lax.dot_general(g3c, sel_ref[...], _TN, preferred_element_type=F32)

    def head_slices(group, h):
        base = group * 4 * D_GROUP + h * D_HEAD
        return [qkv_ref[rows, base + j * D_GROUP:base + j * D_GROUP + D_HEAD] for j in range(4)]

    qk_m, qk_r = [], []
    for h in range(N_HEADS):
        q, k, _, _ = head_slices(0, h)
        qk_m.append(lax.dot_general(q, k, _NT, preferred_element_type=F32))
    for h in range(N_HEADS):
        q, k, _, _ = head_slices(1, h)
        qk_r.append(lax.dot_general(q, k, _NT, preferred_element_type=F32))
    if after_scores is not None:
        after_scores()

    p_m, kw_m, gate_m, p_r, kt_r = [], [], [], [], []
    for h in range(N_HEADS):
        cm_b = rep[:, (0 * N_HEADS + h) * LANES:(0 * N_HEADS + h + 1) * LANES]
        b_b = rep[:, (1 * N_HEADS + h) * LANES:(1 * N_HEADS + h + 1) * LANES]
        u_b = rep[:, (2 * N_HEADS + h) * LANES:(2 * N_HEADS + h + 1) * LANES]
        m_prev = m_ref[h:h + 1, :]
        big_m = jnp.maximum(cm_b, m_prev)
        a_b = jnp.exp(m_prev - big_m)
        m_t = b_b + big_m
        m_last = big_m[last:last + 1, :]
        m_ref[h:h + 1, :] = m_t[last:last + 1, :]
        w = jnp.where(causal, jnp.exp(utc[h:h + 1, :] - big_m), 0.0)
        p_m.append((qk_m[h] * w).astype(BF16))
        w_last = jnp.exp(u_b - m_last)
        if valid_len < L:
            w_last = jnp.where(valid, w_last, 0.0)
        k = head_slices(0, h)[1]
        kw_m.append((k.astype(F32) * w_last).astype(BF16))
        gate_m.append((a_b, m_t))
    for h in range(N_HEADS):
        p_r.append((qk_r[h] * decay_ref[h]).astype(BF16))
        k = head_slices(1, h)[1]
        kt_r.append((k.astype(F32) * tail_ref[h]).astype(BF16))

    intra_m, upd_m, inter_m, intra_r, upd_r, inter_r = [], [], [], [], [], []
    for h in range(N_HEADS):
        q, _, v, _ = head_slices(0, h)
        v_aug = jnp.concatenate([v, ones_blk], axis=1)
        intra_m.append(jnp.dot(p_m[h], v_aug, preferred_element_type=F32))
        upd_m.append(lax.dot_general(kw_m[h], v_aug, _TN, preferred_element_type=F32))
        inter_m.append(jnp.dot(q, s_ref[h].astype(BF16), preferred_element_type=F32))
    for h in range(N_HEADS):
        q, _, v, _ = head_slices(1, h)
        intra_r.append(jnp.dot(p_r[h], v, preferred_element_type=F32))
        upd_r.append(lax.dot_general(kt_r[h], v, _TN, preferred_element_type=F32))
        inter_r.append(jnp.dot(q, r_ref[h].astype(BF16), preferred_element_type=F32))

    for h in range(N_HEADS):
        hc = slice(h * D_HEAD, (h + 1) * D_HEAD)
        a_b, m_t = gate_m[h]
        a_last = a_b[last:last + 1, :]
        num = a_b * inter_m[h][:, :D_HEAD] + intra_m[h][:, :D_HEAD]
        den = a_b * inter_m[h][:, D_HEAD:] + intra_m[h][:, D_HEAD:]
        hh = num / jnp.maximum(jnp.abs(den), jnp.exp(-m_t))
        s_ref[h] = jnp.concatenate([a_last, a_last], axis=1) * s_ref[h] + upd_m[h]
        o = head_slices(0, h)[3]
        y = jax.nn.sigmoid(o.astype(F32)) * _group_norm(hh, gnm_ref[:, hc])
        y_ref[rows, hc] = y.astype(BF16)
    for h in range(N_HEADS):
        hc = slice(h * D_HEAD, (h + 1) * D_HEAD)
        out = intra_r[h] + cross_ref[h] * inter_r[h]
        r_ref[h] = ret_total[h] * r_ref[h] + upd_r[h]
        gf = head_slices(1, h)[3].astype(F32)
        y = gf * jax.nn.sigmoid(gf) * _group_norm(out, gnr_ref[:, hc])
        y_ref[rows, D_GROUP + h * D_HEAD:D_GROUP + (h + 1) * D_HEAD] = y.astype(BF16)


def _mixer_kernel(qkv_ref, g3_ref, ut_ref, s0_ref, m0_ref, r0_ref, gnm_ref, gnr_ref,
                  decay_ref, cross_ref, tail_ref, sel_ref,
                  y_ref, s_ref, m_ref, r_ref, *, n_chunks, valid_len, ret_total):
    @pl.when(pl.program_id(1) == 0)
    def _():
        s_ref[...] = s0_ref[...]
        m_ref[...] = m0_ref[...]
        r_ref[...] = r0_ref[...]

    consts = (gnm_ref, gnr_ref, decay_ref, cross_ref, tail_ref, sel_ref)
    masks = _chunk_masks(valid_len)
    for c in range(n_chunks):
        _mixer_chunk(slice(c * CHUNK, (c + 1) * CHUNK), qkv_ref.at[0], g3_ref[c], ut_ref[c],
                     s_ref.at[0], m_ref.at[0], r_ref.at[0], y_ref.at[0], consts, masks, valid_len, ret_total)


def _mix_out_kernel(qkv_ref, g3_ref, ut_ref, x1_ref, gnm_ref, gnr_ref, decay_ref, cross_ref, tail_ref, sel_ref,
                    wout_ref, g2_ref, wg_ref, wu_ref, wd_ref, gfin_ref,
                    out_ref, s_out, m_out, r_out,
                    y_scr, acc_ref, x2_ref, s_scr, m_scr, r_scr, *, tiles_per_stream, n_tiles, ret_total):
    j = pl.program_id(0)
    n_chunks = TOKEN_TILE // CHUNK

    @pl.when(j == 0)
    def _():
        y_scr[...] = jnp.zeros_like(y_scr)

    @pl.when(j % tiles_per_stream == 0)
    def _():
        s_scr[...] = jnp.zeros_like(s_scr)
        m_scr[...] = jnp.zeros_like(m_scr)
        r_scr[...] = jnp.zeros_like(r_scr)

    consts = (gnm_ref, gnr_ref, decay_ref, cross_ref, tail_ref, sel_ref)
    masks = _chunk_masks(CHUNK)

    x2 = x1_ref[...] + jnp.dot(y_scr[...], wout_ref[...], preferred_element_type=F32)
    x2_ref[...] = x2
    xn = _rms_norm(x2, g2_ref[...]).astype(BF16)

    pending = list(range(N_FF_CHUNKS))

    def ffn(k):
        for _ in range(min(k, len(pending))):
            _swiglu_chunk(acc_ref, xn, wg_ref, wu_ref, wd_ref, pending.pop(0))

    for c in range(n_chunks):
        _mixer_chunk(slice(c * CHUNK, (c + 1) * CHUNK), qkv_ref, g3_ref[c], ut_ref[c],
                     s_scr, m_scr, r_scr, y_scr, consts, masks, CHUNK, ret_total,
                     after_scores=functools.partial(ffn, 1))
        ffn(1)
    ffn(N_FF_CHUNKS)

    x3 = x2_ref[...] + 0.5 * acc_ref[...]
    out_ref[...] = _rms_norm(x3, gfin_ref[...])

    @pl.when((j % tiles_per_stream == tiles_per_stream - 1) & (j < n_tiles))
    def _():
        s_out[0] = s_scr[...]
        m_out[0] = m_scr[0:N_HEADS, :]
        r_out[0] = r_scr[...]


def _mixer_consts(valid_len):
    idx = np.arange(CHUNK, dtype=np.float32)
    diff = idx[:, None] - idx[None, :]
    lg = LOG_GAMMA
    decay = np.where(diff >= 0, np.exp(lg[:, None, None] * np.maximum(diff, 0.0)), 0.0).astype(np.float32)
    rep = lambda col: np.ascontiguousarray(np.broadcast_to(col.T[:, :, None], (N_HEADS, CHUNK, LANES))).astype(np.float32)
    cross = rep(np.exp(lg[None, :] * (idx[:, None] + 1.0)))
    tail = rep(np.where(idx[:, None] < valid_len, np.exp(lg[None, :] * (valid_len - 1.0 - idx[:, None])), 0.0))
    ret_total = tuple(float(np.exp(np.float32(lg_h) * np.float32(valid_len))) for lg_h in lg)
    sel = np.zeros((3, SCAN_ROWS, N_SCAN * N_HEADS, LANES), np.float32)
    for j in range(N_SCAN * N_HEADS):
        sel[:, j, j, :] = 1.0
    sel = sel.reshape(3 * SCAN_ROWS, N_SCAN * N_HEADS * LANES)
    return (jnp.asarray(decay), jnp.asarray(cross), jnp.asarray(tail), jnp.asarray(sel, dtype=BF16)), ret_total


def _mixer(qkv, g3, ut, s0, m0, r0, gnm, gnr, valid_len, seq_tile):
    bsz, seq = qkv.shape[0], qkv.shape[1]
    n_chunks = seq_tile // CHUNK
    n_tiles = seq // seq_tile
    (decay, cross, tail, sel), ret_total = _mixer_consts(valid_len)

    seq_blk = lambda b, j: (b, j, 0)
    chunk_blk = lambda b, j: (b * n_tiles + j, 0, 0)
    per_b4 = lambda b, j: (b, 0, 0, 0)
    per_b3 = lambda b, j: (b, 0, 0)
    const2 = lambda b, j: (0, 0)
    const3 = lambda b, j: (0, 0, 0)
    kern = functools.partial(_mixer_kernel, n_chunks=n_chunks, valid_len=valid_len, ret_total=ret_total)
    return pl.pallas_call(
        kern,
        grid=(bsz, n_tiles),
        in_specs=[
            pl.BlockSpec((1, seq_tile, N_QKV), seq_blk),
            pl.BlockSpec((n_chunks, 3 * SCAN_ROWS, LANES), chunk_blk),
            pl.BlockSpec((n_chunks, 8, LANES), chunk_blk),
            pl.BlockSpec((1, N_HEADS, D_HEAD, 2 * D_HEAD), per_b4),
            pl.BlockSpec((1, N_HEADS, LANES), per_b3),
            pl.BlockSpec((1, N_HEADS, D_HEAD, D_HEAD), per_b4),
            pl.BlockSpec((1, D_GROUP), const2),
            pl.BlockSpec((1, D_GROUP), const2),
            pl.BlockSpec((N_HEADS, CHUNK, CHUNK), const3),
            pl.BlockSpec((N_HEADS, CHUNK, LANES), const3),
            pl.BlockSpec((N_HEADS, CHUNK, LANES), const3),
            pl.BlockSpec(sel.shape, const2),
        ],
        out_specs=[
            pl.BlockSpec((1, seq_tile, 2 * D_GROUP), seq_blk),
            pl.BlockSpec((1, N_HEADS, D_HEAD, 2 * D_HEAD), per_b4),
            pl.BlockSpec((1, N_HEADS, LANES), per_b3),
            pl.BlockSpec((1, N_HEADS, D_HEAD, D_HEAD), per_b4),
        ],
        out_shape=[
            jax.ShapeDtypeStruct((bsz, seq, 2 * D_GROUP), BF16),
            jax.ShapeDtypeStruct((bsz, N_HEADS, D_HEAD, 2 * D_HEAD), F32),
            jax.ShapeDtypeStruct((bsz, N_HEADS, LANES), F32),
            jax.ShapeDtypeStruct((bsz, N_HEADS, D_HEAD, D_HEAD), F32),
        ],
        compiler_params=pltpu.CompilerParams(
            dimension_semantics=("arbitrary", "arbitrary"), vmem_limit_bytes=VMEM_LIMIT),
        name="mixer",
    )(qkv, g3, ut, s0, m0, r0, gnm, gnr, decay, cross, tail, sel)


def _mix_out(qkv, g3, ut, x1, bsz, gnm, gnr, wout, g2, wg, wu, wd, gfin):
    n_tok = x1.shape[0]
    tm = TOKEN_TILE
    n_tiles = n_tok // tm
    tiles_per_stream = n_tiles // bsz
    n_chunks = tm // CHUNK
    (decay, cross, tail, sel), ret_total = _mixer_consts(CHUNK)

    mix_row = lambda j: (jnp.minimum(j, n_tiles - 1), 0)
    mix_blk = lambda j: (jnp.minimum(j, n_tiles - 1), 0, 0)
    dense_row = lambda j: (jnp.maximum(j - 1, 0), 0)
    stream4 = lambda j: (jnp.minimum(j, n_tiles - 1) // tiles_per_stream, 0, 0, 0)
    stream3 = lambda j: (jnp.minimum(j, n_tiles - 1) // tiles_per_stream, 0, 0)
    kern = functools.partial(_mix_out_kernel, tiles_per_stream=tiles_per_stream, n_tiles=n_tiles,
                             ret_total=ret_total)
    return pl.pallas_call(
        kern,
        grid=(n_tiles + 1,),
        in_specs=[
            pl.BlockSpec((tm, N_QKV), mix_row),
            pl.BlockSpec((n_chunks, 3 * SCAN_ROWS, LANES), mix_blk),
            pl.BlockSpec((n_chunks, 8, LANES), mix_blk),
            pl.BlockSpec((tm, D_MODEL), dense_row),
            _const_spec((1, D_GROUP)),
            _const_spec((1, D_GROUP)),
            _const_spec(decay.shape),
            _const_spec(cross.shape),
            _const_spec(tail.shape),
            _const_spec(sel.shape),
            _const_spec(wout.shape),
            _const_spec((1, D_MODEL)),
            _const_spec(wg.shape),
            _const_spec(wu.shape),
            _const_spec(wd.shape),
            _const_spec((1, D_MODEL)),
        ],
        out_specs=[
            pl.BlockSpec((tm, D_MODEL), dense_row),
            pl.BlockSpec((1, N_HEADS, D_HEAD, 2 * D_HEAD), stream4),
            pl.BlockSpec((1, N_HEADS, LANES), stream3),
            pl.BlockSpec((1, N_HEADS, D_HEAD, D_HEAD), stream4),
        ],
        out_shape=[
            jax.ShapeDtypeStruct((n_tok, D_MODEL), F32),
            jax.ShapeDtypeStruct((bsz, N_HEADS, D_HEAD, 2 * D_HEAD), F32),
            jax.ShapeDtypeStruct((bsz, N_HEADS, LANES), F32),
            jax.ShapeDtypeStruct((bsz, N_HEADS, D_HEAD, D_HEAD), F32),
        ],
        scratch_shapes=[
            pltpu.VMEM((tm, D_MODEL), BF16),
            pltpu.VMEM((tm, D_MODEL), F32),
            pltpu.VMEM((tm, D_MODEL), F32),
            pltpu.VMEM((N_HEADS, D_HEAD, 2 * D_HEAD), F32),
            pltpu.VMEM((8, LANES), F32),
            pltpu.VMEM((N_HEADS, D_HEAD, D_HEAD), F32),
        ],
        compiler_params=pltpu.CompilerParams(
            dimension_semantics=("arbitrary",), vmem_limit_bytes=VMEM_LIMIT),
        name="mix_out",
    )(qkv, g3, ut, x1, gnm, gnr, decay, cross, tail, sel, wout, g2, wg, wu, wd, gfin)


def _rope_tables(pos):
    half = D_HEAD // 2
    inv = ROPE_BASE ** (-jnp.arange(half, dtype=F32) / half)
    ang = pos.astype(F32)[:, None] * inv[None, :]
    cos, sin = jnp.cos(ang), jnp.sin(ang)
    return jnp.concatenate([cos, cos], axis=-1), jnp.concatenate([-sin, sin], axis=-1)


def _layer(x, pos_tab, state, weights):
    (g1, wg1, wu1, wd1, gmix, win, wgate, bi, bf, gnm, gnr, wout, g2, wg2, wu2, wd2, gfin) = weights
    bsz, seq, _ = x.shape
    cos2, sin2 = pos_tab
    seg_len = min(seq, CHUNK)
    x1, qkv, g3, ut = _ffn_in(x.reshape(bsz * seq, D_MODEL), g1, wg1, wu1, wd1, gmix, win, wgate, bi, bf,
                              cos2, sin2, seg_len)
    if state is None:
        out, s_new, m_new, r_new = _mix_out(qkv, g3, ut, x1, bsz, gnm, gnr, wout, g2, wg2, wu2, wd2, gfin)
    else:
        c0, n0, m0, r0 = state
        pad = CHUNK - seg_len
        per_blk = LANES // seg_len
        unpack = lambda a: jnp.pad(
            a.reshape(a.shape[0], a.shape[1], per_blk, seg_len).transpose(0, 2, 1, 3).reshape(bsz, a.shape[1], seg_len),
            ((0, 0), (0, 0), (0, pad)))
        qkv = jnp.pad(qkv.reshape(bsz, seq, N_QKV), ((0, 0), (0, pad), (0, 0)))
        s0 = jnp.concatenate([c0, jnp.broadcast_to(n0[..., None], c0.shape)], axis=-1)
        m0 = jnp.broadcast_to(m0[..., None], (bsz, N_HEADS, LANES))
        y, s_new, m_new, r_new = _mixer(qkv, unpack(g3), unpack(ut), s0, m0, r0, gnm, gnr, seg_len, CHUNK)
        out = _ffn_out(x1, y[:, :seq].reshape(bsz * seq, D_MODEL), wout, g2, wg2, wu2, wd2, gfin)
    new_state = (s_new[..., :D_HEAD], s_new[..., D_HEAD], m_new[..., 0], r_new)
    return out.reshape(bsz, seq, D_MODEL), new_state


def kernel(x_prompt, x_sample, state_mlstm_C, state_mlstm_n, state_mlstm_m, state_ret_R,
           g_ffn1, w_ffn1_gate, w_ffn1_up, w_ffn1_down, g_mix, w_in, b_gates, g_gn_mlstm, g_gn_ret,
           w_out, g_ffn2, w_ffn2_gate, w_ffn2_up, w_ffn2_down, g_final):
    depth = g_ffn1.shape[0]
    bp, sp, _ = x_prompt.shape
    bs, ss, _ = x_sample.shape
    assert depth == 1, "the final norm is fused into the layer's last kernel"
    assert ss < CHUNK and LANES % ss == 0 and TOKEN_TILE % ss == 0 and sp % TOKEN_TILE == 0
    pos_p = _rope_tables(jnp.arange(sp))
    cos_s, sin_s = _rope_tables(PAST_LEN + jnp.arange(ss))
    pos_s = (jnp.tile(cos_s, (TOKEN_TILE // ss, 1)), jnp.tile(sin_s, (TOKEN_TILE // ss, 1)))

    hp, hs = x_prompt, x_sample
    new_p, new_s = [], []
    for l in range(depth):
        cast = lambda w: w.astype(BF16)
        wi_t = cast(w_in[l][:, N_QKV:N_QKV + N_HEADS]).T
        wf_t = cast(w_in[l][:, N_QKV + N_HEADS:]).T
        wgate = jnp.concatenate([wi_t, wi_t, wf_t, wf_t], axis=0)
        rows8 = lambda b: jnp.broadcast_to(jnp.tile(b.astype(F32), 2)[:, None], (8, LANES))
        bi, bf = rows8(b_gates[l][:N_HEADS]), rows8(b_gates[l][N_HEADS:])
        weights = (g_ffn1[l][None, :], cast(w_ffn1_gate[l]), cast(w_ffn1_up[l]), cast(w_ffn1_down[l]),
                   g_mix[l][None, :], cast(w_in[l][:, :N_QKV]), wgate, bi, bf,
                   g_gn_mlstm[l][None, :].astype(F32), g_gn_ret[l][None, :].astype(F32),
                   cast(w_out[l]), g_ffn2[l][None, :], cast(w_ffn2_gate[l]), cast(w_ffn2_up[l]),
                   cast(w_ffn2_down[l]), g_final[None, :])
        hp, st_p = _layer(hp, pos_p, None, weights)
        carried = (state_mlstm_C[l].astype(F32), state_mlstm_n[l].astype(F32),
                   state_mlstm_m[l].astype(F32), state_ret_R[l].astype(F32))
        hs, st_s = _layer(hs, pos_s, carried, weights)
        new_p.append(st_p)
        new_s.append(st_s)
    stack = lambda sts, i: jnp.stack([s[i] for s in sts])
    return (hp, hs,
            stack(new_p, 0), stack(new_p, 1), stack(new_p, 2), stack(new_p, 3),
            stack(new_s, 0), stack(new_s, 1), stack(new_s, 2), stack(new_s, 3))
```

```python
import functools

import numpy as np
import jax
import jax.numpy as jnp
from jax import lax
from jax.experimental import pallas as pl
from jax.experimental.pallas import tpu as pltpu

F32 = jnp.float32
BF16 = jnp.bfloat16

D_MODEL = 1024
N_HEADS = 4
D_HEAD = 128
D_GROUP = N_HEADS * D_HEAD
D_FF = 2816
PAST_LEN = 4096
ROPE_BASE = 10000.0
EPS = 1e-6
GN_EPS = 1e-5
LOG_GAMMA = np.log(1.0 - 2.0 ** (-5.0 - np.arange(N_HEADS))).astype(np.float32)

LANES = 128
MXU_DIM = 256
FF_CHUNK = MXU_DIM
N_FF_CHUNKS = D_FF // FF_CHUNK
CHUNK = 128
TOKEN_TILE = 512
N_QKV = 8 * D_GROUP
GATE_ROWS = 16
SCAN_ROWS = 16
N_SCAN = 3
MASKED_U = -1e30
VMEM_LIMIT = 58 * 1024 * 1024

_NT = (((1,), (1,)), ((), ()))
_TN = (((0,), (0,)), ((), ()))


def _rms_norm(x, g):
    ms = jnp.mean(x * x, axis=-1, keepdims=True)
    return x * lax.rsqrt(ms + EPS) * g


def _swiglu_chunk(acc_ref, xn, wg_ref, wu_ref, wd_ref, c):
    cols = slice(c * FF_CHUNK, (c + 1) * FF_CHUNK)
    g = jnp.dot(xn, wg_ref[:, cols], preferred_element_type=F32)
    u = jnp.dot(xn, wu_ref[:, cols], preferred_element_type=F32)
    a = (g * jax.nn.sigmoid(g) * u).astype(BF16)
    d = jnp.dot(a, wd_ref[cols, :], preferred_element_type=F32)
    if c == 0:
        acc_ref[...] = d
    else:
        acc_ref[...] += d


def _swiglu_into(acc_ref, xn, wg_ref, wu_ref, wd_ref):
    for c in range(N_FF_CHUNKS):
        _swiglu_chunk(acc_ref, xn, wg_ref, wu_ref, wd_ref, c)


def _rope_heads(p, cos2, sin2):
    outs = []
    for h in range(N_HEADS):
        xh = p[:, h * D_HEAD:(h + 1) * D_HEAD]
        outs.append(xh * cos2 + pltpu.roll(xh, D_HEAD // 2, axis=1) * sin2)
    return jnp.concatenate(outs, axis=1)


def _log_sigmoid(x):
    return jnp.minimum(x, 0.0) - jnp.log1p(jnp.exp(-jnp.abs(x)))


def _scan_lanes(x, op, fill, pos, seg_len):
    d = 1
    while d < seg_len:
        x = op(x, jnp.where(pos >= d, pltpu.roll(x, d, axis=1), fill))
        d *= 2
    return x


def _split3(x, axis):
    hi = x.astype(BF16)
    r = x - hi.astype(F32)
    mid = r.astype(BF16)
    lo = (r - mid.astype(F32)).astype(BF16)
    return jnp.concatenate([hi, mid, lo], axis=axis)


def _gate_scans(gt, bi, bf, seg_len, g3_ref, ut_ref):
    lane = lax.broadcasted_iota(jnp.int32, (8, LANES), 1)
    sub = lax.broadcasted_iota(jnp.int32, (8, LANES), 0)
    pos = lane & (seg_len - 1)
    for c in range(gt.shape[1] // LANES):
        lanes = slice(c * LANES, (c + 1) * LANES)
        ig = gt[0:8, lanes] + bi
        lf = _log_sigmoid(gt[8:16, lanes] + bf)
        b = _scan_lanes(lf, jnp.add, 0.0, pos, seg_len)
        u = ig - b
        cm = _scan_lanes(u, jnp.maximum, MASKED_U, pos, seg_len)
        packed = jnp.concatenate([jnp.where(sub < N_HEADS, cm, b), jnp.where(sub < N_HEADS, u, 0.0)], axis=0)
        g3_ref[c] = _split3(packed, axis=0)
        ut_ref[c] = u


def _ffn_in_kernel(x_ref, g1_ref, wg_ref, wu_ref, wd_ref, gmix_ref, win_ref, wgate_ref, bi_ref, bf_ref,
                   cos_ref, sin_ref, x1_ref, qkv_ref, g3_ref, ut_ref, acc_ref, *, seg_len):
    x = x_ref[...]
    xn = _rms_norm(x, g1_ref[...]).astype(BF16)
    _swiglu_into(acc_ref, xn, wg_ref, wu_ref, wd_ref)
    x1 = x + 0.5 * acc_ref[...]
    x1_ref[...] = x1
    hn = _rms_norm(x1, gmix_ref[...]).astype(BF16)
    gt = lax.dot_general(wgate_ref[...].astype(BF16), hn, _NT, preferred_element_type=F32)
    _gate_scans(gt, bi_ref[...], bf_ref[...], seg_len, g3_ref, ut_ref)
    cos2 = cos_ref[...]
    sin2 = sin_ref[...]
    key_scale = D_HEAD ** -0.5
    for j in range(8):
        cols = slice(j * D_GROUP, (j + 1) * D_GROUP)
        p = jnp.dot(hn, win_ref[:, cols], preferred_element_type=F32)
        if j in (4, 5):
            p = _rope_heads(p, cos2, sin2)
        if j in (1, 5):
            p = p * key_scale
        qkv_ref[:, cols] = p.astype(BF16)


def _ffn_out_kernel(x1_ref, y_ref, wout_ref, g2_ref, wg_ref, wu_ref, wd_ref, gfin_ref, out_ref, acc_ref, x2_ref):
    x2 = x1_ref[...] + jnp.dot(y_ref[...], wout_ref[...], preferred_element_type=F32)
    x2_ref[...] = x2
    xn = _rms_norm(x2, g2_ref[...]).astype(BF16)
    _swiglu_into(acc_ref, xn, wg_ref, wu_ref, wd_ref)
    x3 = x2_ref[...] + 0.5 * acc_ref[...]
    out_ref[...] = _rms_norm(x3, gfin_ref[...])


def _const_spec(shape):
    nd = len(shape)
    return pl.BlockSpec(shape, lambda *_: (0,) * nd, pipeline_mode=pl.Buffered(1))


def _ffn_in(x, g1, wg, wu, wd, gmix, win, wgate, bi, bf, cos2, sin2, seg_len):
    n_tok = x.shape[0]
    tm = TOKEN_TILE
    pos_tiles = cos2.shape[0] // tm
    row = lambda i: (i, 0)
    blk = lambda i: (i, 0, 0)
    pos = lambda i: (i % pos_tiles, 0)
    return pl.pallas_call(
        functools.partial(_ffn_in_kernel, seg_len=seg_len),
        grid=(n_tok // tm,),
        in_specs=[
            pl.BlockSpec((tm, D_MODEL), row),
            _const_spec((1, D_MODEL)),
            _const_spec(wg.shape),
            _const_spec(wu.shape),
            _const_spec(wd.shape),
            _const_spec((1, D_MODEL)),
            _const_spec(win.shape),
            _const_spec(wgate.shape),
            _const_spec(bi.shape),
            _const_spec(bf.shape),
            pl.BlockSpec((tm, LANES), pos),
            pl.BlockSpec((tm, LANES), pos),
        ],
        out_specs=[
            pl.BlockSpec((tm, D_MODEL), row),
            pl.BlockSpec((tm, N_QKV), row),
            pl.BlockSpec((tm // LANES, 3 * SCAN_ROWS, LANES), blk),
            pl.BlockSpec((tm // LANES, 8, LANES), blk),
        ],
        out_shape=[
            jax.ShapeDtypeStruct((n_tok, D_MODEL), F32),
            jax.ShapeDtypeStruct((n_tok, N_QKV), BF16),
            jax.ShapeDtypeStruct((n_tok // LANES, 3 * SCAN_ROWS, LANES), BF16),
            jax.ShapeDtypeStruct((n_tok // LANES, 8, LANES), F32),
        ],
        scratch_shapes=[pltpu.VMEM((tm, D_MODEL), F32)],
        compiler_params=pltpu.CompilerParams(
            dimension_semantics=("arbitrary",), vmem_limit_bytes=VMEM_LIMIT),
        name="ffn_in",
    )(x, g1, wg, wu, wd, gmix, win, wgate, bi, bf, cos2, sin2)


def _ffn_out(x1, y, wout, g2, wg, wu, wd, gfin):
    n_tok = x1.shape[0]
    tm = TOKEN_TILE
    row = lambda i: (i, 0)
    return pl.pallas_call(
        _ffn_out_kernel,
        grid=(n_tok // tm,),
        in_specs=[
            pl.BlockSpec((tm, D_MODEL), row),
            pl.BlockSpec((tm, D_MODEL), row),
            _const_spec(wout.shape),
            _const_spec((1, D_MODEL)),
            _const_spec(wg.shape),
            _const_spec(wu.shape),
            _const_spec(wd.shape),
            _const_spec((1, D_MODEL)),
        ],
        out_specs=pl.BlockSpec((tm, D_MODEL), row),
        out_shape=jax.ShapeDtypeStruct((n_tok, D_MODEL), F32),
        scratch_shapes=[pltpu.VMEM((tm, D_MODEL), F32), pltpu.VMEM((tm, D_MODEL), F32)],
        compiler_params=pltpu.CompilerParams(
            dimension_semantics=("arbitrary",), vmem_limit_bytes=VMEM_LIMIT),
        name="ffn_out",
    )(x1, y, wout, g2, wg, wu, wd, gfin)


def _group_norm(h, g):
    mu = jnp.mean(h, axis=-1, keepdims=True)
    d = h - mu
    var = jnp.mean(d * d, axis=-1, keepdims=True)
    return d * lax.rsqrt(var + GN_EPS) * g


def _chunk_masks(frames):
    row_idx = lax.broadcasted_iota(jnp.int32, (frames, frames), 0)
    col_idx = lax.broadcasted_iota(jnp.int32, (frames, frames), 1)
    return row_idx >= col_idx, jnp.ones((frames, D_HEAD), BF16)


def _mixer_chunk(rows, qkv_ref, g3c, utc, s_ref, m_ref, r_ref, y_ref, consts, masks, ret_total, after_scores=None):
    gnm_ref, gnr_ref, decay_ref, cross_ref, tail_ref, sel_ref = consts
    causal, ones_blk = masks
    L = causal.shape[0]
    rep = lax.dot_general(g3c, sel_ref[...], _TN, preferred_element_type=F32)

    def head_slices(group, h):
        base = group * 4 * D_GROUP + h * D_HEAD
        return [qkv_ref[rows, base + j * D_GROUP:base + j * D_GROUP + D_HEAD] for j in range(4)]

    qk_m, qk_r = [], []
    for h in range(N_HEADS):
        q, k, _, _ = head_slices(0, h)
        qk_m.append(lax.dot_general(q, k, _NT, preferred_element_type=F32))
    for h in range(N_HEADS):
        q, k, _, _ = head_slices(1, h)
        qk_r.append(lax.dot_general(q, k, _NT, preferred_element_type=F32))
    if after_scores is not None:
        after_scores()

    lhs_m, kw_m, gate_m, lhs_r, kt_r = [], [], [], [], []
    for h in range(N_HEADS):
        cm_b = rep[:, (0 * N_HEADS + h) * LANES:(0 * N_HEADS + h + 1) * LANES]
        b_b = rep[:, (1 * N_HEADS + h) * LANES:(1 * N_HEADS + h + 1) * LANES]
        u_b = rep[:, (2 * N_HEADS + h) * LANES:(2 * N_HEADS + h + 1) * LANES]
        m_prev = m_ref[h:h + 1, :]
        big_m = jnp.maximum(cm_b, m_prev)
        a_b = jnp.exp(m_prev - big_m)
        m_t = b_b + big_m
        m_last = big_m[L - 1:L, :]
        m_ref[h:h + 1, :] = m_t[L - 1:L, :]
        w = jnp.where(causal, jnp.exp(utc[h:h + 1, :] - big_m[:, :L]), 0.0)
        q, k, _, _ = head_slices(0, h)
        lhs_m.append(jnp.concatenate([(q.astype(F32) * a_b).astype(BF16), (qk_m[h] * w).astype(BF16)], axis=1))
        kw_m.append((k.astype(F32) * jnp.exp(u_b - m_last)).astype(BF16))
        gate_m.append((a_b[L - 1:L, :], m_t))
    for h in range(N_HEADS):
        q, k, _, _ = head_slices(1, h)
        lhs_r.append(jnp.concatenate([(q.astype(F32) * cross_ref[h]).astype(BF16),
                                      (qk_r[h] * decay_ref[h]).astype(BF16)], axis=1))
        kt_r.append((k.astype(F32) * tail_ref[h]).astype(BF16))

    res_m, upd_m, res_r, upd_r = [], [], [], []
    for h in range(N_HEADS):
        v = head_slices(0, h)[2]
        v_aug = jnp.concatenate([v, ones_blk], axis=1)
        rhs = jnp.concatenate([s_ref[h].astype(BF16), v_aug], axis=0)
        res_m.append(jnp.dot(lhs_m[h], rhs, preferred_element_type=F32))
        upd_m.append(lax.dot_general(kw_m[h], v_aug, _TN, preferred_element_type=F32))
    for h in range(N_HEADS):
        v = head_slices(1, h)[2]
        rhs = jnp.concatenate([r_ref[h].astype(BF16), v], axis=0)
        res_r.append(jnp.dot(lhs_r[h], rhs, preferred_element_type=F32))
        upd_r.append(lax.dot_general(kt_r[h], v, _TN, preferred_element_type=F32))

    for h in range(N_HEADS):
        hc = slice(h * D_HEAD, (h + 1) * D_HEAD)
        a_last, m_t = gate_m[h]
        num = res_m[h][:, :D_HEAD]
        den = res_m[h][:, D_HEAD:]
        hh = num / jnp.maximum(jnp.abs(den), jnp.exp(-m_t))
        s_ref[h] = jnp.concatenate([a_last, a_last], axis=1) * s_ref[h] + upd_m[h]
        o = head_slices(0, h)[3]
        y = jax.nn.sigmoid(o.astype(F32)) * _group_norm(hh, gnm_ref[:, hc])
        y_ref[rows, hc] = y.astype(BF16)
    for h in range(N_HEADS):
        hc = slice(h * D_HEAD, (h + 1) * D_HEAD)
        r_ref[h] = ret_total[h] * r_ref[h] + upd_r[h]
        gf = head_slices(1, h)[3].astype(F32)
        y = gf * jax.nn.sigmoid(gf) * _group_norm(res_r[h], gnr_ref[:, hc])
        y_ref[rows, D_GROUP + h * D_HEAD:D_GROUP + (h + 1) * D_HEAD] = y.astype(BF16)


def _mixer_kernel(qkv_ref, g3_ref, ut_ref, s0_ref, m0_ref, r0_ref, gnm_ref, gnr_ref,
                  decay_ref, cross_ref, tail_ref, sel_ref,
                  y_ref, s_ref, m_ref, r_ref, *, frames, ret_total):
    s_ref[...] = s0_ref[...]
    m_ref[...] = m0_ref[...]
    r_ref[...] = r0_ref[...]
    consts = (gnm_ref, gnr_ref, decay_ref, cross_ref, tail_ref, sel_ref)
    masks = _chunk_masks(frames)
    for i in range(LANES // frames):
        seg = slice(i * frames, (i + 1) * frames)
        _mixer_chunk(seg, qkv_ref, g3_ref[0][:, seg], ut_ref[0][:, seg],
                     s_ref.at[i], m_ref.at[i], r_ref.at[i], y_ref, consts, masks, ret_total)


def _mix_out_kernel(qkv_ref, g3_ref, ut_ref, x1_ref, gnm_ref, gnr_ref, decay_ref, cross_ref, tail_ref, sel_ref,
                    wout_ref, g2_ref, wg_ref, wu_ref, wd_ref, gfin_ref,
                    out_ref, s_out, m_out, r_out,
                    y_scr, acc_ref, x2_ref, s_scr, m_scr, r_scr, *, tiles_per_stream, n_tiles, ret_total):
    j = pl.program_id(0)
    n_chunks = TOKEN_TILE // CHUNK

    @pl.when(j == 0)
    def _():
        y_scr[...] = jnp.zeros_like(y_scr)

    @pl.when(j % tiles_per_stream == 0)
    def _():
        s_scr[...] = jnp.zeros_like(s_scr)
        m_scr[...] = jnp.zeros_like(m_scr)
        r_scr[...] = jnp.zeros_like(r_scr)

    consts = (gnm_ref, gnr_ref, decay_ref, cross_ref, tail_ref, sel_ref)
    masks = _chunk_masks(CHUNK)

    x2 = x1_ref[...] + jnp.dot(y_scr[...], wout_ref[...], preferred_element_type=F32)
    x2_ref[...] = x2
    xn = _rms_norm(x2, g2_ref[...]).astype(BF16)

    pending = list(range(N_FF_CHUNKS))

    def ffn(k):
        for _ in range(min(k, len(pending))):
            _swiglu_chunk(acc_ref, xn, wg_ref, wu_ref, wd_ref, pending.pop(0))

    for c in range(n_chunks):
        _mixer_chunk(slice(c * CHUNK, (c + 1) * CHUNK), qkv_ref, g3_ref[c], ut_ref[c],
                     s_scr, m_scr, r_scr, y_scr, consts, masks, ret_total,
                     after_scores=functools.partial(ffn, 1))
        ffn(1)
    ffn(N_FF_CHUNKS)

    x3 = x2_ref[...] + 0.5 * acc_ref[...]
    out_ref[...] = _rms_norm(x3, gfin_ref[...])

    @pl.when((j % tiles_per_stream == tiles_per_stream - 1) & (j < n_tiles))
    def _():
        s_out[0] = s_scr[...]
        m_out[0] = m_scr[0:N_HEADS, :]
        r_out[0] = r_scr[...]


def _mixer_consts(frames):
    idx = np.arange(frames, dtype=np.float32)
    diff = idx[:, None] - idx[None, :]
    lg = LOG_GAMMA
    decay = np.where(diff >= 0, np.exp(lg[:, None, None] * np.maximum(diff, 0.0)), 0.0).astype(np.float32)
    rep = lambda col: np.ascontiguousarray(np.broadcast_to(col.T[:, :, None], (N_HEADS, frames, LANES))).astype(np.float32)
    cross = rep(np.exp(lg[None, :] * (idx[:, None] + 1.0)))
    tail = rep(np.exp(lg[None, :] * (frames - 1.0 - idx[:, None])))
    ret_total = tuple(float(np.exp(np.float32(lg_h) * np.float32(frames))) for lg_h in lg)
    sel = np.zeros((3, SCAN_ROWS, N_SCAN * N_HEADS, LANES), np.float32)
    for j in range(N_SCAN * N_HEADS):
        sel[:, j, j, :] = 1.0
    sel = sel.reshape(3 * SCAN_ROWS, N_SCAN * N_HEADS * LANES)
    return (jnp.asarray(decay), jnp.asarray(cross), jnp.asarray(tail), jnp.asarray(sel, dtype=BF16)), ret_total


def _mixer(qkv, g3, ut, s0, m0, r0, gnm, gnr, frames):
    n_tok = qkv.shape[0]
    per_blk = LANES // frames
    (decay, cross, tail, sel), ret_total = _mixer_consts(frames)
    row = lambda i: (i, 0)
    blk3 = lambda i: (i, 0, 0)
    blk4 = lambda i: (i, 0, 0, 0)
    return pl.pallas_call(
        functools.partial(_mixer_kernel, frames=frames, ret_total=ret_total),
        grid=(n_tok // LANES,),
        in_specs=[
            pl.BlockSpec((LANES, N_QKV), row),
            pl.BlockSpec((1, 3 * SCAN_ROWS, LANES), blk3),
            pl.BlockSpec((1, 8, LANES), blk3),
            pl.BlockSpec((per_blk, N_HEADS, D_HEAD, 2 * D_HEAD), blk4),
            pl.BlockSpec((per_blk, N_HEADS, LANES), blk3),
            pl.BlockSpec((per_blk, N_HEADS, D_HEAD, D_HEAD), blk4),
            _const_spec((1, D_GROUP)),
            _const_spec((1, D_GROUP)),
            _const_spec(decay.shape),
            _const_spec(cross.shape),
            _const_spec(tail.shape),
            _const_spec(sel.shape),
        ],
        out_specs=[
            pl.BlockSpec((LANES, D_MODEL), row),
            pl.BlockSpec((per_blk, N_HEADS, D_HEAD, 2 * D_HEAD), blk4),
            pl.BlockSpec((per_blk, N_HEADS, LANES), blk3),
            pl.BlockSpec((per_blk, N_HEADS, D_HEAD, D_HEAD), blk4),
        ],
        out_shape=[
            jax.ShapeDtypeStruct((n_tok, D_MODEL), BF16),
            jax.ShapeDtypeStruct(s0.shape, F32),
            jax.ShapeDtypeStruct(m0.shape, F32),
            jax.ShapeDtypeStruct(r0.shape, F32),
        ],
        compiler_params=pltpu.CompilerParams(
            dimension_semantics=("arbitrary",), vmem_limit_bytes=VMEM_LIMIT),
        name="mixer",
    )(qkv, g3, ut, s0, m0, r0, gnm, gnr, decay, cross, tail, sel)


def _mix_out(qkv, g3, ut, x1, bsz, gnm, gnr, wout, g2, wg, wu, wd, gfin):
    n_tok = x1.shape[0]
    tm = TOKEN_TILE
    n_tiles = n_tok // tm
    tiles_per_stream = n_tiles // bsz
    n_chunks = tm // CHUNK
    (decay, cross, tail, sel), ret_total = _mixer_consts(CHUNK)

    mix_row = lambda j: (jnp.minimum(j, n_tiles - 1), 0)
    mix_blk = lambda j: (jnp.minimum(j, n_tiles - 1), 0, 0)
    dense_row = lambda j: (jnp.maximum(j - 1, 0), 0)
    stream4 = lambda j: (jnp.minimum(j, n_tiles - 1) // tiles_per_stream, 0, 0, 0)
    stream3 = lambda j: (jnp.minimum(j, n_tiles - 1) // tiles_per_stream, 0, 0)
    kern = functools.partial(_mix_out_kernel, tiles_per_stream=tiles_per_stream, n_tiles=n_tiles,
                             ret_total=ret_total)
    return pl.pallas_call(
        kern,
        grid=(n_tiles + 1,),
        in_specs=[
            pl.BlockSpec((tm, N_QKV), mix_row),
            pl.BlockSpec((n_chunks, 3 * SCAN_ROWS, LANES), mix_blk),
            pl.BlockSpec((n_chunks, 8, LANES), mix_blk),
            pl.BlockSpec((tm, D_MODEL), dense_row),
            _const_spec((1, D_GROUP)),
            _const_spec((1, D_GROUP)),
            _const_spec(decay.shape),
            _const_spec(cross.shape),
            _const_spec(tail.shape),
            _const_spec(sel.shape),
            _const_spec(wout.shape),
            _const_spec((1, D_MODEL)),
            _const_spec(wg.shape),
            _const_spec(wu.shape),
            _const_spec(wd.shape),
            _const_spec((1, D_MODEL)),
        ],
        out_specs=[
            pl.BlockSpec((tm, D_MODEL), dense_row),
            pl.BlockSpec((1, N_HEADS, D_HEAD, 2 * D_HEAD), stream4),
            pl.BlockSpec((1, N_HEADS, LANES), stream3),
            pl.BlockSpec((1, N_HEADS, D_HEAD, D_HEAD), stream4),
        ],
        out_shape=[
            jax.ShapeDtypeStruct((n_tok, D_MODEL), F32),
            jax.ShapeDtypeStruct((bsz, N_HEADS, D_HEAD, 2 * D_HEAD), F32),
            jax.ShapeDtypeStruct((bsz, N_HEADS, LANES), F32),
            jax.ShapeDtypeStruct((bsz, N_HEADS, D_HEAD, D_HEAD), F32),
        ],
        scratch_shapes=[
            pltpu.VMEM((tm, D_MODEL), BF16),
            pltpu.VMEM((tm, D_MODEL), F32),
            pltpu.VMEM((tm, D_MODEL), F32),
            pltpu.VMEM((N_HEADS, D_HEAD, 2 * D_HEAD), F32),
            pltpu.VMEM((8, LANES), F32),
            pltpu.VMEM((N_HEADS, D_HEAD, D_HEAD), F32),
        ],
        compiler_params=pltpu.CompilerParams(
            dimension_semantics=("arbitrary",), vmem_limit_bytes=VMEM_LIMIT),
        name="mix_out",
    )(qkv, g3, ut, x1, gnm, gnr, decay, cross, tail, sel, wout, g2, wg, wu, wd, gfin)


def _cast_cols_kernel(w_ref, o_ref):
    o_ref[...] = w_ref[:, :o_ref.shape[1]].astype(o_ref.dtype)


def _cast_leading_cols(w, n_cols):
    n_rows = w.shape[0]
    blk = MXU_DIM
    return pl.pallas_call(
        _cast_cols_kernel,
        grid=(n_rows // blk,),
        in_specs=[pl.BlockSpec((blk, w.shape[1]), lambda i: (i, 0))],
        out_specs=pl.BlockSpec((blk, n_cols), lambda i: (i, 0)),
        out_shape=jax.ShapeDtypeStruct((n_rows, n_cols), BF16),
        compiler_params=pltpu.CompilerParams(dimension_semantics=("arbitrary",)),
        name="cast_w_in",
    )(w)


def _rope_tables(pos):
    half = D_HEAD // 2
    inv = ROPE_BASE ** (-jnp.arange(half, dtype=F32) / half)
    ang = pos.astype(F32)[:, None] * inv[None, :]
    cos, sin = jnp.cos(ang), jnp.sin(ang)
    return jnp.concatenate([cos, cos], axis=-1), jnp.concatenate([-sin, sin], axis=-1)


def _layer(x, pos_tab, state, weights):
    (g1, wg1, wu1, wd1, gmix, win, wgate, bi, bf, gnm, gnr, wout, g2, wg2, wu2, wd2, gfin) = weights
    bsz, seq, _ = x.shape
    cos2, sin2 = pos_tab
    seg_len = min(seq, CHUNK)
    x1, qkv, g3, ut = _ffn_in(x.reshape(bsz * seq, D_MODEL), g1, wg1, wu1, wd1, gmix, win, wgate, bi, bf,
                              cos2, sin2, seg_len)
    if state is None:
        out, s_new, m_new, r_new = _mix_out(qkv, g3, ut, x1, bsz, gnm, gnr, wout, g2, wg2, wu2, wd2, gfin)
    else:
        c0, n0, m0, r0 = state
        s0 = jnp.concatenate([c0, jnp.broadcast_to(n0[..., None], c0.shape)], axis=-1)
        m0 = jnp.broadcast_to(m0[..., None], (bsz, N_HEADS, LANES))
        y, s_new, m_new, r_new = _mixer(qkv, g3, ut, s0, m0, r0, gnm, gnr, seg_len)
        out = _ffn_out(x1, y, wout, g2, wg2, wu2, wd2, gfin)
    new_state = (s_new[..., :D_HEAD], s_new[..., D_HEAD], m_new[..., 0], r_new)
    return out.reshape(bsz, seq, D_MODEL), new_state


def kernel(x_prompt, x_sample, state_mlstm_C, state_mlstm_n, state_mlstm_m, state_ret_R,
           g_ffn1, w_ffn1_gate, w_ffn1_up, w_ffn1_down, g_mix, w_in, b_gates, g_gn_mlstm, g_gn_ret,
           w_out, g_ffn2, w_ffn2_gate, w_ffn2_up, w_ffn2_down, g_final):
    depth = g_ffn1.shape[0]
    bp, sp, _ = x_prompt.shape
    bs, ss, _ = x_sample.shape
    assert depth == 1, "the final norm is fused into the layer's last kernel"
    assert ss < CHUNK and LANES % ss == 0 and TOKEN_TILE % ss == 0 and sp % TOKEN_TILE == 0
    pos_p = _rope_tables(jnp.arange(sp))
    cos_s, sin_s = _rope_tables(PAST_LEN + jnp.arange(ss))
    pos_s = (jnp.tile(cos_s, (TOKEN_TILE // ss, 1)), jnp.tile(sin_s, (TOKEN_TILE // ss, 1)))

    hp, hs = x_prompt, x_sample
    new_p, new_s = [], []
    for l in range(depth):
        cast = lambda w: w.astype(BF16)
        wi_t = w_in[l][:, N_QKV:N_QKV + N_HEADS].T
        wf_t = w_in[l][:, N_QKV + N_HEADS:].T
        wgate = jnp.concatenate([wi_t, wi_t, wf_t, wf_t], axis=0)
        rows8 = lambda b: jnp.broadcast_to(jnp.tile(b.astype(F32), 2)[:, None], (8, LANES))
        bi, bf = rows8(b_gates[l][:N_HEADS]), rows8(b_gates[l][N_HEADS:])
        weights = (g_ffn1[l][None, :], cast(w_ffn1_gate[l]), cast(w_ffn1_up[l]), cast(w_ffn1_down[l]),
                   g_mix[l][None, :], _cast_leading_cols(w_in[l], N_QKV), wgate, bi, bf,
                   g_gn_mlstm[l][None, :].astype(F32), g_gn_ret[l][None, :].astype(F32),
                   cast(w_out[l]), g_ffn2[l][None, :], cast(w_ffn2_gate[l]), cast(w_ffn2_up[l]),
                   cast(w_ffn2_down[l]), g_final[None, :])
        hp, st_p = _layer(hp, pos_p, None, weights)
        carried = (state_mlstm_C[l].astype(F32), state_mlstm_n[l].astype(F32),
                   state_mlstm_m[l].astype(F32), state_ret_R[l].astype(F32))
        hs, st_s = _layer(hs, pos_s, carried, weights)
        new_p.append(st_p)
        new_s.append(st_s)
    stack = lambda sts, i: jnp.stack([s[i] for s in sts])
    return (hp, hs,
            stack(new_p, 0), stack(new_p, 1), stack(new_p, 2), stack(new_p, 3),
            stack(new_s, 0), stack(new_s, 1), stack(new_s, 2), stack(new_s, 3))
```

```python
import functools

import numpy as np
import jax
import jax.numpy as jnp
from jax import lax
from jax.experimental import pallas as pl
from jax.experimental.pallas import tpu as pltpu

F32 = jnp.float32
BF16 = jnp.bfloat16

D_MODEL = 1024
N_HEADS = 4
D_HEAD = 128
D_GROUP = N_HEADS * D_HEAD
D_FF = 2816
PAST_LEN = 4096
ROPE_BASE = 10000.0
EPS = 1e-6
GN_EPS = 1e-5
LOG_GAMMA = np.log(1.0 - 2.0 ** (-5.0 - np.arange(N_HEADS))).astype(np.float32)

LANES = 128
MXU_DIM = 256
FF_CHUNK = MXU_DIM
N_FF_CHUNKS = D_FF // FF_CHUNK
CHUNK = 128
TOKEN_TILE = 512
N_QKV = 8 * D_GROUP
GATE_ROWS = 16
SCAN_ROWS = 16
N_SCAN = 3
MASKED_U = -1e30
VMEM_LIMIT = 58 * 1024 * 1024

_NT = (((1,), (1,)), ((), ()))
_TN = (((0,), (0,)), ((), ()))


def _rms_norm(x, g):
    ms = jnp.mean(x * x, axis=-1, keepdims=True)
    return x * lax.rsqrt(ms + EPS) * g


def _swiglu_chunk(acc_ref, xn, wg_ref, wu_ref, wd_ref, c):
    cols = slice(c * FF_CHUNK, (c + 1) * FF_CHUNK)
    g = jnp.dot(xn, wg_ref[:, cols], preferred_element_type=F32)
    u = jnp.dot(xn, wu_ref[:, cols], preferred_element_type=F32)
    a = (g * jax.nn.sigmoid(g) * u).astype(BF16)
    d = jnp.dot(a, wd_ref[cols, :], preferred_element_type=F32)
    if c == 0:
        acc_ref[...] = d
    else:
        acc_ref[...] += d


def _swiglu_into(acc_ref, xn, wg_ref, wu_ref, wd_ref):
    for c in range(N_FF_CHUNKS):
        _swiglu_chunk(acc_ref, xn, wg_ref, wu_ref, wd_ref, c)


def _rope_heads(p, cos2, sin2):
    outs = []
    for h in range(N_HEADS):
        xh = p[:, h * D_HEAD:(h + 1) * D_HEAD]
        outs.append(xh * cos2 + pltpu.roll(xh, D_HEAD // 2, axis=1) * sin2)
    return jnp.concatenate(outs, axis=1)


def _log_sigmoid(x):
    return jnp.minimum(x, 0.0) - jnp.log1p(jnp.exp(-jnp.abs(x)))


def _scan_lanes(x, op, fill, pos, seg_len):
    d = 1
    while d < seg_len:
        x = op(x, jnp.where(pos >= d, pltpu.roll(x, d, axis=1), fill))
        d *= 2
    return x


def _split3(x, axis):
    hi = x.astype(BF16)
    r = x - hi.astype(F32)
    mid = r.astype(BF16)
    lo = (r - mid.astype(F32)).astype(BF16)
    return jnp.concatenate([hi, mid, lo], axis=axis)


def _gate_scans(gt, bi, bf, seg_len, g3_ref, ut_ref):
    lane = lax.broadcasted_iota(jnp.int32, (8, LANES), 1)
    sub = lax.broadcasted_iota(jnp.int32, (8, LANES), 0)
    pos = lane & (seg_len - 1)
    for c in range(gt.shape[1] // LANES):
        lanes = slice(c * LANES, (c + 1) * LANES)
        ig = gt[0:8, lanes] + bi
        lf = _log_sigmoid(gt[8:16, lanes] + bf)
        b = _scan_lanes(lf, jnp.add, 0.0, pos, seg_len)
        u = ig - b
        cm = _scan_lanes(u, jnp.maximum, MASKED_U, pos, seg_len)
        packed = jnp.concatenate([jnp.where(sub < N_HEADS, cm, b), jnp.where(sub < N_HEADS, u, 0.0)], axis=0)
        g3_ref[c] = _split3(packed, axis=0)
        ut_ref[c] = u


def _ffn_in_kernel(x_ref, g1_ref, wg_ref, wu_ref, wd_ref, gmix_ref, win_ref, wgate_ref, bi_ref, bf_ref,
                   cos_ref, sin_ref, x1_ref, qkv_ref, g3_ref, ut_ref, acc_ref, *, seg_len):
    x = x_ref[...]
    xn = _rms_norm(x, g1_ref[...]).astype(BF16)
    _swiglu_into(acc_ref, xn, wg_ref, wu_ref, wd_ref)
    x1 = x + 0.5 * acc_ref[...]
    x1_ref[...] = x1
    hn = _rms_norm(x1, gmix_ref[...]).astype(BF16)
    gt = lax.dot_general(wgate_ref[...].astype(BF16), hn, _NT, preferred_element_type=F32)
    _gate_scans(gt, bi_ref[...], bf_ref[...], seg_len, g3_ref, ut_ref)
    cos2 = cos_ref[...]
    sin2 = sin_ref[...]
    key_scale = D_HEAD ** -0.5
    for j in range(8):
        cols = slice(j * D_GROUP, (j + 1) * D_GROUP)
        p = lax.dot_general(hn, win_ref[cols, :], _NT, preferred_element_type=F32)
        if j in (4, 5):
            p = _rope_heads(p, cos2, sin2)
        if j in (1, 5):
            p = p * key_scale
        qkv_ref[:, cols] = p.astype(BF16)


def _const_spec(shape):
    nd = len(shape)
    return pl.BlockSpec(shape, lambda *_: (0,) * nd, pipeline_mode=pl.Buffered(1))


def _ffn_in(x, g1, wg, wu, wd, gmix, win, wgate, bi, bf, cos2, sin2, seg_len):
    n_tok = x.shape[0]
    tm = TOKEN_TILE
    pos_tiles = cos2.shape[0] // tm
    row = lambda i: (i, 0)
    blk = lambda i: (i, 0, 0)
    pos = lambda i: (i % pos_tiles, 0)
    return pl.pallas_call(
        functools.partial(_ffn_in_kernel, seg_len=seg_len),
        grid=(n_tok // tm,),
        in_specs=[
            pl.BlockSpec((tm, D_MODEL), row),
            _const_spec((1, D_MODEL)),
            _const_spec(wg.shape),
            _const_spec(wu.shape),
            _const_spec(wd.shape),
            _const_spec((1, D_MODEL)),
            _const_spec(win.shape),
            _const_spec(wgate.shape),
            _const_spec(bi.shape),
            _const_spec(bf.shape),
            pl.BlockSpec((tm, LANES), pos),
            pl.BlockSpec((tm, LANES), pos),
        ],
        out_specs=[
            pl.BlockSpec((tm, D_MODEL), row),
            pl.BlockSpec((tm, N_QKV), row),
            pl.BlockSpec((tm // LANES, 3 * SCAN_ROWS, LANES), blk),
            pl.BlockSpec((tm // LANES, 8, LANES), blk),
        ],
        out_shape=[
            jax.ShapeDtypeStruct((n_tok, D_MODEL), F32),
            jax.ShapeDtypeStruct((n_tok, N_QKV), BF16),
            jax.ShapeDtypeStruct((n_tok // LANES, 3 * SCAN_ROWS, LANES), BF16),
            jax.ShapeDtypeStruct((n_tok // LANES, 8, LANES), F32),
        ],
        scratch_shapes=[pltpu.VMEM((tm, D_MODEL), F32)],
        compiler_params=pltpu.CompilerParams(
            dimension_semantics=("arbitrary",), vmem_limit_bytes=VMEM_LIMIT),
        name="ffn_in",
    )(x, g1, wg, wu, wd, gmix, win, wgate, bi, bf, cos2, sin2)


def _group_norm(h, g):
    mu = jnp.mean(h, axis=-1, keepdims=True)
    d = h - mu
    var = jnp.mean(d * d, axis=-1, keepdims=True)
    return d * lax.rsqrt(var + GN_EPS) * g


def _chunk_masks(frames):
    row_idx = lax.broadcasted_iota(jnp.int32, (frames, frames), 0)
    col_idx = lax.broadcasted_iota(jnp.int32, (frames, frames), 1)
    return row_idx >= col_idx, jnp.ones((frames, D_HEAD), BF16)


def _mixer_chunk(rows, qkv_ref, g3c, utc, s_ref, m_ref, r_ref, y_ref, consts, masks, ret_total, after_scores=None):
    gnm_ref, gnr_ref, decay_ref, cross_ref, tail_ref, sel_ref = consts
    causal, ones_blk = masks
    L = causal.shape[0]
    rep = lax.dot_general(g3c, sel_ref[...], _TN, preferred_element_type=F32)

    def head_slices(group, h):
        base = group * 4 * D_GROUP + h * D_HEAD
        return [qkv_ref[rows, base + j * D_GROUP:base + j * D_GROUP + D_HEAD] for j in range(4)]

    qk_m, qk_r = [], []
    for h in range(N_HEADS):
        q, k, _, _ = head_slices(0, h)
        qk_m.append(lax.dot_general(q, k, _NT, preferred_element_type=F32))
    for h in range(N_HEADS):
        q, k, _, _ = head_slices(1, h)
        qk_r.append(lax.dot_general(q, k, _NT, preferred_element_type=F32))
    if after_scores is not None:
        after_scores()

    lhs_m, kw_m, gate_m, lhs_r, kt_r = [], [], [], [], []
    for h in range(N_HEADS):
        cm_b = rep[:, (0 * N_HEADS + h) * LANES:(0 * N_HEADS + h + 1) * LANES]
        b_b = rep[:, (1 * N_HEADS + h) * LANES:(1 * N_HEADS + h + 1) * LANES]
        u_b = rep[:, (2 * N_HEADS + h) * LANES:(2 * N_HEADS + h + 1) * LANES]
        m_prev = m_ref[h:h + 1, :]
        big_m = jnp.maximum(cm_b, m_prev)
        a_b = jnp.exp(m_prev - big_m)
        m_t = b_b + big_m
        m_last = big_m[L - 1:L, :]
        m_ref[h:h + 1, :] = m_t[L - 1:L, :]
        w = jnp.where(causal, jnp.exp(utc[h:h + 1, :] - big_m[:, :L]), 0.0)
        q, k, _, _ = head_slices(0, h)
        lhs_m.append(jnp.concatenate([(q.astype(F32) * a_b).astype(BF16), (qk_m[h] * w).astype(BF16)], axis=1))
        kw_m.append((k.astype(F32) * jnp.exp(u_b - m_last)).astype(BF16))
        gate_m.append((a_b[L - 1:L, :], m_t))
    for h in range(N_HEADS):
        q, k, _, _ = head_slices(1, h)
        lhs_r.append(jnp.concatenate([(q.astype(F32) * cross_ref[h]).astype(BF16),
                                      (qk_r[h] * decay_ref[h]).astype(BF16)], axis=1))
        kt_r.append((k.astype(F32) * tail_ref[h]).astype(BF16))

    res_m, upd_m, res_r, upd_r = [], [], [], []
    for h in range(N_HEADS):
        v = head_slices(0, h)[2]
        v_aug = jnp.concatenate([v, ones_blk], axis=1)
        rhs = jnp.concatenate([s_ref[h].astype(BF16), v_aug], axis=0)
        res_m.append(jnp.dot(lhs_m[h], rhs, preferred_element_type=F32))
        upd_m.append(lax.dot_general(kw_m[h], v_aug, _TN, preferred_element_type=F32))
    for h in range(N_HEADS):
        v = head_slices(1, h)[2]
        rhs = jnp.concatenate([r_ref[h].astype(BF16), v], axis=0)
        res_r.append(jnp.dot(lhs_r[h], rhs, preferred_element_type=F32))
        upd_r.append(lax.dot_general(kt_r[h], v, _TN, preferred_element_type=F32))

    for h in range(N_HEADS):
        hc = slice(h * D_HEAD, (h + 1) * D_HEAD)
        a_last, m_t = gate_m[h]
        num = res_m[h][:, :D_HEAD]
        den = res_m[h][:, D_HEAD:]
        hh = num / jnp.maximum(jnp.abs(den), jnp.exp(-m_t))
        s_ref[h] = jnp.concatenate([a_last, a_last], axis=1) * s_ref[h] + upd_m[h]
        o = head_slices(0, h)[3]
        y = jax.nn.sigmoid(o.astype(F32)) * _group_norm(hh, gnm_ref[:, hc])
        y_ref[rows, hc] = y.astype(BF16)
    for h in range(N_HEADS):
        hc = slice(h * D_HEAD, (h + 1) * D_HEAD)
        r_ref[h] = ret_total[h] * r_ref[h] + upd_r[h]
        gf = head_slices(1, h)[3].astype(F32)
        y = gf * jax.nn.sigmoid(gf) * _group_norm(res_r[h], gnr_ref[:, hc])
        y_ref[rows, D_GROUP + h * D_HEAD:D_GROUP + (h + 1) * D_HEAD] = y.astype(BF16)


def _mixer_kernel(qkv_ref, g3_ref, ut_ref, s0_ref, m0_ref, r0_ref, gnm_ref, gnr_ref,
                  decay_ref, cross_ref, tail_ref, sel_ref,
                  y_ref, s_ref, m_ref, r_ref, *, frames, ret_total):
    s_ref[...] = s0_ref[...]
    m_ref[...] = m0_ref[...]
    r_ref[...] = r0_ref[...]
    consts = (gnm_ref, gnr_ref, decay_ref, cross_ref, tail_ref, sel_ref)
    masks = _chunk_masks(frames)
    for i in range(LANES // frames):
        seg = slice(i * frames, (i + 1) * frames)
        _mixer_chunk(seg, qkv_ref, g3_ref[0][:, seg], ut_ref[0][:, seg],
                     s_ref.at[i], m_ref.at[i], r_ref.at[i], y_ref, consts, masks, ret_total)


def _mix_out_kernel(qkv_ref, g3_ref, ut_ref, x1_ref, x1s_ref, ys_ref,
                    gnm_ref, gnr_ref, decay_ref, cross_ref, tail_ref, sel_ref,
                    wout_ref, g2_ref, wg_ref, wu_ref, wd_ref, gfin_ref,
                    out_ref, outs_ref, s_out, m_out, r_out,
                    y_scr, acc_ref, x2_ref, s_scr, m_scr, r_scr, *, tiles_per_stream, n_tiles, ret_total):
    j = pl.program_id(0)
    n_chunks = TOKEN_TILE // CHUNK

    @pl.when(j == 0)
    def _():
        y_scr[...] = ys_ref[...]

    @pl.when(j % tiles_per_stream == 0)
    def _():
        s_scr[...] = jnp.zeros_like(s_scr)
        m_scr[...] = jnp.zeros_like(m_scr)
        r_scr[...] = jnp.zeros_like(r_scr)

    consts = (gnm_ref, gnr_ref, decay_ref, cross_ref, tail_ref, sel_ref)
    masks = _chunk_masks(CHUNK)

    x1 = jnp.where(j == 0, x1s_ref[...], x1_ref[...])
    x2 = x1 + jnp.dot(y_scr[...], wout_ref[...], preferred_element_type=F32)
    x2_ref[...] = x2
    xn = _rms_norm(x2, g2_ref[...]).astype(BF16)

    pending = list(range(N_FF_CHUNKS))

    def ffn(k):
        for _ in range(min(k, len(pending))):
            _swiglu_chunk(acc_ref, xn, wg_ref, wu_ref, wd_ref, pending.pop(0))

    for c in range(n_chunks):
        _mixer_chunk(slice(c * CHUNK, (c + 1) * CHUNK), qkv_ref, g3_ref[c], ut_ref[c],
                     s_scr, m_scr, r_scr, y_scr, consts, masks, ret_total,
                     after_scores=functools.partial(ffn, 1))
        ffn(1)
    ffn(N_FF_CHUNKS)

    x3 = x2_ref[...] + 0.5 * acc_ref[...]
    out_ref[...] = _rms_norm(x3, gfin_ref[...])

    @pl.when(j == 0)
    def _():
        outs_ref[...] = out_ref[...]

    @pl.when((j % tiles_per_stream == tiles_per_stream - 1) & (j < n_tiles))
    def _():
        s_out[0] = s_scr[...]
        m_out[0] = m_scr[0:N_HEADS, :]
        r_out[0] = r_scr[...]


def _mixer_consts(frames):
    idx = np.arange(frames, dtype=np.float32)
    diff = idx[:, None] - idx[None, :]
    lg = LOG_GAMMA
    decay = np.where(diff >= 0, np.exp(lg[:, None, None] * np.maximum(diff, 0.0)), 0.0).astype(np.float32)
    rep = lambda col: np.ascontiguousarray(np.broadcast_to(col.T[:, :, None], (N_HEADS, frames, LANES))).astype(np.float32)
    cross = rep(np.exp(lg[None, :] * (idx[:, None] + 1.0)))
    tail = rep(np.exp(lg[None, :] * (frames - 1.0 - idx[:, None])))
    ret_total = tuple(float(np.exp(np.float32(lg_h) * np.float32(frames))) for lg_h in lg)
    sel = np.zeros((3, SCAN_ROWS, N_SCAN * N_HEADS, LANES), np.float32)
    for j in range(N_SCAN * N_HEADS):
        sel[:, j, j, :] = 1.0
    sel = sel.reshape(3 * SCAN_ROWS, N_SCAN * N_HEADS * LANES)
    return (jnp.asarray(decay), jnp.asarray(cross), jnp.asarray(tail), jnp.asarray(sel, dtype=BF16)), ret_total


def _mixer(qkv, g3, ut, s0, m0, r0, gnm, gnr, frames):
    n_tok = qkv.shape[0]
    per_blk = LANES // frames
    (decay, cross, tail, sel), ret_total = _mixer_consts(frames)
    row = lambda i: (i, 0)
    blk3 = lambda i: (i, 0, 0)
    blk4 = lambda i: (i, 0, 0, 0)
    return pl.pallas_call(
        functools.partial(_mixer_kernel, frames=frames, ret_total=ret_total),
        grid=(n_tok // LANES,),
        in_specs=[
            pl.BlockSpec((LANES, N_QKV), row),
            pl.BlockSpec((1, 3 * SCAN_ROWS, LANES), blk3),
            pl.BlockSpec((1, 8, LANES), blk3),
            pl.BlockSpec((per_blk, N_HEADS, D_HEAD, 2 * D_HEAD), blk4),
            pl.BlockSpec((per_blk, N_HEADS, LANES), blk3),
            pl.BlockSpec((per_blk, N_HEADS, D_HEAD, D_HEAD), blk4),
            _const_spec((1, D_GROUP)),
            _const_spec((1, D_GROUP)),
            _const_spec(decay.shape),
            _const_spec(cross.shape),
            _const_spec(tail.shape),
            _const_spec(sel.shape),
        ],
        out_specs=[
            pl.BlockSpec((LANES, D_MODEL), row),
            pl.BlockSpec((per_blk, N_HEADS, D_HEAD, 2 * D_HEAD), blk4),
            pl.BlockSpec((per_blk, N_HEADS, LANES), blk3),
            pl.BlockSpec((per_blk, N_HEADS, D_HEAD, D_HEAD), blk4),
        ],
        out_shape=[
            jax.ShapeDtypeStruct((n_tok, D_MODEL), BF16),
            jax.ShapeDtypeStruct(s0.shape, F32),
            jax.ShapeDtypeStruct(m0.shape, F32),
            jax.ShapeDtypeStruct(r0.shape, F32),
        ],
        compiler_params=pltpu.CompilerParams(
            dimension_semantics=("arbitrary",), vmem_limit_bytes=VMEM_LIMIT),
        name="mixer",
    )(qkv, g3, ut, s0, m0, r0, gnm, gnr, decay, cross, tail, sel)


def _mix_out(qkv, g3, ut, x1, x1_short, y_short, bsz, gnm, gnr, wout, g2, wg, wu, wd, gfin):
    n_tok = x1.shape[0]
    tm = TOKEN_TILE
    assert x1_short.shape == (tm, D_MODEL) and y_short.shape == (tm, D_MODEL)
    n_tiles = n_tok // tm
    tiles_per_stream = n_tiles // bsz
    n_chunks = tm // CHUNK
    (decay, cross, tail, sel), ret_total = _mixer_consts(CHUNK)

    mix_row = lambda j: (jnp.minimum(j, n_tiles - 1), 0)
    mix_blk = lambda j: (jnp.minimum(j, n_tiles - 1), 0, 0)
    dense_row = lambda j: (jnp.maximum(j - 1, 0), 0)
    stream4 = lambda j: (jnp.minimum(j, n_tiles - 1) // tiles_per_stream, 0, 0, 0)
    stream3 = lambda j: (jnp.minimum(j, n_tiles - 1) // tiles_per_stream, 0, 0)
    kern = functools.partial(_mix_out_kernel, tiles_per_stream=tiles_per_stream, n_tiles=n_tiles,
                             ret_total=ret_total)
    return pl.pallas_call(
        kern,
        grid=(n_tiles + 1,),
        in_specs=[
            pl.BlockSpec((tm, N_QKV), mix_row),
            pl.BlockSpec((n_chunks, 3 * SCAN_ROWS, LANES), mix_blk),
            pl.BlockSpec((n_chunks, 8, LANES), mix_blk),
            pl.BlockSpec((tm, D_MODEL), dense_row),
            _const_spec((tm, D_MODEL)),
            _const_spec((tm, D_MODEL)),
            _const_spec((1, D_GROUP)),
            _const_spec((1, D_GROUP)),
            _const_spec(decay.shape),
            _const_spec(cross.shape),
            _const_spec(tail.shape),
            _const_spec(sel.shape),
            _const_spec(wout.shape),
            _const_spec((1, D_MODEL)),
            _const_spec(wg.shape),
            _const_spec(wu.shape),
            _const_spec(wd.shape),
            _const_spec((1, D_MODEL)),
        ],
        out_specs=[
            pl.BlockSpec((tm, D_MODEL), dense_row),
            pl.BlockSpec((tm, D_MODEL), lambda j: (0, 0)),
            pl.BlockSpec((1, N_HEADS, D_HEAD, 2 * D_HEAD), stream4),
            pl.BlockSpec((1, N_HEADS, LANES), stream3),
            pl.BlockSpec((1, N_HEADS, D_HEAD, D_HEAD), stream4),
        ],
        out_shape=[
            jax.ShapeDtypeStruct((n_tok, D_MODEL), F32),
            jax.ShapeDtypeStruct((tm, D_MODEL), F32),
            jax.ShapeDtypeStruct((bsz, N_HEADS, D_HEAD, 2 * D_HEAD), F32),
            jax.ShapeDtypeStruct((bsz, N_HEADS, LANES), F32),
            jax.ShapeDtypeStruct((bsz, N_HEADS, D_HEAD, D_HEAD), F32),
        ],
        scratch_shapes=[
            pltpu.VMEM((tm, D_MODEL), BF16),
            pltpu.VMEM((tm, D_MODEL), F32),
            pltpu.VMEM((tm, D_MODEL), F32),
            pltpu.VMEM((N_HEADS, D_HEAD, 2 * D_HEAD), F32),
            pltpu.VMEM((8, LANES), F32),
            pltpu.VMEM((N_HEADS, D_HEAD, D_HEAD), F32),
        ],
        compiler_params=pltpu.CompilerParams(
            dimension_semantics=("arbitrary",), vmem_limit_bytes=VMEM_LIMIT),
        name="mix_out",
    )(qkv, g3, ut, x1, x1_short, y_short, gnm, gnr, decay, cross, tail, sel, wout, g2, wg, wu, wd, gfin)


def _rope_tables(pos):
    half = D_HEAD // 2
    inv = ROPE_BASE ** (-jnp.arange(half, dtype=F32) / half)
    ang = pos.astype(F32)[:, None] * inv[None, :]
    cos, sin = jnp.cos(ang), jnp.sin(ang)
    return jnp.concatenate([cos, cos], axis=-1), jnp.concatenate([-sin, sin], axis=-1)


def _pack_state(c, n, m):
    s = jnp.concatenate([c, jnp.broadcast_to(n[..., None], c.shape)], axis=-1)
    return s, jnp.broadcast_to(m[..., None], m.shape + (LANES,))


def _unpack_state(s, m, r):
    return s[..., :D_HEAD], s[..., D_HEAD], m[..., 0], r


def kernel(x_prompt, x_sample, state_mlstm_C, state_mlstm_n, state_mlstm_m, state_ret_R,
           g_ffn1, w_ffn1_gate, w_ffn1_up, w_ffn1_down, g_mix, w_in, b_gates, g_gn_mlstm, g_gn_ret,
           w_out, g_ffn2, w_ffn2_gate, w_ffn2_up, w_ffn2_down, g_final):
    depth = g_ffn1.shape[0]
    bp, sp, _ = x_prompt.shape
    bs, ss, _ = x_sample.shape
    assert depth == 1, "the final norm is fused into the layer's last kernel"
    assert ss < CHUNK and LANES % ss == 0 and bs * ss == TOKEN_TILE and sp % TOKEN_TILE == 0
    pos_p = _rope_tables(jnp.arange(sp))
    cos_s, sin_s = _rope_tables(PAST_LEN + jnp.arange(ss))
    pos_s = (jnp.tile(cos_s, (TOKEN_TILE // ss, 1)), jnp.tile(sin_s, (TOKEN_TILE // ss, 1)))

    new_p, new_s = [], []
    for l in range(depth):
        cast = lambda w: w.astype(BF16)
        w_in_t = w_in[l].T
        wi_t = w_in_t[N_QKV:N_QKV + N_HEADS]
        wf_t = w_in_t[N_QKV + N_HEADS:]
        wgate = jnp.concatenate([wi_t, wi_t, wf_t, wf_t], axis=0)
        rows8 = lambda b: jnp.broadcast_to(jnp.tile(b.astype(F32), 2)[:, None], (8, LANES))
        in_w = (g_ffn1[l][None, :], cast(w_ffn1_gate[l]), cast(w_ffn1_up[l]), cast(w_ffn1_down[l]),
                g_mix[l][None, :], cast(w_in_t[:N_QKV]), wgate,
                rows8(b_gates[l][:N_HEADS]), rows8(b_gates[l][N_HEADS:]))
        gnm, gnr = g_gn_mlstm[l][None, :].astype(F32), g_gn_ret[l][None, :].astype(F32)
        out_w = (cast(w_out[l]), g_ffn2[l][None, :], cast(w_ffn2_gate[l]), cast(w_ffn2_up[l]),
                 cast(w_ffn2_down[l]), g_final[None, :])

        x1s, qkv_s, g3_s, ut_s = _ffn_in(x_sample.reshape(bs * ss, D_MODEL), *in_w, *pos_s, ss)
        s0, m0 = _pack_state(state_mlstm_C[l].astype(F32), state_mlstm_n[l].astype(F32),
                             state_mlstm_m[l].astype(F32))
        ys, s_s, m_s, r_s = _mixer(qkv_s, g3_s, ut_s, s0, m0, state_ret_R[l].astype(F32), gnm, gnr, ss)
        x1p, qkv_p, g3_p, ut_p = _ffn_in(x_prompt.reshape(bp * sp, D_MODEL), *in_w, *pos_p, CHUNK)
        out_p, out_s, s_p, m_p, r_p = _mix_out(qkv_p, g3_p, ut_p, x1p, x1s, ys, bp, gnm, gnr, *out_w)
        new_p.append(_unpack_state(s_p, m_p, r_p))
        new_s.append(_unpack_state(s_s, m_s, r_s))
    hp, hs = out_p.reshape(bp, sp, D_MODEL), out_s.reshape(bs, ss, D_MODEL)
    stack = lambda sts, i: jnp.stack([s[i] for s in sts])
    return (hp, hs,
            stack(new_p, 0), stack(new_p, 1), stack(new_p, 2), stack(new_p, 3),
            stack(new_s, 0), stack(new_s, 1), stack(new_s, 2), stack(new_s, 3))
```

```python
import functools

import numpy as np
import jax
import jax.numpy as jnp
from jax import lax
from jax.experimental import pallas as pl
from jax.experimental.pallas import tpu as pltpu

F32 = jnp.float32
BF16 = jnp.bfloat16

D_MODEL = 1024
N_HEADS = 4
D_HEAD = 128
D_GROUP = N_HEADS * D_HEAD
D_FF = 2816
PAST_LEN = 4096
ROPE_BASE = 10000.0
EPS = 1e-6
GN_EPS = 1e-5
LOG_GAMMA = np.log(1.0 - 2.0 ** (-5.0 - np.arange(N_HEADS))).astype(np.float32)

LANES = 128
MXU_DIM = 256
FF_CHUNK = MXU_DIM
N_FF_CHUNKS = D_FF // FF_CHUNK
CHUNK = 128
TOKEN_TILE = 512
N_QKV = 8 * D_GROUP
GATE_ROWS = 16
SCAN_COLS = 12
MASKED_U = -1e30
VMEM_LIMIT = 58 * 1024 * 1024

_NT = (((1,), (1,)), ((), ()))
_TN = (((0,), (0,)), ((), ()))


def _rms_norm(x, g):
    ms = jnp.mean(x * x, axis=-1, keepdims=True)
    return x * lax.rsqrt(ms + EPS) * g


def _swiglu_chunk(acc_ref, xn, wg_ref, wu_ref, wd_ref, c):
    cols = slice(c * FF_CHUNK, (c + 1) * FF_CHUNK)
    g = jnp.dot(xn, wg_ref[:, cols], preferred_element_type=F32)
    u = jnp.dot(xn, wu_ref[:, cols], preferred_element_type=F32)
    a = (g * jax.nn.sigmoid(g) * u).astype(BF16)
    d = jnp.dot(a, wd_ref[cols, :], preferred_element_type=F32)
    if c == 0:
        acc_ref[...] = d
    else:
        acc_ref[...] += d


def _swiglu_into(acc_ref, xn, wg_ref, wu_ref, wd_ref):
    for c in range(N_FF_CHUNKS):
        _swiglu_chunk(acc_ref, xn, wg_ref, wu_ref, wd_ref, c)


def _rope_heads(p, cos2, sin2):
    outs = []
    for h in range(N_HEADS):
        xh = p[:, h * D_HEAD:(h + 1) * D_HEAD]
        outs.append(xh * cos2 + pltpu.roll(xh, D_HEAD // 2, axis=1) * sin2)
    return jnp.concatenate(outs, axis=1)


def _log_sigmoid(x):
    return jnp.minimum(x, 0.0) - jnp.log1p(jnp.exp(-jnp.abs(x)))


def _scan_lanes(x, op, fill, pos, seg_len):
    d = 1
    while d < seg_len:
        x = op(x, jnp.where(pos >= d, pltpu.roll(x, d, axis=1), fill))
        d *= 2
    return x


def _gate_scans(gt, bi, bf, seg_len, gc_ref, ut_ref):
    pad_rows = jnp.zeros((LANES - 16, LANES), F32)
    lane = lax.broadcasted_iota(jnp.int32, (8, LANES), 1)
    sub = lax.broadcasted_iota(jnp.int32, (8, LANES), 0)
    pos = lane & (seg_len - 1)
    for c in range(gt.shape[1] // LANES):
        lanes = slice(c * LANES, (c + 1) * LANES)
        ig = gt[0:8, lanes] + bi
        lf = _log_sigmoid(gt[8:16, lanes] + bf)
        b = _scan_lanes(lf, jnp.add, 0.0, pos, seg_len)
        u = ig - b
        cm = _scan_lanes(u, jnp.maximum, MASKED_U, pos, seg_len)
        packed = jnp.concatenate([jnp.where(sub < N_HEADS, cm, b), jnp.where(sub < N_HEADS, u, 0.0)], axis=0)
        gc_ref[c] = jnp.concatenate([packed, pad_rows], axis=0).T
        ut_ref[c] = u


def _ffn_in_kernel(x_ref, g1_ref, wg_ref, wu_ref, wd_ref, gmix_ref, win_ref, wgate_ref, bi_ref, bf_ref,
                   cos_ref, sin_ref, x1_ref, qkv_ref, gc_ref, ut_ref, acc_ref, *, seg_len):
    x = x_ref[...]
    xn = _rms_norm(x, g1_ref[...]).astype(BF16)
    _swiglu_into(acc_ref, xn, wg_ref, wu_ref, wd_ref)
    x1 = x + 0.5 * acc_ref[...]
    x1_ref[...] = x1
    hn = _rms_norm(x1, gmix_ref[...]).astype(BF16)
    gt = lax.dot_general(wgate_ref[...].astype(BF16), hn, _NT, preferred_element_type=F32)
    _gate_scans(gt, bi_ref[...], bf_ref[...], seg_len, gc_ref, ut_ref)
    cos2 = cos_ref[...]
    sin2 = sin_ref[...]
    key_scale = D_HEAD ** -0.5
    for j in range(8):
        cols = slice(j * D_GROUP, (j + 1) * D_GROUP)
        p = lax.dot_general(hn, win_ref[cols, :], _NT, preferred_element_type=F32)
        if j in (4, 5):
            p = _rope_heads(p, cos2, sin2)
        if j in (1, 5):
            p = p * key_scale
        qkv_ref[:, cols] = p.astype(BF16)


def _const_spec(shape):
    nd = len(shape)
    return pl.BlockSpec(shape, lambda *_: (0,) * nd, pipeline_mode=pl.Buffered(1))


def _ffn_in(x, g1, wg, wu, wd, gmix, win, wgate, bi, bf, cos2, sin2, seg_len):
    n_tok = x.shape[0]
    tm = TOKEN_TILE
    pos_tiles = cos2.shape[0] // tm
    row = lambda i: (i, 0)
    blk = lambda i: (i, 0, 0)
    pos = lambda i: (i % pos_tiles, 0)
    return pl.pallas_call(
        functools.partial(_ffn_in_kernel, seg_len=seg_len),
        grid=(n_tok // tm,),
        in_specs=[
            pl.BlockSpec((tm, D_MODEL), row),
            _const_spec((1, D_MODEL)),
            _const_spec(wg.shape),
            _const_spec(wu.shape),
            _const_spec(wd.shape),
            _const_spec((1, D_MODEL)),
            _const_spec(win.shape),
            _const_spec(wgate.shape),
            _const_spec(bi.shape),
            _const_spec(bf.shape),
            pl.BlockSpec((tm, LANES), pos),
            pl.BlockSpec((tm, LANES), pos),
        ],
        out_specs=[
            pl.BlockSpec((tm, D_MODEL), row),
            pl.BlockSpec((tm, N_QKV), row),
            pl.BlockSpec((tm // LANES, LANES, LANES), blk),
            pl.BlockSpec((tm // LANES, 8, LANES), blk),
        ],
        out_shape=[
            jax.ShapeDtypeStruct((n_tok, D_MODEL), F32),
            jax.ShapeDtypeStruct((n_tok, N_QKV), BF16),
            jax.ShapeDtypeStruct((n_tok // LANES, LANES, LANES), F32),
            jax.ShapeDtypeStruct((n_tok // LANES, 8, LANES), F32),
        ],
        scratch_shapes=[pltpu.VMEM((tm, D_MODEL), F32)],
        compiler_params=pltpu.CompilerParams(
            dimension_semantics=("arbitrary",), vmem_limit_bytes=VMEM_LIMIT),
        name="ffn_in",
    )(x, g1, wg, wu, wd, gmix, win, wgate, bi, bf, cos2, sin2)


def _group_norm(h, g):
    mu = jnp.mean(h, axis=-1, keepdims=True)
    d = h - mu
    var = jnp.mean(d * d, axis=-1, keepdims=True)
    return d * lax.rsqrt(var + GN_EPS) * g


def _chunk_masks(frames):
    row_idx = lax.broadcasted_iota(jnp.int32, (frames, frames), 0)
    col_idx = lax.broadcasted_iota(jnp.int32, (frames, frames), 1)
    return row_idx >= col_idx, jnp.ones((frames, D_HEAD), BF16)


def _mixer_chunk(rows, qkv_ref, gcc, utc, s_ref, m_ref, r_ref, y_ref, consts, masks, ret_total, after_scores=None):
    gnm_ref, gnr_ref, decay_ref, cross_ref, tail_ref = consts
    causal, ones_blk = masks
    L = causal.shape[0]

    def scan_col(q, h):
        j = q * N_HEADS + h
        return jnp.broadcast_to(gcc[:, j:j + 1], (L, LANES))

    def head_slices(group, h):
        base = group * 4 * D_GROUP + h * D_HEAD
        return [qkv_ref[rows, base + j * D_GROUP:base + j * D_GROUP + D_HEAD] for j in range(4)]

    def scores(group):
        if L < LANES:
            return [lax.dot_general(*head_slices(group, h)[:2], _NT, preferred_element_type=F32)
                    for h in range(N_HEADS)]
        out = []
        zero_blk = jnp.zeros((L, D_HEAD), BF16)
        for h in range(0, N_HEADS, 2):
            base = group * 4 * D_GROUP + h * D_HEAD
            q2 = qkv_ref[rows, base:base + 2 * D_HEAD]
            k_a, k_b = head_slices(group, h)[1], head_slices(group, h + 1)[1]
            k_bd = jnp.concatenate([jnp.concatenate([k_a, zero_blk], axis=1),
                                    jnp.concatenate([zero_blk, k_b], axis=1)], axis=0)
            qk2 = lax.dot_general(q2, k_bd, _NT, preferred_element_type=F32)
            out += [qk2[:, :L], qk2[:, L:]]
        return out

    qk_m, qk_r = scores(0), scores(1)
    if after_scores is not None:
        after_scores()

    lhs_m, kw_m, gate_m, lhs_r, kt_r = [], [], [], [], []
    for h in range(N_HEADS):
        cm_b, b_b, u_b = scan_col(0, h), scan_col(1, h), scan_col(2, h)
        m_prev = m_ref[h:h + 1, :]
        big_m = jnp.maximum(cm_b, m_prev)
        a_b = jnp.exp(m_prev - big_m)
        m_t = b_b + big_m
        m_last = big_m[L - 1:L, :]
        m_ref[h:h + 1, :] = m_t[L - 1:L, :]
        w = jnp.where(causal, jnp.exp(utc[h:h + 1, :] - big_m[:, :L]), 0.0)
        q, k, _, _ = head_slices(0, h)
        lhs_m.append(jnp.concatenate([(q.astype(F32) * a_b).astype(BF16), (qk_m[h] * w).astype(BF16)], axis=1))
        kw_m.append((k.astype(F32) * jnp.exp(u_b - m_last)).astype(BF16))
        gate_m.append((a_b[L - 1:L, :], m_t))
    for h in range(N_HEADS):
        q, k, _, _ = head_slices(1, h)
        lhs_r.append(jnp.concatenate([(q.astype(F32) * cross_ref[h]).astype(BF16),
                                      (qk_r[h] * decay_ref[h]).astype(BF16)], axis=1))
        kt_r.append((k.astype(F32) * tail_ref[h]).astype(BF16))

    res_m, upd_m, res_r, upd_r = [], [], [], []
    for h in range(N_HEADS):
        v = head_slices(0, h)[2]
        v_aug = jnp.concatenate([v, ones_blk], axis=1)
        rhs = jnp.concatenate([s_ref[h].astype(BF16), v_aug], axis=0)
        res_m.append(jnp.dot(lhs_m[h], rhs, preferred_element_type=F32))
        upd_m.append(lax.dot_general(kw_m[h], v_aug, _TN, preferred_element_type=F32))
    for h in range(N_HEADS):
        v = head_slices(1, h)[2]
        rhs = jnp.concatenate([r_ref[h].astype(BF16), v], axis=0)
        res_r.append(jnp.dot(lhs_r[h], rhs, preferred_element_type=F32))
        upd_r.append(lax.dot_general(kt_r[h], v, _TN, preferred_element_type=F32))

    for h in range(N_HEADS):
        hc = slice(h * D_HEAD, (h + 1) * D_HEAD)
        a_last, m_t = gate_m[h]
        num = res_m[h][:, :D_HEAD]
        den = res_m[h][:, D_HEAD:]
        hh = num / jnp.maximum(jnp.abs(den), jnp.exp(-m_t))
        s_ref[h] = jnp.concatenate([a_last, a_last], axis=1) * s_ref[h] + upd_m[h]
        o = head_slices(0, h)[3]
        y = jax.nn.sigmoid(o.astype(F32)) * _group_norm(hh, gnm_ref[:, hc])
        y_ref[rows, hc] = y.astype(BF16)
    for h in range(N_HEADS):
        hc = slice(h * D_HEAD, (h + 1) * D_HEAD)
        r_ref[h] = ret_total[h] * r_ref[h] + upd_r[h]
        gf = head_slices(1, h)[3].astype(F32)
        y = gf * jax.nn.sigmoid(gf) * _group_norm(res_r[h], gnr_ref[:, hc])
        y_ref[rows, D_GROUP + h * D_HEAD:D_GROUP + (h + 1) * D_HEAD] = y.astype(BF16)


def _mixer_kernel(qkv_ref, gc_ref, ut_ref, s0_ref, m0_ref, r0_ref, gnm_ref, gnr_ref,
                  decay_ref, cross_ref, tail_ref,
                  y_ref, s_ref, m_ref, r_ref, *, frames, ret_total):
    s_ref[...] = s0_ref[...]
    m_ref[...] = m0_ref[...]
    r_ref[...] = r0_ref[...]
    consts = (gnm_ref, gnr_ref, decay_ref, cross_ref, tail_ref)
    masks = _chunk_masks(frames)
    for i in range(LANES // frames):
        seg = slice(i * frames, (i + 1) * frames)
        _mixer_chunk(seg, qkv_ref, gc_ref[0][seg, :], ut_ref[0][:, seg],
                     s_ref.at[i], m_ref.at[i], r_ref.at[i], y_ref, consts, masks, ret_total)


def _mix_out_kernel(qkv_ref, gc_ref, ut_ref, x1_ref, x1s_ref, ys_ref,
                    gnm_ref, gnr_ref, decay_ref, cross_ref, tail_ref,
                    wout_ref, g2_ref, wg_ref, wu_ref, wd_ref, gfin_ref,
                    out_ref, outs_ref, s_out, m_out, r_out,
                    y_scr, acc_ref, x2_ref, s_scr, m_scr, r_scr, *, tiles_per_stream, n_tiles, ret_total):
    j = pl.program_id(0)
    n_chunks = TOKEN_TILE // CHUNK

    @pl.when(j == 0)
    def _():
        y_scr[...] = ys_ref[...]

    @pl.when(j % tiles_per_stream == 0)
    def _():
        s_scr[...] = jnp.zeros_like(s_scr)
        m_scr[...] = jnp.zeros_like(m_scr)
        r_scr[...] = jnp.zeros_like(r_scr)

    consts = (gnm_ref, gnr_ref, decay_ref, cross_ref, tail_ref)
    masks = _chunk_masks(CHUNK)

    x1 = jnp.where(j == 0, x1s_ref[...], x1_ref[...])
    x2 = x1 + jnp.dot(y_scr[...], wout_ref[...], preferred_element_type=F32)
    x2_ref[...] = x2
    xn = _rms_norm(x2, g2_ref[...]).astype(BF16)

    pending = list(range(N_FF_CHUNKS))

    def ffn(k):
        for _ in range(min(k, len(pending))):
            _swiglu_chunk(acc_ref, xn, wg_ref, wu_ref, wd_ref, pending.pop(0))

    for c in range(n_chunks):
        _mixer_chunk(slice(c * CHUNK, (c + 1) * CHUNK), qkv_ref, gc_ref[c], ut_ref[c],
                     s_scr, m_scr, r_scr, y_scr, consts, masks, ret_total,
                     after_scores=functools.partial(ffn, 1))
        ffn(1)
    ffn(N_FF_CHUNKS)

    x3 = x2_ref[...] + 0.5 * acc_ref[...]
    out_ref[...] = _rms_norm(x3, gfin_ref[...])

    @pl.when(j == 0)
    def _():
        outs_ref[...] = out_ref[...]

    @pl.when((j % tiles_per_stream == tiles_per_stream - 1) & (j < n_tiles))
    def _():
        s_out[0] = s_scr[...]
        m_out[0] = m_scr[0:N_HEADS, :]
        r_out[0] = r_scr[...]


def _mixer_consts(frames):
    idx = np.arange(frames, dtype=np.float32)
    diff = idx[:, None] - idx[None, :]
    lg = LOG_GAMMA
    decay = np.where(diff >= 0, np.exp(lg[:, None, None] * np.maximum(diff, 0.0)), 0.0).astype(np.float32)
    rep = lambda col: np.ascontiguousarray(np.broadcast_to(col.T[:, :, None], (N_HEADS, frames, LANES))).astype(np.float32)
    cross = rep(np.exp(lg[None, :] * (idx[:, None] + 1.0)))
    tail = rep(np.exp(lg[None, :] * (frames - 1.0 - idx[:, None])))
    ret_total = tuple(float(np.exp(np.float32(lg_h) * np.float32(frames))) for lg_h in lg)
    return (jnp.asarray(decay), jnp.asarray(cross), jnp.asarray(tail)), ret_total


def _mixer(qkv, gc, ut, s0, m0, r0, gnm, gnr, frames):
    n_tok = qkv.shape[0]
    per_blk = LANES // frames
    (decay, cross, tail), ret_total = _mixer_consts(frames)
    row = lambda i: (i, 0)
    blk3 = lambda i: (i, 0, 0)
    blk4 = lambda i: (i, 0, 0, 0)
    return pl.pallas_call(
        functools.partial(_mixer_kernel, frames=frames, ret_total=ret_total),
        grid=(n_tok // LANES,),
        in_specs=[
            pl.BlockSpec((LANES, N_QKV), row),
            pl.BlockSpec((1, LANES, LANES), blk3),
            pl.BlockSpec((1, 8, LANES), blk3),
            pl.BlockSpec((per_blk, N_HEADS, D_HEAD, 2 * D_HEAD), blk4),
            pl.BlockSpec((per_blk, N_HEADS, LANES), blk3),
            pl.BlockSpec((per_blk, N_HEADS, D_HEAD, D_HEAD), blk4),
            _const_spec((1, D_GROUP)),
            _const_spec((1, D_GROUP)),
            _const_spec(decay.shape),
            _const_spec(cross.shape),
            _const_spec(tail.shape),
        ],
        out_specs=[
            pl.BlockSpec((LANES, D_MODEL), row),
            pl.BlockSpec((per_blk, N_HEADS, D_HEAD, 2 * D_HEAD), blk4),
            pl.BlockSpec((per_blk, N_HEADS, LANES), blk3),
            pl.BlockSpec((per_blk, N_HEADS, D_HEAD, D_HEAD), blk4),
        ],
        out_shape=[
            jax.ShapeDtypeStruct((n_tok, D_MODEL), BF16),
            jax.ShapeDtypeStruct(s0.shape, F32),
            jax.ShapeDtypeStruct(m0.shape, F32),
            jax.ShapeDtypeStruct(r0.shape, F32),
        ],
        compiler_params=pltpu.CompilerParams(
            dimension_semantics=("arbitrary",), vmem_limit_bytes=VMEM_LIMIT),
        name="mixer",
    )(qkv, gc, ut, s0, m0, r0, gnm, gnr, decay, cross, tail)


def _mix_out(qkv, gc, ut, x1, x1_short, y_short, bsz, gnm, gnr, wout, g2, wg, wu, wd, gfin):
    n_tok = x1.shape[0]
    tm = TOKEN_TILE
    assert x1_short.shape == (tm, D_MODEL) and y_short.shape == (tm, D_MODEL)
    n_tiles = n_tok // tm
    tiles_per_stream = n_tiles // bsz
    n_chunks = tm // CHUNK
    (decay, cross, tail), ret_total = _mixer_consts(CHUNK)

    mix_row = lambda j: (jnp.minimum(j, n_tiles - 1), 0)
    mix_blk = lambda j: (jnp.minimum(j, n_tiles - 1), 0, 0)
    dense_row = lambda j: (jnp.maximum(j - 1, 0), 0)
    stream4 = lambda j: (jnp.minimum(j, n_tiles - 1) // tiles_per_stream, 0, 0, 0)
    stream3 = lambda j: (jnp.minimum(j, n_tiles - 1) // tiles_per_stream, 0, 0)
    kern = functools.partial(_mix_out_kernel, tiles_per_stream=tiles_per_stream, n_tiles=n_tiles,
                             ret_total=ret_total)
    return pl.pallas_call(
        kern,
        grid=(n_tiles + 1,),
        in_specs=[
            pl.BlockSpec((tm, N_QKV), mix_row),
            pl.BlockSpec((n_chunks, LANES, LANES), mix_blk),
            pl.BlockSpec((n_chunks, 8, LANES), mix_blk),
            pl.BlockSpec((tm, D_MODEL), dense_row),
            _const_spec((tm, D_MODEL)),
            _const_spec((tm, D_MODEL)),
            _const_spec((1, D_GROUP)),
            _const_spec((1, D_GROUP)),
            _const_spec(decay.shape),
            _const_spec(cross.shape),
            _const_spec(tail.shape),
            _const_spec(wout.shape),
            _const_spec((1, D_MODEL)),
            _const_spec(wg.shape),
            _const_spec(wu.shape),
            _const_spec(wd.shape),
            _const_spec((1, D_MODEL)),
        ],
        out_specs=[
            pl.BlockSpec((tm, D_MODEL), dense_row),
            pl.BlockSpec((tm, D_MODEL), lambda j: (0, 0)),
            pl.BlockSpec((1, N_HEADS, D_HEAD, 2 * D_HEAD), stream4),
            pl.BlockSpec((1, N_HEADS, LANES), stream3),
            pl.BlockSpec((1, N_HEADS, D_HEAD, D_HEAD), stream4),
        ],
        out_shape=[
            jax.ShapeDtypeStruct((n_tok, D_MODEL), F32),
            jax.ShapeDtypeStruct((tm, D_MODEL), F32),
            jax.ShapeDtypeStruct((bsz, N_HEADS, D_HEAD, 2 * D_HEAD), F32),
            jax.ShapeDtypeStruct((bsz, N_HEADS, LANES), F32),
            jax.ShapeDtypeStruct((bsz, N_HEADS, D_HEAD, D_HEAD), F32),
        ],
        scratch_shapes=[
            pltpu.VMEM((tm, D_MODEL), BF16),
            pltpu.VMEM((tm, D_MODEL), F32),
            pltpu.VMEM((tm, D_MODEL), F32),
            pltpu.VMEM((N_HEADS, D_HEAD, 2 * D_HEAD), F32),
            pltpu.VMEM((8, LANES), F32),
            pltpu.VMEM((N_HEADS, D_HEAD, D_HEAD), F32),
        ],
        compiler_params=pltpu.CompilerParams(
            dimension_semantics=("arbitrary",), vmem_limit_bytes=VMEM_LIMIT),
        name="mix_out",
    )(qkv, gc, ut, x1, x1_short, y_short, gnm, gnr, decay, cross, tail, wout, g2, wg, wu, wd, gfin)


def _rope_tables(pos):
    half = D_HEAD // 2
    inv = ROPE_BASE ** (-jnp.arange(half, dtype=F32) / half)
    ang = pos.astype(F32)[:, None] * inv[None, :]
    cos, sin = jnp.cos(ang), jnp.sin(ang)
    return jnp.concatenate([cos, cos], axis=-1), jnp.concatenate([-sin, sin], axis=-1)


def _pack_state(c, n, m):
    s = jnp.concatenate([c, jnp.broadcast_to(n[..., None], c.shape)], axis=-1)
    return s, jnp.broadcast_to(m[..., None], m.shape + (LANES,))


def _unpack_state(s, m, r):
    return s[..., :D_HEAD], s[..., D_HEAD], m[..., 0], r


def kernel(x_prompt, x_sample, state_mlstm_C, state_mlstm_n, state_mlstm_m, state_ret_R,
           g_ffn1, w_ffn1_gate, w_ffn1_up, w_ffn1_down, g_mix, w_in, b_gates, g_gn_mlstm, g_gn_ret,
           w_out, g_ffn2, w_ffn2_gate, w_ffn2_up, w_ffn2_down, g_final):
    depth = g_ffn1.shape[0]
    bp, sp, _ = x_prompt.shape
    bs, ss, _ = x_sample.shape
    assert depth == 1, "the final norm is fused into the layer's last kernel"
    assert ss < CHUNK and LANES % ss == 0 and bs * ss == TOKEN_TILE and sp % TOKEN_TILE == 0
    pos_p = _rope_tables(jnp.arange(sp))
    cos_s, sin_s = _rope_tables(PAST_LEN + jnp.arange(ss))
    pos_s = (jnp.tile(cos_s, (bs, 1)), jnp.tile(sin_s, (bs, 1)))

    new_p, new_s = [], []
    for l in range(depth):
        cast = lambda w: w.astype(BF16)
        w_in_t = w_in[l].T
        wi_t = w_in_t[N_QKV:N_QKV + N_HEADS]
        wf_t = w_in_t[N_QKV + N_HEADS:]
        wgate = jnp.concatenate([wi_t, wi_t, wf_t, wf_t], axis=0)
        rows8 = lambda b: jnp.broadcast_to(jnp.tile(b.astype(F32), 2)[:, None], (8, LANES))
        in_w = (g_ffn1[l][None, :], cast(w_ffn1_gate[l]), cast(w_ffn1_up[l]), cast(w_ffn1_down[l]),
                g_mix[l][None, :], cast(w_in_t), wgate,
                rows8(b_gates[l][:N_HEADS]), rows8(b_gates[l][N_HEADS:]))
        gnm, gnr = g_gn_mlstm[l][None, :].astype(F32), g_gn_ret[l][None, :].astype(F32)
        out_w = (cast(w_out[l]), g_ffn2[l][None, :], cast(w_ffn2_gate[l]), cast(w_ffn2_up[l]),
                 cast(w_ffn2_down[l]), g_final[None, :])

        x1s, qkv_s, gc_s, ut_s = _ffn_in(x_sample.reshape(bs * ss, D_MODEL), *in_w, *pos_s, ss)
        s0, m0 = _pack_state(state_mlstm_C[l].astype(F32), state_mlstm_n[l].astype(F32),
                             state_mlstm_m[l].astype(F32))
        ys, s_s, m_s, r_s = _mixer(qkv_s, gc_s, ut_s, s0, m0, state_ret_R[l].astype(F32), gnm, gnr, ss)
        x1p, qkv_p, gc_p, ut_p = _ffn_in(x_prompt.reshape(bp * sp, D_MODEL), *in_w, *pos_p, CHUNK)
        out_p, out_s, s_p, m_p, r_p = _mix_out(qkv_p, gc_p, ut_p, x1p, x1s, ys, bp, gnm, gnr, *out_w)
        new_p.append(_unpack_state(s_p, m_p, r_p))
        new_s.append(_unpack_state(s_s, m_s, r_s))
    hp, hs = out_p.reshape(bp, sp, D_MODEL), out_s.reshape(bs, ss, D_MODEL)
    stack = lambda sts, i: jnp.stack([s[i] for s in sts])
    return (hp, hs,
            stack(new_p, 0), stack(new_p, 1), stack(new_p, 2), stack(new_p, 3),
            stack(new_s, 0), stack(new_s, 1), stack(new_s, 2), stack(new_s, 3))
```

```python
import functools

import numpy as np
import jax
import jax.numpy as jnp
from jax import lax
from jax.experimental import pallas as pl
from jax.experimental.pallas import tpu as pltpu

F32 = jnp.float32
BF16 = jnp.bfloat16

D_MODEL = 1024
N_HEADS = 4
D_HEAD = 128
D_GROUP = N_HEADS * D_HEAD
D_FF = 2816
PAST_LEN = 4096
ROPE_BASE = 10000.0
EPS = 1e-6
GN_EPS = 1e-5
LOG_GAMMA = np.log(1.0 - 2.0 ** (-5.0 - np.arange(N_HEADS))).astype(np.float32)

LANES = 128
MXU_DIM = 256
FF_CHUNK = MXU_DIM
N_FF_CHUNKS = D_FF // FF_CHUNK
CHUNK = 128
TOKEN_TILE = 512
N_QKV = 8 * D_GROUP
GATE_ROWS = 16
SCAN_COLS = 12
VMEM_LIMIT = 58 * 1024 * 1024

_NT = (((1,), (1,)), ((), ()))
_TN = (((0,), (0,)), ((), ()))


def _rms_norm(x, g):
    ms = jnp.mean(x * x, axis=-1, keepdims=True)
    return x * lax.rsqrt(ms + EPS) * g


def _swiglu_chunk(acc_ref, xn, wg_ref, wu_ref, wd_ref, c):
    cols = slice(c * FF_CHUNK, (c + 1) * FF_CHUNK)
    g = jnp.dot(xn, wg_ref[:, cols], preferred_element_type=F32)
    u = jnp.dot(xn, wu_ref[:, cols], preferred_element_type=F32)
    a = (g * jax.nn.sigmoid(g) * u).astype(BF16)
    d = jnp.dot(a, wd_ref[cols, :], preferred_element_type=F32)
    if c == 0:
        acc_ref[...] = d
    else:
        acc_ref[...] += d


def _swiglu_into(acc_ref, xn, wg_ref, wu_ref, wd_ref):
    for c in range(N_FF_CHUNKS):
        _swiglu_chunk(acc_ref, xn, wg_ref, wu_ref, wd_ref, c)


def _rope_heads(p, cos2, sin2):
    outs = []
    for h in range(N_HEADS):
        xh = p[:, h * D_HEAD:(h + 1) * D_HEAD]
        outs.append(xh * cos2 + pltpu.roll(xh, D_HEAD // 2, axis=1) * sin2)
    return jnp.concatenate(outs, axis=1)


def _log_sigmoid(x):
    return jnp.minimum(x, 0.0) - jnp.log1p(jnp.exp(-jnp.abs(x)))


def _scan_lanes(x, op, fill, pos, seg_len):
    d = 1
    while d < seg_len:
        x = op(x, jnp.where(pos >= d, pltpu.roll(x, d, axis=1), fill))
        d *= 2
    return x


def _gate_scans(gt, bi, bf, seg_len, gc_ref, ut_ref):
    pad_rows = jnp.zeros((LANES - 16, LANES), F32)
    lane = lax.broadcasted_iota(jnp.int32, (8, LANES), 1)
    sub = lax.broadcasted_iota(jnp.int32, (8, LANES), 0)
    pos = lane & (seg_len - 1)
    for c in range(gt.shape[1] // LANES):
        lanes = slice(c * LANES, (c + 1) * LANES)
        ig = gt[0:8, lanes] + bi
        lf = _log_sigmoid(gt[8:16, lanes] + bf)
        b = _scan_lanes(lf, jnp.add, 0.0, pos, seg_len)
        u = ig - b
        cm = _scan_lanes(u, jnp.maximum, -jnp.inf, pos, seg_len)
        packed = jnp.concatenate([jnp.where(sub < N_HEADS, cm, b), jnp.where(sub < N_HEADS, u, 0.0)], axis=0)
        gc_ref[c] = jnp.concatenate([packed, pad_rows], axis=0).T
        ut_ref[c] = u


def _ffn_in_kernel(x_ref, g1_ref, wg_ref, wu_ref, wd_ref, gmix_ref, win_ref, wgate_ref, bi_ref, bf_ref,
                   cos_ref, sin_ref, x1_ref, qkv_ref, gc_ref, ut_ref, acc_ref, *, seg_len):
    x = x_ref[...]
    xn = _rms_norm(x, g1_ref[...]).astype(BF16)
    _swiglu_into(acc_ref, xn, wg_ref, wu_ref, wd_ref)
    x1 = x + 0.5 * acc_ref[...]
    x1_ref[...] = x1
    hn = _rms_norm(x1, gmix_ref[...]).astype(BF16)
    gt = lax.dot_general(wgate_ref[...].astype(BF16), hn, _NT, preferred_element_type=F32)
    _gate_scans(gt, bi_ref[...], bf_ref[...], seg_len, gc_ref, ut_ref)
    cos2 = cos_ref[...]
    sin2 = sin_ref[...]
    key_scale = D_HEAD ** -0.5
    for j in range(8):
        cols = slice(j * D_GROUP, (j + 1) * D_GROUP)
        p = lax.dot_general(hn, win_ref[cols, :], _NT, preferred_element_type=F32)
        if j in (4, 5):
            p = _rope_heads(p, cos2, sin2)
        if j in (1, 5):
            p = p * key_scale
        qkv_ref[:, cols] = p.astype(BF16)


def _const_spec(shape):
    nd = len(shape)
    return pl.BlockSpec(shape, lambda *_: (0,) * nd, pipeline_mode=pl.Buffered(1))


def _ffn_in(x, g1, wg, wu, wd, gmix, win, wgate, bi, bf, cos2, sin2, seg_len):
    n_tok = x.shape[0]
    tm = TOKEN_TILE
    pos_tiles = cos2.shape[0] // tm
    row = lambda i: (i, 0)
    blk = lambda i: (i, 0, 0)
    pos = lambda i: (i % pos_tiles, 0)
    return pl.pallas_call(
        functools.partial(_ffn_in_kernel, seg_len=seg_len),
        grid=(n_tok // tm,),
        in_specs=[
            pl.BlockSpec((tm, D_MODEL), row),
            _const_spec((1, D_MODEL)),
            _const_spec(wg.shape),
            _const_spec(wu.shape),
            _const_spec(wd.shape),
            _const_spec((1, D_MODEL)),
            _const_spec(win.shape),
            _const_spec(wgate.shape),
            _const_spec(bi.shape),
            _const_spec(bf.shape),
            pl.BlockSpec((tm, LANES), pos),
            pl.BlockSpec((tm, LANES), pos),
        ],
        out_specs=[
            pl.BlockSpec((tm, D_MODEL), row),
            pl.BlockSpec((tm, N_QKV), row),
            pl.BlockSpec((tm // LANES, LANES, LANES), blk),
            pl.BlockSpec((tm // LANES, 8, LANES), blk),
        ],
        out_shape=[
            jax.ShapeDtypeStruct((n_tok, D_MODEL), F32),
            jax.ShapeDtypeStruct((n_tok, N_QKV), BF16),
            jax.ShapeDtypeStruct((n_tok // LANES, LANES, LANES), F32),
            jax.ShapeDtypeStruct((n_tok // LANES, 8, LANES), F32),
        ],
        scratch_shapes=[pltpu.VMEM((tm, D_MODEL), F32)],
        compiler_params=pltpu.CompilerParams(
            dimension_semantics=("arbitrary",), vmem_limit_bytes=VMEM_LIMIT),
        name="ffn_in",
    )(x, g1, wg, wu, wd, gmix, win, wgate, bi, bf, cos2, sin2)


def _group_norm(h, g):
    mu = jnp.mean(h, axis=-1, keepdims=True)
    d = h - mu
    var = jnp.mean(d * d, axis=-1, keepdims=True)
    return d * lax.rsqrt(var + GN_EPS) * g


def _chunk_masks(frames):
    row_idx = lax.broadcasted_iota(jnp.int32, (frames, frames), 0)
    col_idx = lax.broadcasted_iota(jnp.int32, (frames, frames), 1)
    return row_idx >= col_idx, jnp.ones((frames, D_HEAD), BF16)


def _mixer_chunk(rows, qkv_ref, gcc, utc, s_ref, m_ref, r_ref, y_ref, consts, masks, ret_total, after_scores=None):
    gnm_ref, gnr_ref, decay_ref, cross_ref, tail_ref = consts
    causal, ones_blk = masks
    L = causal.shape[0]

    def scan_col(q, h):
        j = q * N_HEADS + h
        return jnp.broadcast_to(gcc[:, j:j + 1], (L, LANES))

    def head_slices(group, h):
        base = group * 4 * D_GROUP + h * D_HEAD
        return [qkv_ref[rows, base + j * D_GROUP:base + j * D_GROUP + D_HEAD] for j in range(4)]

    def scores(group):
        if L < LANES:
            return [lax.dot_general(*head_slices(group, h)[:2], _NT, preferred_element_type=F32)
                    for h in range(N_HEADS)]
        out = []
        zero_blk = jnp.zeros((L, D_HEAD), BF16)
        for h in range(0, N_HEADS, 2):
            base = group * 4 * D_GROUP + h * D_HEAD
            q2 = qkv_ref[rows, base:base + 2 * D_HEAD]
            k_a, k_b = head_slices(group, h)[1], head_slices(group, h + 1)[1]
            k_bd = jnp.concatenate([jnp.concatenate([k_a, zero_blk], axis=1),
                                    jnp.concatenate([zero_blk, k_b], axis=1)], axis=0)
            qk2 = lax.dot_general(q2, k_bd, _NT, preferred_element_type=F32)
            out += [qk2[:, :L], qk2[:, L:]]
        return out

    qk_m, qk_r = scores(0), scores(1)
    if after_scores is not None:
        after_scores()

    lhs_m, kw_m, gate_m, lhs_r, kt_r = [], [], [], [], []
    for h in range(N_HEADS):
        cm_b, b_b, u_b = scan_col(0, h), scan_col(1, h), scan_col(2, h)
        m_prev = m_ref[h:h + 1, :]
        big_m = jnp.maximum(cm_b, m_prev)
        a_b = jnp.exp(m_prev - big_m)
        m_t = b_b + big_m
        m_last = big_m[L - 1:L, :]
        m_ref[h:h + 1, :] = m_t[L - 1:L, :]
        w = jnp.where(causal, jnp.exp(utc[h:h + 1, :] - big_m[:, :L]), 0.0)
        q, k, _, _ = head_slices(0, h)
        lhs_m.append(jnp.concatenate([(q.astype(F32) * a_b).astype(BF16), (qk_m[h] * w).astype(BF16)], axis=1))
        kw_m.append((k.astype(F32) * jnp.exp(u_b - m_last)).astype(BF16))
        gate_m.append((a_b[L - 1:L, :], m_t))
    for h in range(N_HEADS):
        q, k, _, _ = head_slices(1, h)
        lhs_r.append(jnp.concatenate([(q.astype(F32) * cross_ref[h]).astype(BF16),
                                      (qk_r[h] * decay_ref[h]).astype(BF16)], axis=1))
        kt_r.append((k.astype(F32) * tail_ref[h]).astype(BF16))

    res_m, upd_m, res_r, upd_r = [], [], [], []
    for h in range(N_HEADS):
        v = head_slices(0, h)[2]
        v_aug = jnp.concatenate([v, ones_blk], axis=1)
        rhs = jnp.concatenate([s_ref[h].astype(BF16), v_aug], axis=0)
        res_m.append(jnp.dot(lhs_m[h], rhs, preferred_element_type=F32))
        upd_m.append(lax.dot_general(kw_m[h], v_aug, _TN, preferred_element_type=F32))
    for h in range(N_HEADS):
        v = head_slices(1, h)[2]
        rhs = jnp.concatenate([r_ref[h].astype(BF16), v], axis=0)
        res_r.append(jnp.dot(lhs_r[h], rhs, preferred_element_type=F32))
        upd_r.append(lax.dot_general(kt_r[h], v, _TN, preferred_element_type=F32))

    for h in range(N_HEADS):
        hc = slice(h * D_HEAD, (h + 1) * D_HEAD)
        a_last, m_t = gate_m[h]
        num = res_m[h][:, :D_HEAD]
        den = res_m[h][:, D_HEAD:]
        hh = num / jnp.maximum(jnp.abs(den), jnp.exp(-m_t))
        s_ref[h] = jnp.concatenate([a_last, a_last], axis=1) * s_ref[h] + upd_m[h]
        o = head_slices(0, h)[3]
        y = jax.nn.sigmoid(o.astype(F32)) * _group_norm(hh, gnm_ref[:, hc])
        y_ref[rows, hc] = y.astype(BF16)
    for h in range(N_HEADS):
        hc = slice(h * D_HEAD, (h + 1) * D_HEAD)
        r_ref[h] = ret_total[h] * r_ref[h] + upd_r[h]
        gf = head_slices(1, h)[3].astype(F32)
        y = gf * jax.nn.sigmoid(gf) * _group_norm(res_r[h], gnr_ref[:, hc])
        y_ref[rows, D_GROUP + h * D_HEAD:D_GROUP + (h + 1) * D_HEAD] = y.astype(BF16)


def _load_state(s_scr, c_ref, n_ref):
    s_scr[:, :, 0:D_HEAD] = c_ref[...]
    n_cols = jnp.concatenate([n_ref[...], jnp.zeros((LANES - 8, D_HEAD), F32)], axis=0).T
    for h in range(N_HEADS):
        s_scr[h, :, D_HEAD:2 * D_HEAD] = jnp.broadcast_to(n_cols[:, h:h + 1], (D_HEAD, LANES))


def _store_state(s_scr, c_ref, n_ref):
    c_ref[...] = s_scr[:, :, 0:D_HEAD]
    for h in range(N_HEADS):
        n_ref[h:h + 1, :] = s_scr[h, :, D_HEAD:2 * D_HEAD].T[0:1, :]


def _mixer_kernel(qkv_ref, gc_ref, ut_ref, c0_ref, n0_ref, m0_ref, r0_ref, gnm_ref, gnr_ref,
                  decay_ref, cross_ref, tail_ref,
                  y_ref, c_ref, n_ref, m_ref, r_ref, s_scr, *, frames, ret_total):
    m_ref[...] = m0_ref[...]
    r_ref[...] = r0_ref[...]
    consts = (gnm_ref, gnr_ref, decay_ref, cross_ref, tail_ref)
    masks = _chunk_masks(frames)
    for i in range(LANES // frames):
        seg = slice(i * frames, (i + 1) * frames)
        _load_state(s_scr, c0_ref.at[i], n0_ref.at[i])
        _mixer_chunk(seg, qkv_ref, gc_ref[0][seg, :], ut_ref[0][:, seg],
                     s_scr, m_ref.at[i], r_ref.at[i], y_ref, consts, masks, ret_total)
        _store_state(s_scr, c_ref.at[i], n_ref.at[i])


def _mix_out_kernel(qkv_ref, gc_ref, ut_ref, x1_ref, x1s_ref, ys_ref,
                    gnm_ref, gnr_ref, decay_ref, cross_ref, tail_ref,
                    wout_ref, g2_ref, wg_ref, wu_ref, wd_ref, gfin_ref,
                    out_ref, outs_ref, c_out, n_out, m_out, r_out,
                    y_scr, acc_ref, x2_ref, s_scr, m_scr, r_scr, *, tiles_per_stream, n_tiles, ret_total):
    j = pl.program_id(0)
    n_chunks = TOKEN_TILE // CHUNK

    @pl.when(j == 0)
    def _():
        y_scr[...] = ys_ref[...]

    @pl.when(j % tiles_per_stream == 0)
    def _():
        s_scr[...] = jnp.zeros_like(s_scr)
        m_scr[...] = jnp.zeros_like(m_scr)
        r_scr[...] = jnp.zeros_like(r_scr)

    consts = (gnm_ref, gnr_ref, decay_ref, cross_ref, tail_ref)
    masks = _chunk_masks(CHUNK)

    x1 = jnp.where(j == 0, x1s_ref[...], x1_ref[...])
    x2 = x1 + jnp.dot(y_scr[...], wout_ref[...], preferred_element_type=F32)
    x2_ref[...] = x2
    xn = _rms_norm(x2, g2_ref[...]).astype(BF16)

    pending = list(range(N_FF_CHUNKS))

    def ffn(k):
        for _ in range(min(k, len(pending))):
            _swiglu_chunk(acc_ref, xn, wg_ref, wu_ref, wd_ref, pending.pop(0))

    for c in range(n_chunks):
        _mixer_chunk(slice(c * CHUNK, (c + 1) * CHUNK), qkv_ref, gc_ref[c], ut_ref[c],
                     s_scr, m_scr, r_scr, y_scr, consts, masks, ret_total,
                     after_scores=functools.partial(ffn, 1))
        ffn(1)
    ffn(N_FF_CHUNKS)

    x3 = x2_ref[...] + 0.5 * acc_ref[...]
    out_ref[...] = _rms_norm(x3, gfin_ref[...])

    @pl.when(j == 0)
    def _():
        outs_ref[...] = out_ref[...]

    @pl.when((j % tiles_per_stream == tiles_per_stream - 1) & (j < n_tiles))
    def _():
        _store_state(s_scr, c_out.at[0], n_out.at[0])
        m_out[0] = m_scr[0:N_HEADS, :]
        r_out[0] = r_scr[...]


def _mixer_consts(frames):
    idx = np.arange(frames, dtype=np.float32)
    diff = idx[:, None] - idx[None, :]
    lg = LOG_GAMMA
    decay = np.where(diff >= 0, np.exp(lg[:, None, None] * np.maximum(diff, 0.0)), 0.0).astype(np.float32)
    rep = lambda col: np.ascontiguousarray(np.broadcast_to(col.T[:, :, None], (N_HEADS, frames, LANES))).astype(np.float32)
    cross = rep(np.exp(lg[None, :] * (idx[:, None] + 1.0)))
    tail = rep(np.exp(lg[None, :] * (frames - 1.0 - idx[:, None])))
    ret_total = tuple(float(np.exp(np.float32(lg_h) * np.float32(frames))) for lg_h in lg)
    return (jnp.asarray(decay), jnp.asarray(cross), jnp.asarray(tail)), ret_total


def _mixer(qkv, gc, ut, c0, n0, m0, r0, gnm, gnr, frames):
    n_tok = qkv.shape[0]
    per_blk = LANES // frames
    (decay, cross, tail), ret_total = _mixer_consts(frames)
    row = lambda i: (i, 0)
    blk3 = lambda i: (i, 0, 0)
    blk4 = lambda i: (i, 0, 0, 0)
    return pl.pallas_call(
        functools.partial(_mixer_kernel, frames=frames, ret_total=ret_total),
        grid=(n_tok // LANES,),
        in_specs=[
            pl.BlockSpec((LANES, N_QKV), row),
            pl.BlockSpec((1, LANES, LANES), blk3),
            pl.BlockSpec((1, 8, LANES), blk3),
            pl.BlockSpec((per_blk, N_HEADS, D_HEAD, D_HEAD), blk4),
            pl.BlockSpec((per_blk, 8, D_HEAD), blk3),
            pl.BlockSpec((per_blk, N_HEADS, LANES), blk3),
            pl.BlockSpec((per_blk, N_HEADS, D_HEAD, D_HEAD), blk4),
            _const_spec((1, D_GROUP)),
            _const_spec((1, D_GROUP)),
            _const_spec(decay.shape),
            _const_spec(cross.shape),
            _const_spec(tail.shape),
        ],
        out_specs=[
            pl.BlockSpec((LANES, D_MODEL), row),
            pl.BlockSpec((per_blk, N_HEADS, D_HEAD, D_HEAD), blk4),
            pl.BlockSpec((per_blk, N_HEADS, D_HEAD), blk3),
            pl.BlockSpec((per_blk, N_HEADS, LANES), blk3),
            pl.BlockSpec((per_blk, N_HEADS, D_HEAD, D_HEAD), blk4),
        ],
        out_shape=[
            jax.ShapeDtypeStruct((n_tok, D_MODEL), BF16),
            jax.ShapeDtypeStruct(c0.shape, F32),
            jax.ShapeDtypeStruct((n0.shape[0], N_HEADS, D_HEAD), F32),
            jax.ShapeDtypeStruct(m0.shape, F32),
            jax.ShapeDtypeStruct(r0.shape, F32),
        ],
        scratch_shapes=[pltpu.VMEM((N_HEADS, D_HEAD, 2 * D_HEAD), F32)],
        compiler_params=pltpu.CompilerParams(
            dimension_semantics=("arbitrary",), vmem_limit_bytes=VMEM_LIMIT),
        name="mixer",
    )(qkv, gc, ut, c0, n0, m0, r0, gnm, gnr, decay, cross, tail)


def _mix_out(qkv, gc, ut, x1, x1_short, y_short, bsz, gnm, gnr, wout, g2, wg, wu, wd, gfin):
    n_tok = x1.shape[0]
    tm = TOKEN_TILE
    assert x1_short.shape == (tm, D_MODEL) and y_short.shape == (tm, D_MODEL)
    n_tiles = n_tok // tm
    tiles_per_stream = n_tiles // bsz
    n_chunks = tm // CHUNK
    (decay, cross, tail), ret_total = _mixer_consts(CHUNK)

    mix_row = lambda j: (jnp.minimum(j, n_tiles - 1), 0)
    mix_blk = lambda j: (jnp.minimum(j, n_tiles - 1), 0, 0)
    dense_row = lambda j: (jnp.maximum(j - 1, 0), 0)
    stream4 = lambda j: (jnp.minimum(j, n_tiles - 1) // tiles_per_stream, 0, 0, 0)
    stream3 = lambda j: (jnp.minimum(j, n_tiles - 1) // tiles_per_stream, 0, 0)
    kern = functools.partial(_mix_out_kernel, tiles_per_stream=tiles_per_stream, n_tiles=n_tiles,
                             ret_total=ret_total)
    return pl.pallas_call(
        kern,
        grid=(n_tiles + 1,),
        in_specs=[
            pl.BlockSpec((tm, N_QKV), mix_row),
            pl.BlockSpec((n_chunks, LANES, LANES), mix_blk),
            pl.BlockSpec((n_chunks, 8, LANES), mix_blk),
            pl.BlockSpec((tm, D_MODEL), dense_row),
            _const_spec((tm, D_MODEL)),
            _const_spec((tm, D_MODEL)),
            _const_spec((1, D_GROUP)),
            _const_spec((1, D_GROUP)),
            _const_spec(decay.shape),
            _const_spec(cross.shape),
            _const_spec(tail.shape),
            _const_spec(wout.shape),
            _const_spec((1, D_MODEL)),
            _const_spec(wg.shape),
            _const_spec(wu.shape),
            _const_spec(wd.shape),
            _const_spec((1, D_MODEL)),
        ],
        out_specs=[
            pl.BlockSpec((tm, D_MODEL), dense_row),
            pl.BlockSpec((tm, D_MODEL), lambda j: (0, 0)),
            pl.BlockSpec((1, N_HEADS, D_HEAD, D_HEAD), stream4),
            pl.BlockSpec((1, N_HEADS, D_HEAD), stream3),
            pl.BlockSpec((1, N_HEADS, LANES), stream3),
            pl.BlockSpec((1, N_HEADS, D_HEAD, D_HEAD), stream4),
        ],
        out_shape=[
            jax.ShapeDtypeStruct((n_tok, D_MODEL), F32),
            jax.ShapeDtypeStruct((tm, D_MODEL), F32),
            jax.ShapeDtypeStruct((bsz, N_HEADS, D_HEAD, D_HEAD), F32),
            jax.ShapeDtypeStruct((bsz, N_HEADS, D_HEAD), F32),
            jax.ShapeDtypeStruct((bsz, N_HEADS, LANES), F32),
            jax.ShapeDtypeStruct((bsz, N_HEADS, D_HEAD, D_HEAD), F32),
        ],
        scratch_shapes=[
            pltpu.VMEM((tm, D_MODEL), BF16),
            pltpu.VMEM((tm, D_MODEL), F32),
            pltpu.VMEM((tm, D_MODEL), F32),
            pltpu.VMEM((N_HEADS, D_HEAD, 2 * D_HEAD), F32),
            pltpu.VMEM((8, LANES), F32),
            pltpu.VMEM((N_HEADS, D_HEAD, D_HEAD), F32),
        ],
        compiler_params=pltpu.CompilerParams(
            dimension_semantics=("arbitrary",), vmem_limit_bytes=VMEM_LIMIT),
        name="mix_out",
    )(qkv, gc, ut, x1, x1_short, y_short, gnm, gnr, decay, cross, tail, wout, g2, wg, wu, wd, gfin)


def _rope_tables(pos):
    half = D_HEAD // 2
    inv = ROPE_BASE ** (-jnp.arange(half, dtype=F32) / half)
    ang = pos.astype(F32)[:, None] * inv[None, :]
    cos, sin = jnp.cos(ang), jnp.sin(ang)
    return jnp.concatenate([cos, cos], axis=-1), jnp.concatenate([-sin, sin], axis=-1)


def kernel(x_prompt, x_sample, state_mlstm_C, state_mlstm_n, state_mlstm_m, state_ret_R,
           g_ffn1, w_ffn1_gate, w_ffn1_up, w_ffn1_down, g_mix, w_in, b_gates, g_gn_mlstm, g_gn_ret,
           w_out, g_ffn2, w_ffn2_gate, w_ffn2_up, w_ffn2_down, g_final):
    depth = g_ffn1.shape[0]
    bp, sp, _ = x_prompt.shape
    bs, ss, _ = x_sample.shape
    assert depth == 1, "the final norm is fused into the layer's last kernel"
    assert ss < CHUNK and LANES % ss == 0 and bs * ss == TOKEN_TILE and sp % TOKEN_TILE == 0
    pos_p = _rope_tables(jnp.arange(sp))
    cos_s, sin_s = _rope_tables(PAST_LEN + jnp.arange(ss))
    pos_s = (jnp.tile(cos_s, (bs, 1)), jnp.tile(sin_s, (bs, 1)))

    new_p, new_s = [], []
    for l in range(depth):
        cast = lambda w: w.astype(BF16)
        w_in_t = w_in[l].T
        wi_t = w_in_t[N_QKV:N_QKV + N_HEADS]
        wf_t = w_in_t[N_QKV + N_HEADS:]
        wgate = jnp.concatenate([wi_t, wi_t, wf_t, wf_t], axis=0)
        rows8 = lambda b: jnp.broadcast_to(jnp.tile(b.astype(F32), 2)[:, None], (8, LANES))
        in_w = (g_ffn1[l][None, :], cast(w_ffn1_gate[l]), cast(w_ffn1_up[l]), cast(w_ffn1_down[l]),
                g_mix[l][None, :], cast(w_in_t), wgate,
                rows8(b_gates[l][:N_HEADS]), rows8(b_gates[l][N_HEADS:]))
        gnm, gnr = g_gn_mlstm[l][None, :].astype(F32), g_gn_ret[l][None, :].astype(F32)
        out_w = (cast(w_out[l]), g_ffn2[l][None, :], cast(w_ffn2_gate[l]), cast(w_ffn2_up[l]),
                 cast(w_ffn2_down[l]), g_final[None, :])

        x1s, qkv_s, gc_s, ut_s = _ffn_in(x_sample.reshape(bs * ss, D_MODEL), *in_w, *pos_s, ss)
        m0 = jnp.broadcast_to(state_mlstm_m[l].astype(F32)[..., None], (bs, N_HEADS, LANES))
        n0 = jnp.pad(state_mlstm_n[l].astype(F32), ((0, 0), (0, 8 - N_HEADS), (0, 0)))
        ys, c_s, n_s, m_s, r_s = _mixer(qkv_s, gc_s, ut_s, state_mlstm_C[l].astype(F32), n0, m0,
                                        state_ret_R[l].astype(F32), gnm, gnr, ss)
        x1p, qkv_p, gc_p, ut_p = _ffn_in(x_prompt.reshape(bp * sp, D_MODEL), *in_w, *pos_p, CHUNK)
        out_p, out_s, c_p, n_p, m_p, r_p = _mix_out(qkv_p, gc_p, ut_p, x1p, x1s, ys, bp, gnm, gnr, *out_w)
        new_p.append((c_p, n_p, m_p[..., 0], r_p))
        new_s.append((c_s, n_s, m_s[..., 0], r_s))
    hp, hs = out_p.reshape(bp, sp, D_MODEL), out_s.reshape(bs, ss, D_MODEL)
    stack = lambda sts, i: jnp.stack([s[i] for s in sts])
    return (hp, hs,
            stack(new_p, 0), stack(new_p, 1), stack(new_p, 2), stack(new_p, 3),
            stack(new_s, 0), stack(new_s, 1), stack(new_s, 2), stack(new_s, 3))
```

```python
import functools

import numpy as np
import jax
import jax.numpy as jnp
from jax import lax
from jax.experimental import pallas as pl
from jax.experimental.pallas import tpu as pltpu

F32 = jnp.float32
BF16 = jnp.bfloat16

D_MODEL = 1024
N_HEADS = 4
D_HEAD = 128
D_GROUP = N_HEADS * D_HEAD
D_FF = 2816
PAST_LEN = 4096
ROPE_BASE = 10000.0
EPS = 1e-6
GN_EPS = 1e-5
LOG_GAMMA = np.log(1.0 - 2.0 ** (-5.0 - np.arange(N_HEADS))).astype(np.float32)

LANES = 128
MXU_DIM = 256
FF_CHUNK = MXU_DIM
N_FF_CHUNKS = D_FF // FF_CHUNK
CHUNK = 128
TOKEN_TILE = 512
N_QKV = 8 * D_GROUP
GATE_ROWS = 16
SCAN_COLS = 12
VMEM_LIMIT = 58 * 1024 * 1024

_NT = (((1,), (1,)), ((), ()))
_TN = (((0,), (0,)), ((), ()))


def _rms_norm(x, g):
    ms = jnp.mean(x * x, axis=-1, keepdims=True)
    return x * lax.rsqrt(ms + EPS) * g


def _swiglu_chunk(acc_ref, xn, wg_ref, wu_ref, wd_ref, c):
    cols = slice(c * FF_CHUNK, (c + 1) * FF_CHUNK)
    g = jnp.dot(xn, wg_ref[:, cols], preferred_element_type=F32)
    u = jnp.dot(xn, wu_ref[:, cols], preferred_element_type=F32)
    a = (g * jax.nn.sigmoid(g) * u).astype(BF16)
    d = jnp.dot(a, wd_ref[cols, :], preferred_element_type=F32)
    if c == 0:
        acc_ref[...] = d
    else:
        acc_ref[...] += d


def _swiglu_into(acc_ref, xn, wg_ref, wu_ref, wd_ref):
    for c in range(N_FF_CHUNKS):
        _swiglu_chunk(acc_ref, xn, wg_ref, wu_ref, wd_ref, c)


def _rope_heads(p, cos2, sin2):
    outs = []
    for h in range(N_HEADS):
        xh = p[:, h * D_HEAD:(h + 1) * D_HEAD]
        outs.append(xh * cos2 + pltpu.roll(xh, D_HEAD // 2, axis=1) * sin2)
    return jnp.concatenate(outs, axis=1)


def _log_sigmoid(x):
    return jnp.minimum(x, 0.0) - jnp.log1p(jnp.exp(-jnp.abs(x)))


def _scan_lanes(x, op, fill, pos, seg_len):
    d = 1
    while d < seg_len:
        x = op(x, jnp.where(pos >= d, pltpu.roll(x, d, axis=1), fill))
        d *= 2
    return x


def _gate_scans(gt, bi, bf, seg_len, gc_ref, ut_ref):
    pad_rows = jnp.zeros((LANES - 16, LANES), F32)
    lane = lax.broadcasted_iota(jnp.int32, (8, LANES), 1)
    sub = lax.broadcasted_iota(jnp.int32, (8, LANES), 0)
    pos = lane & (seg_len - 1)
    for c in range(gt.shape[1] // LANES):
        lanes = slice(c * LANES, (c + 1) * LANES)
        ig = gt[0:8, lanes] + bi
        lf = _log_sigmoid(gt[8:16, lanes] + bf)
        b = _scan_lanes(lf, jnp.add, 0.0, pos, seg_len)
        u = ig - b
        cm = _scan_lanes(u, jnp.maximum, -jnp.inf, pos, seg_len)
        packed = jnp.concatenate([jnp.where(sub < N_HEADS, cm, b), jnp.where(sub < N_HEADS, u, 0.0)], axis=0)
        gc_ref[c] = jnp.concatenate([packed, pad_rows], axis=0).T
        ut_ref[c] = u


def _ffn_in_kernel(x_ref, g1_ref, wg_ref, wu_ref, wd_ref, gmix_ref, win_ref, wgate_ref, bi_ref, bf_ref,
                   cos_ref, sin_ref, x1_ref, qkv_ref, gc_ref, ut_ref, acc_ref, *, seg_len):
    x = x_ref[...]
    xn = _rms_norm(x, g1_ref[...]).astype(BF16)
    _swiglu_into(acc_ref, xn, wg_ref, wu_ref, wd_ref)
    x1 = x + 0.5 * acc_ref[...]
    x1_ref[...] = x1
    hn = _rms_norm(x1, gmix_ref[...]).astype(BF16)
    gt = lax.dot_general(wgate_ref[...].astype(BF16), hn, _NT, preferred_element_type=F32)
    _gate_scans(gt, bi_ref[...], bf_ref[...], seg_len, gc_ref, ut_ref)
    cos2 = cos_ref[...]
    sin2 = sin_ref[...]
    key_scale = D_HEAD ** -0.5
    for j in range(8):
        cols = slice(j * D_GROUP, (j + 1) * D_GROUP)
        p = lax.dot_general(hn, win_ref[cols, :], _NT, preferred_element_type=F32)
        if j in (4, 5):
            p = _rope_heads(p, cos2, sin2)
        if j in (1, 5):
            p = p * key_scale
        qkv_ref[:, cols] = p.astype(BF16)


def _const_spec(shape):
    nd = len(shape)
    return pl.BlockSpec(shape, lambda *_: (0,) * nd, pipeline_mode=pl.Buffered(1))


def _ffn_in(x, g1, wg, wu, wd, gmix, win, wgate, bi, bf, cos2, sin2, seg_len):
    n_tok = x.shape[0]
    tm = TOKEN_TILE
    pos_tiles = cos2.shape[0] // tm
    row = lambda i: (i, 0)
    blk = lambda i: (i, 0, 0)
    pos = lambda i: (i % pos_tiles, 0)
    return pl.pallas_call(
        functools.partial(_ffn_in_kernel, seg_len=seg_len),
        grid=(n_tok // tm,),
        in_specs=[
            pl.BlockSpec((tm, D_MODEL), row),
            _const_spec((1, D_MODEL)),
            _const_spec(wg.shape),
            _const_spec(wu.shape),
            _const_spec(wd.shape),
            _const_spec((1, D_MODEL)),
            _const_spec(win.shape),
            _const_spec(wgate.shape),
            _const_spec(bi.shape),
            _const_spec(bf.shape),
            pl.BlockSpec((tm, LANES), pos),
            pl.BlockSpec((tm, LANES), pos),
        ],
        out_specs=[
            pl.BlockSpec((tm, D_MODEL), row),
            pl.BlockSpec((tm, N_QKV), row),
            pl.BlockSpec((tm // LANES, LANES, LANES), blk),
            pl.BlockSpec((tm // LANES, 8, LANES), blk),
        ],
        out_shape=[
            jax.ShapeDtypeStruct((n_tok, D_MODEL), F32),
            jax.ShapeDtypeStruct((n_tok, N_QKV), BF16),
            jax.ShapeDtypeStruct((n_tok // LANES, LANES, LANES), F32),
            jax.ShapeDtypeStruct((n_tok // LANES, 8, LANES), F32),
        ],
        scratch_shapes=[pltpu.VMEM((tm, D_MODEL), F32)],
        compiler_params=pltpu.CompilerParams(
            dimension_semantics=("arbitrary",), vmem_limit_bytes=VMEM_LIMIT),
        name="ffn_in",
    )(x, g1, wg, wu, wd, gmix, win, wgate, bi, bf, cos2, sin2)


def _group_norm(h, g):
    mu = jnp.mean(h, axis=-1, keepdims=True)
    d = h - mu
    var = jnp.mean(d * d, axis=-1, keepdims=True)
    return d * lax.rsqrt(var + GN_EPS) * g


def _chunk_masks(frames):
    row_idx = lax.broadcasted_iota(jnp.int32, (frames, frames), 0)
    col_idx = lax.broadcasted_iota(jnp.int32, (frames, frames), 1)
    return row_idx >= col_idx, jnp.ones((frames, D_HEAD), BF16)


def _mixer_chunk(rows, qkv_ref, gcc, utc, s_ref, m_ref, r_ref, y_ref, consts, masks, ret_total, after_scores=None):
    gnm_ref, gnr_ref, decay_ref, cross_ref, tail_ref = consts
    causal, ones_blk = masks
    L = causal.shape[0]

    def scan_col(q, h):
        j = q * N_HEADS + h
        return jnp.broadcast_to(gcc[:, j:j + 1], (L, LANES))

    def head_slices(group, h):
        base = group * 4 * D_GROUP + h * D_HEAD
        return [qkv_ref[rows, base + j * D_GROUP:base + j * D_GROUP + D_HEAD] for j in range(4)]

    def scores(group):
        if L < LANES:
            return [lax.dot_general(*head_slices(group, h)[:2], _NT, preferred_element_type=F32)
                    for h in range(N_HEADS)]
        out = []
        zero_blk = jnp.zeros((L, D_HEAD), BF16)
        for h in range(0, N_HEADS, 2):
            base = group * 4 * D_GROUP + h * D_HEAD
            q2 = qkv_ref[rows, base:base + 2 * D_HEAD]
            k_a, k_b = head_slices(group, h)[1], head_slices(group, h + 1)[1]
            k_bd = jnp.concatenate([jnp.concatenate([k_a, zero_blk], axis=1),
                                    jnp.concatenate([zero_blk, k_b], axis=1)], axis=0)
            qk2 = lax.dot_general(q2, k_bd, _NT, preferred_element_type=F32)
            out += [qk2[:, :L], qk2[:, L:]]
        return out

    qk_m, qk_r = scores(0), scores(1)
    if after_scores is not None:
        after_scores()

    lhs_m, kw_m, gate_m, lhs_r, kt_r = [], [], [], [], []
    for h in range(N_HEADS):
        cm_b, b_b, u_b = scan_col(0, h), scan_col(1, h), scan_col(2, h)
        m_prev = m_ref[h:h + 1, :]
        big_m = jnp.maximum(cm_b, m_prev)
        a_b = jnp.exp(m_prev - big_m)
        m_t = b_b + big_m
        m_last = big_m[L - 1:L, :]
        m_ref[h:h + 1, :] = m_t[L - 1:L, :]
        w = jnp.where(causal, jnp.exp(utc[h:h + 1, :] - big_m[:, :L]), 0.0)
        q, k, _, _ = head_slices(0, h)
        lhs_m.append(jnp.concatenate([(q.astype(F32) * a_b).astype(BF16), (qk_m[h] * w).astype(BF16)], axis=1))
        kw_m.append((k.astype(F32) * jnp.exp(u_b - m_last)).astype(BF16))
        gate_m.append((a_b[L - 1:L, :], m_t))
    for h in range(N_HEADS):
        q, k, _, _ = head_slices(1, h)
        lhs_r.append(jnp.concatenate([(q.astype(F32) * cross_ref[h]).astype(BF16),
                                      (qk_r[h] * decay_ref[h]).astype(BF16)], axis=1))
        kt_r.append((k.astype(F32) * tail_ref[h]).astype(BF16))

    res_m, upd_m, res_r, upd_r = [], [], [], []
    for h in range(N_HEADS):
        v = head_slices(0, h)[2]
        v_aug = jnp.concatenate([v, ones_blk], axis=1)
        rhs = jnp.concatenate([s_ref[h].astype(BF16), v_aug], axis=0)
        res_m.append(jnp.dot(lhs_m[h], rhs, preferred_element_type=F32))
        upd_m.append(lax.dot_general(kw_m[h], v_aug, _TN, preferred_element_type=F32))
    for h in range(N_HEADS):
        v = head_slices(1, h)[2]
        rhs = jnp.concatenate([r_ref[h].astype(BF16), v], axis=0)
        res_r.append(jnp.dot(lhs_r[h], rhs, preferred_element_type=F32))
        upd_r.append(lax.dot_general(kt_r[h], v, _TN, preferred_element_type=F32))

    for h in range(N_HEADS):
        hc = slice(h * D_HEAD, (h + 1) * D_HEAD)
        a_last, m_t = gate_m[h]
        num = res_m[h][:, :D_HEAD]
        den = res_m[h][:, D_HEAD:]
        hh = num / jnp.maximum(jnp.abs(den), jnp.exp(-m_t))
        s_ref[h] = jnp.concatenate([a_last, a_last], axis=1) * s_ref[h] + upd_m[h]
        o = head_slices(0, h)[3]
        y = jax.nn.sigmoid(o.astype(F32)) * _group_norm(hh, gnm_ref[:, hc])
        y_ref[rows, hc] = y.astype(BF16)
    for h in range(N_HEADS):
        hc = slice(h * D_HEAD, (h + 1) * D_HEAD)
        r_ref[h] = ret_total[h] * r_ref[h] + upd_r[h]
        gf = head_slices(1, h)[3].astype(F32)
        y = gf * jax.nn.sigmoid(gf) * _group_norm(res_r[h], gnr_ref[:, hc])
        y_ref[rows, D_GROUP + h * D_HEAD:D_GROUP + (h + 1) * D_HEAD] = y.astype(BF16)


def _load_state(s_scr, c_ref, n_ref):
    s_scr[:, :, 0:D_HEAD] = c_ref[...]
    n_cols = jnp.concatenate([n_ref[...], jnp.zeros((LANES - 8, D_HEAD), F32)], axis=0).T
    for h in range(N_HEADS):
        s_scr[h, :, D_HEAD:2 * D_HEAD] = jnp.broadcast_to(n_cols[:, h:h + 1], (D_HEAD, LANES))


def _store_state(s_scr, c_ref, n_ref):
    c_ref[...] = s_scr[:, :, 0:D_HEAD]
    for h in range(N_HEADS):
        n_ref[h:h + 1, :] = s_scr[h, :, D_HEAD:2 * D_HEAD].T[0:1, :]


def _mixer_kernel(qkv_ref, gc_ref, ut_ref, c0_ref, n0_ref, m0_ref, r0_ref, gnm_ref, gnr_ref,
                  decay_ref, cross_ref, tail_ref,
                  y_ref, c_ref, n_ref, m_ref, r_ref, s_scr, *, frames, ret_total):
    m_ref[...] = m0_ref[...]
    r_ref[...] = r0_ref[...]
    consts = (gnm_ref, gnr_ref, decay_ref, cross_ref, tail_ref)
    masks = _chunk_masks(frames)
    for i in range(LANES // frames):
        seg = slice(i * frames, (i + 1) * frames)
        _load_state(s_scr, c0_ref.at[i], n0_ref.at[i])
        _mixer_chunk(seg, qkv_ref, gc_ref[0][seg, :], ut_ref[0][:, seg],
                     s_scr, m_ref.at[i], r_ref.at[i], y_ref, consts, masks, ret_total)
        _store_state(s_scr, c_ref.at[i], n_ref.at[i])


def _mix_out_kernel(qkv_ref, gc_ref, ut_ref, x1_ref, x1s_ref, ys_ref,
                    gnm_ref, gnr_ref, decay_ref, cross_ref, tail_ref,
                    wout_ref, g2_ref, wg_ref, wu_ref, wd_ref, gfin_ref,
                    out_ref, outs_ref, c_out, n_out, m_out, r_out,
                    y_scr, acc_ref, x2_ref, s_scr, m_scr, r_scr, *, tiles_per_stream, n_tiles, ret_total):
    j = pl.program_id(0)
    n_chunks = TOKEN_TILE // CHUNK

    @pl.when(j == 0)
    def _():
        y_scr[...] = ys_ref[...]

    @pl.when(j % tiles_per_stream == 0)
    def _():
        s_scr[...] = jnp.zeros_like(s_scr)
        m_scr[...] = jnp.zeros_like(m_scr)
        r_scr[...] = jnp.zeros_like(r_scr)

    consts = (gnm_ref, gnr_ref, decay_ref, cross_ref, tail_ref)
    masks = _chunk_masks(CHUNK)

    x1 = jnp.where(j == 0, x1s_ref[...], x1_ref[...])
    x2 = x1 + jnp.dot(y_scr[...], wout_ref[...], preferred_element_type=F32)
    x2_ref[...] = x2
    xn = _rms_norm(x2, g2_ref[...]).astype(BF16)

    pending = list(range(N_FF_CHUNKS))

    def ffn(k):
        for _ in range(min(k, len(pending))):
            _swiglu_chunk(acc_ref, xn, wg_ref, wu_ref, wd_ref, pending.pop(0))

    for c in range(n_chunks):
        _mixer_chunk(slice(c * CHUNK, (c + 1) * CHUNK), qkv_ref, gc_ref[c], ut_ref[c],
                     s_scr, m_scr, r_scr, y_scr, consts, masks, ret_total,
                     after_scores=functools.partial(ffn, 1))
        ffn(1)
    ffn(N_FF_CHUNKS)

    x3 = x2_ref[...] + 0.5 * acc_ref[...]
    out_ref[...] = _rms_norm(x3, gfin_ref[...])

    @pl.when(j == 0)
    def _():
        outs_ref[...] = out_ref[...]

    @pl.when((j % tiles_per_stream == tiles_per_stream - 1) & (j < n_tiles))
    def _():
        _store_state(s_scr, c_out.at[0], n_out.at[0])
        m_out[0] = m_scr[0:N_HEADS, :]
        r_out[0] = r_scr[...]


def _mixer_consts(frames):
    idx = np.arange(frames, dtype=np.float32)
    diff = idx[:, None] - idx[None, :]
    lg = LOG_GAMMA
    decay = np.where(diff >= 0, np.exp(lg[:, None, None] * np.maximum(diff, 0.0)), 0.0).astype(np.float32)
    rep = lambda col: np.ascontiguousarray(np.broadcast_to(col.T[:, :, None], (N_HEADS, frames, LANES))).astype(np.float32)
    cross = rep(np.exp(lg[None, :] * (idx[:, None] + 1.0)))
    tail = rep(np.exp(lg[None, :] * (frames - 1.0 - idx[:, None])))
    ret_total = tuple(float(np.exp(np.float32(lg_h) * np.float32(frames))) for lg_h in lg)
    return (jnp.asarray(decay), jnp.asarray(cross), jnp.asarray(tail)), ret_total


def _mixer(qkv, gc, ut, c0, n0, m0, r0, gnm, gnr, frames):
    n_tok = qkv.shape[0]
    per_blk = LANES // frames
    (decay, cross, tail), ret_total = _mixer_consts(frames)
    row = lambda i: (i, 0)
    blk3 = lambda i: (i, 0, 0)
    blk4 = lambda i: (i, 0, 0, 0)
    return pl.pallas_call(
        functools.partial(_mixer_kernel, frames=frames, ret_total=ret_total),
        grid=(n_tok // LANES,),
        in_specs=[
            pl.BlockSpec((LANES, N_QKV), row),
            pl.BlockSpec((1, LANES, LANES), blk3),
            pl.BlockSpec((1, 8, LANES), blk3),
            pl.BlockSpec((per_blk, N_HEADS, D_HEAD, D_HEAD), blk4),
            pl.BlockSpec((per_blk, 8, D_HEAD), blk3),
            pl.BlockSpec((per_blk, N_HEADS, LANES), blk3),
            pl.BlockSpec((per_blk, N_HEADS, D_HEAD, D_HEAD), blk4),
            _const_spec((1, D_GROUP)),
            _const_spec((1, D_GROUP)),
            _const_spec(decay.shape),
            _const_spec(cross.shape),
            _const_spec(tail.shape),
        ],
        out_specs=[
            pl.BlockSpec((LANES, D_MODEL), row),
            pl.BlockSpec((per_blk, N_HEADS, D_HEAD, D_HEAD), blk4),
            pl.BlockSpec((per_blk, N_HEADS, D_HEAD), blk3),
            pl.BlockSpec((per_blk, N_HEADS, LANES), blk3),
            pl.BlockSpec((per_blk, N_HEADS, D_HEAD, D_HEAD), blk4),
        ],
        out_shape=[
            jax.ShapeDtypeStruct((n_tok, D_MODEL), BF16),
            jax.ShapeDtypeStruct(c0.shape, F32),
            jax.ShapeDtypeStruct((n0.shape[0], N_HEADS, D_HEAD), F32),
            jax.ShapeDtypeStruct(m0.shape, F32),
            jax.ShapeDtypeStruct(r0.shape, F32),
        ],
        scratch_shapes=[pltpu.VMEM((N_HEADS, D_HEAD, 2 * D_HEAD), F32)],
        compiler_params=pltpu.CompilerParams(
            dimension_semantics=("arbitrary",), vmem_limit_bytes=VMEM_LIMIT),
        name="mixer",
    )(qkv, gc, ut, c0, n0, m0, r0, gnm, gnr, decay, cross, tail)


def _mix_out(qkv, gc, ut, x1, x1_short, y_short, bsz, gnm, gnr, wout, g2, wg, wu, wd, gfin):
    n_tok = x1.shape[0]
    tm = TOKEN_TILE
    assert x1_short.shape == (tm, D_MODEL) and y_short.shape == (tm, D_MODEL)
    n_tiles = n_tok // tm
    tiles_per_stream = n_tiles // bsz
    n_chunks = tm // CHUNK
    (decay, cross, tail), ret_total = _mixer_consts(CHUNK)

    mix_row = lambda j: (jnp.minimum(j, n_tiles - 1), 0)
    mix_blk = lambda j: (jnp.minimum(j, n_tiles - 1), 0, 0)
    dense_row = lambda j: (jnp.maximum(j - 1, 0), 0)
    stream4 = lambda j: (jnp.minimum(j, n_tiles - 1) // tiles_per_stream, 0, 0, 0)
    stream3 = lambda j: (jnp.minimum(j, n_tiles - 1) // tiles_per_stream, 0, 0)
    kern = functools.partial(_mix_out_kernel, tiles_per_stream=tiles_per_stream, n_tiles=n_tiles,
                             ret_total=ret_total)
    return pl.pallas_call(
        kern,
        grid=(n_tiles + 1,),
        in_specs=[
            pl.BlockSpec((tm, N_QKV), mix_row),
            pl.BlockSpec((n_chunks, LANES, LANES), mix_blk),
            pl.BlockSpec((n_chunks, 8, LANES), mix_blk),
            pl.BlockSpec((tm, D_MODEL), dense_row),
            _const_spec((tm, D_MODEL)),
            _const_spec((tm, D_MODEL)),
            _const_spec((1, D_GROUP)),
            _const_spec((1, D_GROUP)),
            _const_spec(decay.shape),
            _const_spec(cross.shape),
            _const_spec(tail.shape),
            _const_spec(wout.shape),
            _const_spec((1, D_MODEL)),
            _const_spec(wg.shape),
            _const_spec(wu.shape),
            _const_spec(wd.shape),
            _const_spec((1, D_MODEL)),
        ],
        out_specs=[
            pl.BlockSpec((tm, D_MODEL), dense_row),
            pl.BlockSpec((tm, D_MODEL), lambda j: (0, 0)),
            pl.BlockSpec((1, N_HEADS, D_HEAD, D_HEAD), stream4),
            pl.BlockSpec((1, N_HEADS, D_HEAD), stream3),
            pl.BlockSpec((1, N_HEADS, LANES), stream3),
            pl.BlockSpec((1, N_HEADS, D_HEAD, D_HEAD), stream4),
        ],
        out_shape=[
            jax.ShapeDtypeStruct((n_tok, D_MODEL), F32),
            jax.ShapeDtypeStruct((tm, D_MODEL), F32),
            jax.ShapeDtypeStruct((bsz, N_HEADS, D_HEAD, D_HEAD), F32),
            jax.ShapeDtypeStruct((bsz, N_HEADS, D_HEAD), F32),
            jax.ShapeDtypeStruct((bsz, N_HEADS, LANES), F32),
            jax.ShapeDtypeStruct((bsz, N_HEADS, D_HEAD, D_HEAD), F32),
        ],
        scratch_shapes=[
            pltpu.VMEM((tm, D_MODEL), BF16),
            pltpu.VMEM((tm, D_MODEL), F32),
            pltpu.VMEM((tm, D_MODEL), F32),
            pltpu.VMEM((N_HEADS, D_HEAD, 2 * D_HEAD), F32),
            pltpu.VMEM((8, LANES), F32),
            pltpu.VMEM((N_HEADS, D_HEAD, D_HEAD), F32),
        ],
        compiler_params=pltpu.CompilerParams(
            dimension_semantics=("arbitrary",), vmem_limit_bytes=VMEM_LIMIT),
        name="mix_out",
    )(qkv, gc, ut, x1, x1_short, y_short, gnm, gnr, decay, cross, tail, wout, g2, wg, wu, wd, gfin)


def _rope_tables(pos):
    half = D_HEAD // 2
    inv = ROPE_BASE ** (-np.arange(half, dtype=np.float64) / half)
    ang = np.asarray(pos, np.float64)[:, None] * inv[None, :]
    cos, sin = np.cos(ang), np.sin(ang)
    return (jnp.asarray(np.concatenate([cos, cos], axis=-1), F32),
            jnp.asarray(np.concatenate([-sin, sin], axis=-1), F32))


def kernel(x_prompt, x_sample, state_mlstm_C, state_mlstm_n, state_mlstm_m, state_ret_R,
           g_ffn1, w_ffn1_gate, w_ffn1_up, w_ffn1_down, g_mix, w_in, b_gates, g_gn_mlstm, g_gn_ret,
           w_out, g_ffn2, w_ffn2_gate, w_ffn2_up, w_ffn2_down, g_final):
    depth = g_ffn1.shape[0]
    bp, sp, _ = x_prompt.shape
    bs, ss, _ = x_sample.shape
    assert depth == 1, "the final norm is fused into the layer's last kernel"
    assert ss < CHUNK and LANES % ss == 0 and bs * ss == TOKEN_TILE and sp % TOKEN_TILE == 0
    pos_p = _rope_tables(np.arange(sp))
    cos_s, sin_s = _rope_tables(PAST_LEN + np.arange(ss))
    pos_s = (jnp.tile(cos_s, (bs, 1)), jnp.tile(sin_s, (bs, 1)))

    new_p, new_s = [], []
    for l in range(depth):
        cast = lambda w: w.astype(BF16)
        w_in_t = w_in[l].T
        wi_t = w_in_t[N_QKV:N_QKV + N_HEADS]
        wf_t = w_in_t[N_QKV + N_HEADS:]
        wgate = jnp.concatenate([wi_t, wi_t, wf_t, wf_t], axis=0)
        rows8 = lambda b: jnp.broadcast_to(jnp.tile(b.astype(F32), 2)[:, None], (8, LANES))
        in_w = (g_ffn1[l][None, :], cast(w_ffn1_gate[l]), cast(w_ffn1_up[l]), cast(w_ffn1_down[l]),
                g_mix[l][None, :], cast(w_in_t), wgate,
                rows8(b_gates[l][:N_HEADS]), rows8(b_gates[l][N_HEADS:]))
        gnm, gnr = g_gn_mlstm[l][None, :].astype(F32), g_gn_ret[l][None, :].astype(F32)
        out_w = (cast(w_out[l]), g_ffn2[l][None, :], cast(w_ffn2_gate[l]), cast(w_ffn2_up[l]),
                 cast(w_ffn2_down[l]), g_final[None, :])

        x1s, qkv_s, gc_s, ut_s = _ffn_in(x_sample.reshape(bs * ss, D_MODEL), *in_w, *pos_s, ss)
        m0 = jnp.broadcast_to(state_mlstm_m[l].astype(F32)[..., None], (bs, N_HEADS, LANES))
        n0 = jnp.pad(state_mlstm_n[l].astype(F32), ((0, 0), (0, 8 - N_HEADS), (0, 0)))
        ys, c_s, n_s, m_s, r_s = _mixer(qkv_s, gc_s, ut_s, state_mlstm_C[l].astype(F32), n0, m0,
                                        state_ret_R[l].astype(F32), gnm, gnr, ss)
        x1p, qkv_p, gc_p, ut_p = _ffn_in(x_prompt.reshape(bp * sp, D_MODEL), *in_w, *pos_p, CHUNK)
        out_p, out_s, c_p, n_p, m_p, r_p = _mix_out(qkv_p, gc_p, ut_p, x1p, x1s, ys, bp, gnm, gnr, *out_w)
        new_p.append((c_p, n_p, m_p[..., 0], r_p))
        new_s.append((c_s, n_s, m_s[..., 0], r_s))
    hp, hs = out_p.reshape(bp, sp, D_MODEL), out_s.reshape(bs, ss, D_MODEL)
    stack = lambda sts, i: jnp.stack([s[i] for s in sts])
    return (hp, hs,
            stack(new_p, 0), stack(new_p, 1), stack(new_p, 2), stack(new_p, 3),
            stack(new_s, 0), stack(new_s, 1), stack(new_s, 2), stack(new_s, 3))
```

```python
import functools

import numpy as np
import jax
import jax.numpy as jnp
from jax import lax
from jax.experimental import pallas as pl
from jax.experimental.pallas import tpu as pltpu

F32 = jnp.float32
BF16 = jnp.bfloat16

D_MODEL = 1024
N_HEADS = 4
D_HEAD = 128
D_GROUP = N_HEADS * D_HEAD
D_FF = 2816
PAST_LEN = 4096
ROPE_BASE = 10000.0
EPS = 1e-6
GN_EPS = 1e-5
LOG_GAMMA = np.log(1.0 - 2.0 ** (-5.0 - np.arange(N_HEADS))).astype(np.float32)

LANES = 128
MXU_DIM = 256
FF_CHUNK = MXU_DIM
N_FF_CHUNKS = D_FF // FF_CHUNK
CHUNK = 128
TOKEN_TILE = 512
N_QKV = 8 * D_GROUP
GATE_ROWS = 16
SCAN_COLS = 12
VMEM_LIMIT = 58 * 1024 * 1024

_NT = (((1,), (1,)), ((), ()))
_TN = (((0,), (0,)), ((), ()))


def _rms_norm(x, g):
    ms = jnp.mean(x * x, axis=-1, keepdims=True)
    return x * lax.rsqrt(ms + EPS) * g


def _swiglu_chunk(acc_ref, xn, wg_ref, wu_ref, wd_ref, c):
    cols = slice(c * FF_CHUNK, (c + 1) * FF_CHUNK)
    g = jnp.dot(xn, wg_ref[:, cols], preferred_element_type=F32)
    u = jnp.dot(xn, wu_ref[:, cols], preferred_element_type=F32)
    a = (g * jax.nn.sigmoid(g) * u).astype(BF16)
    d = jnp.dot(a, wd_ref[cols, :], preferred_element_type=F32)
    if c == 0:
        acc_ref[...] = d
    else:
        acc_ref[...] += d


def _swiglu_into(acc_ref, xn, wg_ref, wu_ref, wd_ref):
    for c in range(N_FF_CHUNKS):
        _swiglu_chunk(acc_ref, xn, wg_ref, wu_ref, wd_ref, c)


def _rope_heads(p, cos2, sin2):
    outs = []
    for h in range(N_HEADS):
        xh = p[:, h * D_HEAD:(h + 1) * D_HEAD]
        outs.append(xh * cos2 + pltpu.roll(xh, D_HEAD // 2, axis=1) * sin2)
    return jnp.concatenate(outs, axis=1)


def _log_sigmoid(x):
    return jnp.minimum(x, 0.0) - jnp.log1p(jnp.exp(-jnp.abs(x)))


def _scan_lanes(x, op, fill, pos, seg_len):
    d = 1
    while d < seg_len:
        x = op(x, jnp.where(pos >= d, pltpu.roll(x, d, axis=1), fill))
        d *= 2
    return x


def _gate_scans(gt, bi, bf, seg_len, gc_ref, ut_ref):
    pad_rows = jnp.zeros((LANES - 16, LANES), F32)
    lane = lax.broadcasted_iota(jnp.int32, (8, LANES), 1)
    sub = lax.broadcasted_iota(jnp.int32, (8, LANES), 0)
    pos = lane & (seg_len - 1)
    for c in range(gt.shape[1] // LANES):
        lanes = slice(c * LANES, (c + 1) * LANES)
        ig = gt[0:8, lanes] + bi
        lf = _log_sigmoid(gt[8:16, lanes] + bf)
        b = _scan_lanes(lf, jnp.add, 0.0, pos, seg_len)
        u = ig - b
        cm = _scan_lanes(u, jnp.maximum, -jnp.inf, pos, seg_len)
        packed = jnp.concatenate([jnp.where(sub < N_HEADS, cm, b), jnp.where(sub < N_HEADS, u, 0.0)], axis=0)
        gc_ref[c] = jnp.concatenate([packed, pad_rows], axis=0).T
        ut_ref[c] = u


def _ffn_in_kernel(x_ref, g1_ref, wg_ref, wu_ref, wd_ref, gmix_ref, win_ref, wgate_ref, bi_ref, bf_ref,
                   cos_ref, sin_ref, x1_ref, qkv_ref, gc_ref, ut_ref, acc_ref, *, seg_len):
    x = x_ref[...]
    xn = _rms_norm(x, g1_ref[...]).astype(BF16)
    _swiglu_into(acc_ref, xn, wg_ref, wu_ref, wd_ref)
    x1 = x + 0.5 * acc_ref[...]
    x1_ref[...] = x1
    hn = _rms_norm(x1, gmix_ref[...]).astype(BF16)
    gt = lax.dot_general(wgate_ref[...].astype(BF16), hn, _NT, preferred_element_type=F32)
    _gate_scans(gt, bi_ref[...], bf_ref[...], seg_len, gc_ref, ut_ref)
    cos2 = cos_ref[...]
    sin2 = sin_ref[...]
    key_scale = D_HEAD ** -0.5
    for j in range(8):
        cols = slice(j * D_GROUP, (j + 1) * D_GROUP)
        p = lax.dot_general(hn, win_ref[cols, :], _NT, preferred_element_type=F32)
        if j in (4, 5):
            p = _rope_heads(p, cos2, sin2)
        if j in (1, 5):
            p = p * key_scale
        qkv_ref[:, cols] = p.astype(BF16)


def _const_spec(shape):
    nd = len(shape)
    return pl.BlockSpec(shape, lambda *_: (0,) * nd, pipeline_mode=pl.Buffered(1))


def _ffn_in(x, g1, wg, wu, wd, gmix, win, wgate, bi, bf, cos2, sin2, seg_len):
    n_tok = x.shape[0]
    tm = TOKEN_TILE
    pos_tiles = cos2.shape[0] // tm
    row = lambda i: (i, 0)
    blk = lambda i: (i, 0, 0)
    pos = lambda i: (i % pos_tiles, 0)
    return pl.pallas_call(
        functools.partial(_ffn_in_kernel, seg_len=seg_len),
        grid=(n_tok // tm,),
        in_specs=[
            pl.BlockSpec((tm, D_MODEL), row),
            _const_spec((1, D_MODEL)),
            _const_spec(wg.shape),
            _const_spec(wu.shape),
            _const_spec(wd.shape),
            _const_spec((1, D_MODEL)),
            _const_spec(win.shape),
            _const_spec(wgate.shape),
            _const_spec(bi.shape),
            _const_spec(bf.shape),
            pl.BlockSpec((tm, LANES), pos),
            pl.BlockSpec((tm, LANES), pos),
        ],
        out_specs=[
            pl.BlockSpec((tm, D_MODEL), row),
            pl.BlockSpec((tm, N_QKV), row),
            pl.BlockSpec((tm // LANES, LANES, LANES), blk),
            pl.BlockSpec((tm // LANES, 8, LANES), blk),
        ],
        out_shape=[
            jax.ShapeDtypeStruct((n_tok, D_MODEL), F32),
            jax.ShapeDtypeStruct((n_tok, N_QKV), BF16),
            jax.ShapeDtypeStruct((n_tok // LANES, LANES, LANES), F32),
            jax.ShapeDtypeStruct((n_tok // LANES, 8, LANES), F32),
        ],
        scratch_shapes=[pltpu.VMEM((tm, D_MODEL), F32)],
        compiler_params=pltpu.CompilerParams(
            dimension_semantics=("arbitrary",), vmem_limit_bytes=VMEM_LIMIT),
        name="ffn_in",
    )(x, g1, wg, wu, wd, gmix, win, wgate, bi, bf, cos2, sin2)


def _group_norm(h, g):
    mu = jnp.mean(h, axis=-1, keepdims=True)
    d = h - mu
    var = jnp.mean(d * d, axis=-1, keepdims=True)
    return d * lax.rsqrt(var + GN_EPS) * g


def _chunk_masks(frames):
    row_idx = lax.broadcasted_iota(jnp.int32, (frames, frames), 0)
    col_idx = lax.broadcasted_iota(jnp.int32, (frames, frames), 1)
    return row_idx >= col_idx, jnp.ones((frames, D_HEAD), BF16)


def _mixer_chunk(rows, qkv_ref, gcc, utc, s_ref, m_ref, r_ref, y_ref, consts, masks, ret_total, after_scores=None):
    gnm_ref, gnr_ref, decay_ref, cross_ref, tail_ref = consts
    causal, ones_blk = masks
    L = causal.shape[0]

    def scan_col(q, h):
        j = q * N_HEADS + h
        return jnp.broadcast_to(gcc[:, j:j + 1], (L, LANES))

    def head_slices(group, h):
        base = group * 4 * D_GROUP + h * D_HEAD
        return [qkv_ref[rows, base + j * D_GROUP:base + j * D_GROUP + D_HEAD] for j in range(4)]

    def scores(group):
        if L < LANES:
            return [lax.dot_general(*head_slices(group, h)[:2], _NT, preferred_element_type=F32)
                    for h in range(N_HEADS)]
        out = []
        zero_blk = jnp.zeros((L, D_HEAD), BF16)
        for h in range(0, N_HEADS, 2):
            base = group * 4 * D_GROUP + h * D_HEAD
            q2 = qkv_ref[rows, base:base + 2 * D_HEAD]
            k_a, k_b = head_slices(group, h)[1], head_slices(group, h + 1)[1]
            k_bd = jnp.concatenate([jnp.concatenate([k_a, zero_blk], axis=1),
                                    jnp.concatenate([zero_blk, k_b], axis=1)], axis=0)
            qk2 = lax.dot_general(q2, k_bd, _NT, preferred_element_type=F32)
            out += [qk2[:, :L], qk2[:, L:]]
        return out

    qk_m, qk_r = scores(0), scores(1)
    if after_scores is not None:
        after_scores()

    lhs_m, kw_m, gate_m, lhs_r, kt_r = [], [], [], [], []
    for h in range(N_HEADS):
        cm_b, b_b, u_b = scan_col(0, h), scan_col(1, h), scan_col(2, h)
        m_prev = m_ref[h:h + 1, :]
        big_m = jnp.maximum(cm_b, m_prev)
        a_b = jnp.exp(m_prev - big_m)
        m_t = b_b + big_m
        m_last = big_m[L - 1:L, :]
        m_ref[h:h + 1, :] = m_t[L - 1:L, :]
        w = jnp.where(causal, jnp.exp(utc[h:h + 1, :] - big_m[:, :L]), 0.0)
        q, k, _, _ = head_slices(0, h)
        lhs_m.append(jnp.concatenate([(q.astype(F32) * a_b).astype(BF16), (qk_m[h] * w).astype(BF16)], axis=1))
        kw_m.append((k.astype(F32) * jnp.exp(u_b - m_last)).astype(BF16))
        gate_m.append((a_b[L - 1:L, :], m_t))
    for h in range(N_HEADS):
        q, k, _, _ = head_slices(1, h)
        lhs_r.append(jnp.concatenate([(q.astype(F32) * cross_ref[h]).astype(BF16),
                                      (qk_r[h] * decay_ref[h]).astype(BF16)], axis=1))
        kt_r.append((k.astype(F32) * tail_ref[h]).astype(BF16))

    res_m, upd_m, res_r, upd_r = [], [], [], []
    for h in range(N_HEADS):
        v = head_slices(0, h)[2]
        v_aug = jnp.concatenate([v, ones_blk], axis=1)
        rhs = jnp.concatenate([s_ref[h].astype(BF16), v_aug], axis=0)
        res_m.append(jnp.dot(lhs_m[h], rhs, preferred_element_type=F32))
        upd_m.append(lax.dot_general(kw_m[h], v_aug, _TN, preferred_element_type=F32))
    for h in range(N_HEADS):
        v = head_slices(1, h)[2]
        rhs = jnp.concatenate([r_ref[h].astype(BF16), v], axis=0)
        res_r.append(jnp.dot(lhs_r[h], rhs, preferred_element_type=F32))
        upd_r.append(lax.dot_general(kt_r[h], v, _TN, preferred_element_type=F32))

    for h in range(N_HEADS):
        hc = slice(h * D_HEAD, (h + 1) * D_HEAD)
        a_last, m_t = gate_m[h]
        num = res_m[h][:, :D_HEAD]
        den = res_m[h][:, D_HEAD:]
        hh = num / jnp.maximum(jnp.abs(den), jnp.exp(-m_t))
        s_ref[h] = jnp.concatenate([a_last, a_last], axis=1) * s_ref[h] + upd_m[h]
        o = head_slices(0, h)[3]
        y = jax.nn.sigmoid(o.astype(F32)) * _group_norm(hh, gnm_ref[:, hc])
        y_ref[rows, hc] = y.astype(BF16)
    for h in range(N_HEADS):
        hc = slice(h * D_HEAD, (h + 1) * D_HEAD)
        r_ref[h] = ret_total[h] * r_ref[h] + upd_r[h]
        gf = head_slices(1, h)[3].astype(F32)
        y = gf * jax.nn.sigmoid(gf) * _group_norm(res_r[h], gnr_ref[:, hc])
        y_ref[rows, D_GROUP + h * D_HEAD:D_GROUP + (h + 1) * D_HEAD] = y.astype(BF16)


def _load_state(s_scr, c_ref, n_ref):
    s_scr[:, :, 0:D_HEAD] = c_ref[...]
    n_cols = jnp.concatenate([n_ref[...], jnp.zeros((LANES - 8, D_HEAD), F32)], axis=0).T
    for h in range(N_HEADS):
        s_scr[h, :, D_HEAD:2 * D_HEAD] = jnp.broadcast_to(n_cols[:, h:h + 1], (D_HEAD, LANES))


def _store_state(s_scr, c_ref, n_ref):
    c_ref[...] = s_scr[:, :, 0:D_HEAD]
    lane = lax.broadcasted_iota(jnp.int32, (D_HEAD, LANES), 1)
    n_cols = s_scr[0, :, D_HEAD:2 * D_HEAD]
    for h in range(1, N_HEADS):
        n_cols = jnp.where(lane == h, s_scr[h, :, D_HEAD:2 * D_HEAD], n_cols)
    n_ref[...] = n_cols.T[0:N_HEADS, :]


def _mixer_kernel(qkv_ref, gc_ref, ut_ref, c0_ref, n0_ref, m0_ref, r0_ref, gnm_ref, gnr_ref,
                  decay_ref, cross_ref, tail_ref,
                  y_ref, c_ref, n_ref, m_ref, r_ref, s_scr, *, frames, ret_total):
    m_ref[...] = m0_ref[...]
    r_ref[...] = r0_ref[...]
    consts = (gnm_ref, gnr_ref, decay_ref, cross_ref, tail_ref)
    masks = _chunk_masks(frames)
    for i in range(LANES // frames):
        seg = slice(i * frames, (i + 1) * frames)
        _load_state(s_scr, c0_ref.at[i], n0_ref.at[i])
        _mixer_chunk(seg, qkv_ref, gc_ref[0][seg, :], ut_ref[0][:, seg],
                     s_scr, m_ref.at[i], r_ref.at[i], y_ref, consts, masks, ret_total)
        _store_state(s_scr, c_ref.at[i], n_ref.at[i])


def _mix_out_kernel(qkv_ref, gc_ref, ut_ref, x1_ref, x1s_ref, ys_ref,
                    gnm_ref, gnr_ref, decay_ref, cross_ref, tail_ref,
                    wout_ref, g2_ref, wg_ref, wu_ref, wd_ref, gfin_ref,
                    out_ref, outs_ref, c_out, n_out, m_out, r_out,
                    y_scr, acc_ref, x2_ref, s_scr, m_scr, r_scr, *, tiles_per_stream, n_tiles, ret_total):
    j = pl.program_id(0)
    n_chunks = TOKEN_TILE // CHUNK

    @pl.when(j == 0)
    def _():
        y_scr[...] = ys_ref[...]

    @pl.when(j % tiles_per_stream == 0)
    def _():
        s_scr[...] = jnp.zeros_like(s_scr)
        m_scr[...] = jnp.zeros_like(m_scr)
        r_scr[...] = jnp.zeros_like(r_scr)

    consts = (gnm_ref, gnr_ref, decay_ref, cross_ref, tail_ref)
    masks = _chunk_masks(CHUNK)

    x1 = jnp.where(j == 0, x1s_ref[...], x1_ref[...])
    x2 = x1 + jnp.dot(y_scr[...], wout_ref[...], preferred_element_type=F32)
    x2_ref[...] = x2
    xn = _rms_norm(x2, g2_ref[...]).astype(BF16)

    pending = list(range(N_FF_CHUNKS))

    def ffn(k):
        for _ in range(min(k, len(pending))):
            _swiglu_chunk(acc_ref, xn, wg_ref, wu_ref, wd_ref, pending.pop(0))

    for c in range(n_chunks):
        _mixer_chunk(slice(c * CHUNK, (c + 1) * CHUNK), qkv_ref, gc_ref[c], ut_ref[c],
                     s_scr, m_scr, r_scr, y_scr, consts, masks, ret_total,
                     after_scores=functools.partial(ffn, 1))
        ffn(1)
    ffn(N_FF_CHUNKS)

    x3 = x2_ref[...] + 0.5 * acc_ref[...]
    out_ref[...] = _rms_norm(x3, gfin_ref[...])

    @pl.when(j == 0)
    def _():
        outs_ref[...] = out_ref[...]

    @pl.when((j % tiles_per_stream == tiles_per_stream - 1) & (j < n_tiles))
    def _():
        _store_state(s_scr, c_out.at[0], n_out.at[0])
        m_out[0] = m_scr[0:N_HEADS, :]
        r_out[0] = r_scr[...]


def _mixer_consts(frames):
    idx = np.arange(frames, dtype=np.float32)
    diff = idx[:, None] - idx[None, :]
    lg = LOG_GAMMA
    decay = np.where(diff >= 0, np.exp(lg[:, None, None] * np.maximum(diff, 0.0)), 0.0).astype(np.float32)
    rep = lambda col: np.ascontiguousarray(np.broadcast_to(col.T[:, :, None], (N_HEADS, frames, LANES))).astype(np.float32)
    cross = rep(np.exp(lg[None, :] * (idx[:, None] + 1.0)))
    tail = rep(np.exp(lg[None, :] * (frames - 1.0 - idx[:, None])))
    ret_total = tuple(float(np.exp(np.float32(lg_h) * np.float32(frames))) for lg_h in lg)
    return (jnp.asarray(decay), jnp.asarray(cross), jnp.asarray(tail)), ret_total


def _mixer(qkv, gc, ut, c0, n0, m0, r0, gnm, gnr, frames):
    n_tok = qkv.shape[0]
    per_blk = LANES // frames
    (decay, cross, tail), ret_total = _mixer_consts(frames)
    row = lambda i: (i, 0)
    blk3 = lambda i: (i, 0, 0)
    blk4 = lambda i: (i, 0, 0, 0)
    return pl.pallas_call(
        functools.partial(_mixer_kernel, frames=frames, ret_total=ret_total),
        grid=(n_tok // LANES,),
        in_specs=[
            pl.BlockSpec((LANES, N_QKV), row),
            pl.BlockSpec((1, LANES, LANES), blk3),
            pl.BlockSpec((1, 8, LANES), blk3),
            pl.BlockSpec((per_blk, N_HEADS, D_HEAD, D_HEAD), blk4),
            pl.BlockSpec((per_blk, 8, D_HEAD), blk3),
            pl.BlockSpec((per_blk, N_HEADS, LANES), blk3),
            pl.BlockSpec((per_blk, N_HEADS, D_HEAD, D_HEAD), blk4),
            _const_spec((1, D_GROUP)),
            _const_spec((1, D_GROUP)),
            _const_spec(decay.shape),
            _const_spec(cross.shape),
            _const_spec(tail.shape),
        ],
        out_specs=[
            pl.BlockSpec((LANES, D_MODEL), row),
            pl.BlockSpec((per_blk, N_HEADS, D_HEAD, D_HEAD), blk4),
            pl.BlockSpec((per_blk, N_HEADS, D_HEAD), blk3),
            pl.BlockSpec((per_blk, N_HEADS, LANES), blk3),
            pl.BlockSpec((per_blk, N_HEADS, D_HEAD, D_HEAD), blk4),
        ],
        out_shape=[
            jax.ShapeDtypeStruct((n_tok, D_MODEL), BF16),
            jax.ShapeDtypeStruct(c0.shape, F32),
            jax.ShapeDtypeStruct((n0.shape[0], N_HEADS, D_HEAD), F32),
            jax.ShapeDtypeStruct(m0.shape, F32),
            jax.ShapeDtypeStruct(r0.shape, F32),
        ],
        scratch_shapes=[pltpu.VMEM((N_HEADS, D_HEAD, 2 * D_HEAD), F32)],
        compiler_params=pltpu.CompilerParams(
            dimension_semantics=("arbitrary",), vmem_limit_bytes=VMEM_LIMIT),
        name="mixer",
    )(qkv, gc, ut, c0, n0, m0, r0, gnm, gnr, decay, cross, tail)


def _mix_out(qkv, gc, ut, x1, x1_short, y_short, bsz, gnm, gnr, wout, g2, wg, wu, wd, gfin):
    n_tok = x1.shape[0]
    tm = TOKEN_TILE
    assert x1_short.shape == (tm, D_MODEL) and y_short.shape == (tm, D_MODEL)
    n_tiles = n_tok // tm
    tiles_per_stream = n_tiles // bsz
    n_chunks = tm // CHUNK
    (decay, cross, tail), ret_total = _mixer_consts(CHUNK)

    mix_row = lambda j: (jnp.minimum(j, n_tiles - 1), 0)
    mix_blk = lambda j: (jnp.minimum(j, n_tiles - 1), 0, 0)
    dense_row = lambda j: (jnp.maximum(j - 1, 0), 0)
    stream4 = lambda j: (jnp.minimum(j, n_tiles - 1) // tiles_per_stream, 0, 0, 0)
    stream3 = lambda j: (jnp.minimum(j, n_tiles - 1) // tiles_per_stream, 0, 0)
    kern = functools.partial(_mix_out_kernel, tiles_per_stream=tiles_per_stream, n_tiles=n_tiles,
                             ret_total=ret_total)
    return pl.pallas_call(
        kern,
        grid=(n_tiles + 1,),
        in_specs=[
            pl.BlockSpec((tm, N_QKV), mix_row),
            pl.BlockSpec((n_chunks, LANES, LANES), mix_blk),
            pl.BlockSpec((n_chunks, 8, LANES), mix_blk),
            pl.BlockSpec((tm, D_MODEL), dense_row),
            _const_spec((tm, D_MODEL)),
            _const_spec((tm, D_MODEL)),
            _const_spec((1, D_GROUP)),
            _const_spec((1, D_GROUP)),
            _const_spec(decay.shape),
            _const_spec(cross.shape),
            _const_spec(tail.shape),
            _const_spec(wout.shape),
            _const_spec((1, D_MODEL)),
            _const_spec(wg.shape),
            _const_spec(wu.shape),
            _const_spec(wd.shape),
            _const_spec((1, D_MODEL)),
        ],
        out_specs=[
            pl.BlockSpec((tm, D_MODEL), dense_row),
            pl.BlockSpec((tm, D_MODEL), lambda j: (0, 0)),
            pl.BlockSpec((1, N_HEADS, D_HEAD, D_HEAD), stream4),
            pl.BlockSpec((1, N_HEADS, D_HEAD), stream3),
            pl.BlockSpec((1, N_HEADS, LANES), stream3),
            pl.BlockSpec((1, N_HEADS, D_HEAD, D_HEAD), stream4),
        ],
        out_shape=[
            jax.ShapeDtypeStruct((n_tok, D_MODEL), F32),
            jax.ShapeDtypeStruct((tm, D_MODEL), F32),
            jax.ShapeDtypeStruct((bsz, N_HEADS, D_HEAD, D_HEAD), F32),
            jax.ShapeDtypeStruct((bsz, N_HEADS, D_HEAD), F32),
            jax.ShapeDtypeStruct((bsz, N_HEADS, LANES), F32),
            jax.ShapeDtypeStruct((bsz, N_HEADS, D_HEAD, D_HEAD), F32),
        ],
        scratch_shapes=[
            pltpu.VMEM((tm, D_MODEL), BF16),
            pltpu.VMEM((tm, D_MODEL), F32),
            pltpu.VMEM((tm, D_MODEL), F32),
            pltpu.VMEM((N_HEADS, D_HEAD, 2 * D_HEAD), F32),
            pltpu.VMEM((8, LANES), F32),
            pltpu.VMEM((N_HEADS, D_HEAD, D_HEAD), F32),
        ],
        compiler_params=pltpu.CompilerParams(
            dimension_semantics=("arbitrary",), vmem_limit_bytes=VMEM_LIMIT),
        name="mix_out",
    )(qkv, gc, ut, x1, x1_short, y_short, gnm, gnr, decay, cross, tail, wout, g2, wg, wu, wd, gfin)


def _rope_tables(pos):
    half = D_HEAD // 2
    inv = ROPE_BASE ** (-np.arange(half, dtype=np.float64) / half)
    ang = np.asarray(pos, np.float64)[:, None] * inv[None, :]
    cos, sin = np.cos(ang), np.sin(ang)
    return (jnp.asarray(np.concatenate([cos, cos], axis=-1), F32),
            jnp.asarray(np.concatenate([-sin, sin], axis=-1), F32))


def kernel(x_prompt, x_sample, state_mlstm_C, state_mlstm_n, state_mlstm_m, state_ret_R,
           g_ffn1, w_ffn1_gate, w_ffn1_up, w_ffn1_down, g_mix, w_in, b_gates, g_gn_mlstm, g_gn_ret,
           w_out, g_ffn2, w_ffn2_gate, w_ffn2_up, w_ffn2_down, g_final):
    depth = g_ffn1.shape[0]
    bp, sp, _ = x_prompt.shape
    bs, ss, _ = x_sample.shape
    assert depth == 1, "the final norm is fused into the layer's last kernel"
    assert ss < CHUNK and LANES % ss == 0 and bs * ss == TOKEN_TILE and sp % TOKEN_TILE == 0
    pos_p = _rope_tables(np.arange(sp))
    pos_s = _rope_tables(np.tile(PAST_LEN + np.arange(ss), bs))

    new_p, new_s = [], []
    for l in range(depth):
        cast = lambda w: w.astype(BF16)
        w_in_t = w_in[l].T
        wi_t = w_in_t[N_QKV:N_QKV + N_HEADS]
        wf_t = w_in_t[N_QKV + N_HEADS:]
        wgate = jnp.concatenate([wi_t, wi_t, wf_t, wf_t], axis=0)
        rows8 = lambda b: jnp.broadcast_to(jnp.tile(b.astype(F32), 2)[:, None], (8, LANES))
        in_w = (g_ffn1[l][None, :], cast(w_ffn1_gate[l]), cast(w_ffn1_up[l]), cast(w_ffn1_down[l]),
                g_mix[l][None, :], cast(w_in_t), wgate,
                rows8(b_gates[l][:N_HEADS]), rows8(b_gates[l][N_HEADS:]))
        gnm, gnr = g_gn_mlstm[l][None, :].astype(F32), g_gn_ret[l][None, :].astype(F32)
        out_w = (cast(w_out[l]), g_ffn2[l][None, :], cast(w_ffn2_gate[l]), cast(w_ffn2_up[l]),
                 cast(w_ffn2_down[l]), g_final[None, :])

        x1s, qkv_s, gc_s, ut_s = _ffn_in(x_sample.reshape(bs * ss, D_MODEL), *in_w, *pos_s, ss)
        m0 = jnp.broadcast_to(state_mlstm_m[l].astype(F32)[..., None], (bs, N_HEADS, LANES))
        n0 = jnp.pad(state_mlstm_n[l].astype(F32), ((0, 0), (0, 8 - N_HEADS), (0, 0)))
        ys, c_s, n_s, m_s, r_s = _mixer(qkv_s, gc_s, ut_s, state_mlstm_C[l].astype(F32), n0, m0,
                                        state_ret_R[l].astype(F32), gnm, gnr, ss)
        x1p, qkv_p, gc_p, ut_p = _ffn_in(x_prompt.reshape(bp * sp, D_MODEL), *in_w, *pos_p, CHUNK)
        out_p, out_s, c_p, n_p, m_p, r_p = _mix_out(qkv_p, gc_p, ut_p, x1p, x1s, ys, bp, gnm, gnr, *out_w)
        new_p.append((c_p, n_p, m_p[..., 0], r_p))
        new_s.append((c_s, n_s, m_s[..., 0], r_s))
    hp, hs = out_p.reshape(bp, sp, D_MODEL), out_s.reshape(bs, ss, D_MODEL)
    stack = lambda sts, i: jnp.stack([s[i] for s in sts])
    return (hp, hs,
            stack(new_p, 0), stack(new_p, 1), stack(new_p, 2), stack(new_p, 3),
            stack(new_s, 0), stack(new_s, 1), stack(new_s, 2), stack(new_s, 3))
```

```python
import functools

import numpy as np
import jax
import jax.numpy as jnp
from jax import lax
from jax.experimental import pallas as pl
from jax.experimental.pallas import tpu as pltpu

F32 = jnp.float32
BF16 = jnp.bfloat16

D_MODEL = 1024
N_HEADS = 4
D_HEAD = 128
D_GROUP = N_HEADS * D_HEAD
D_FF = 2816
PAST_LEN = 4096
ROPE_BASE = 10000.0
EPS = 1e-6
GN_EPS = 1e-5
LOG_GAMMA = np.log(1.0 - 2.0 ** (-5.0 - np.arange(N_HEADS))).astype(np.float32)

LANES = 128
MXU_DIM = 256
FF_CHUNK = MXU_DIM
N_FF_CHUNKS = D_FF // FF_CHUNK
CHUNK = 128
TOKEN_TILE = 512
N_QKV = 8 * D_GROUP
GATE_ROWS = 16
SCAN_COLS = 12
VMEM_LIMIT = 58 * 1024 * 1024

_NT = (((1,), (1,)), ((), ()))
_TN = (((0,), (0,)), ((), ()))


def _rms_norm(x, g):
    ms = jnp.mean(x * x, axis=-1, keepdims=True)
    return x * lax.rsqrt(ms + EPS) * g


def _swiglu_chunk(acc_ref, xn, wg_ref, wu_ref, wd_ref, c):
    cols = slice(c * FF_CHUNK, (c + 1) * FF_CHUNK)
    g = jnp.dot(xn, wg_ref[:, cols], preferred_element_type=F32)
    u = jnp.dot(xn, wu_ref[:, cols], preferred_element_type=F32)
    a = (g * jax.nn.sigmoid(g) * u).astype(BF16)
    d = jnp.dot(a, wd_ref[cols, :], preferred_element_type=F32)
    if c == 0:
        acc_ref[...] = d
    else:
        acc_ref[...] += d


def _swiglu_into(acc_ref, xn, wg_ref, wu_ref, wd_ref):
    for c in range(N_FF_CHUNKS):
        _swiglu_chunk(acc_ref, xn, wg_ref, wu_ref, wd_ref, c)


def _rope_heads(p, cos2, sin2):
    outs = []
    for h in range(N_HEADS):
        xh = p[:, h * D_HEAD:(h + 1) * D_HEAD]
        outs.append(xh * cos2 + pltpu.roll(xh, D_HEAD // 2, axis=1) * sin2)
    return jnp.concatenate(outs, axis=1)


def _log_sigmoid(x):
    return jnp.minimum(x, 0.0) - jnp.log1p(jnp.exp(-jnp.abs(x)))


def _scan_lanes(x, op, fill, pos, seg_len):
    d = 1
    while d < seg_len:
        x = op(x, jnp.where(pos >= d, pltpu.roll(x, d, axis=1), fill))
        d *= 2
    return x


def _gate_scans(gt, bi, bf, seg_len, gc_ref, ut_ref):
    pad_rows = jnp.zeros((LANES - 16, LANES), F32)
    lane = lax.broadcasted_iota(jnp.int32, (8, LANES), 1)
    sub = lax.broadcasted_iota(jnp.int32, (8, LANES), 0)
    pos = lane & (seg_len - 1)
    for c in range(gt.shape[1] // LANES):
        lanes = slice(c * LANES, (c + 1) * LANES)
        ig = gt[0:8, lanes] + bi
        lf = _log_sigmoid(gt[8:16, lanes] + bf)
        b = _scan_lanes(lf, jnp.add, 0.0, pos, seg_len)
        u = ig - b
        cm = _scan_lanes(u, jnp.maximum, -jnp.inf, pos, seg_len)
        packed = jnp.concatenate([jnp.where(sub < N_HEADS, cm, b), jnp.where(sub < N_HEADS, u, 0.0)], axis=0)
        gc_ref[c] = jnp.concatenate([packed, pad_rows], axis=0).T
        ut_ref[c] = u


def _ffn_in_kernel(x_ref, g1_ref, wg_ref, wu_ref, wd_ref, gmix_ref, win_ref, wgate_ref, bi_ref, bf_ref,
                   cos_ref, sin_ref, x1_ref, qkv_ref, gc_ref, ut_ref, acc_ref, *, seg_len):
    x = x_ref[...]
    xn = _rms_norm(x, g1_ref[...]).astype(BF16)
    _swiglu_into(acc_ref, xn, wg_ref, wu_ref, wd_ref)
    x1 = x + 0.5 * acc_ref[...]
    x1_ref[...] = x1
    hn = _rms_norm(x1, gmix_ref[...]).astype(BF16)
    gt = lax.dot_general(wgate_ref[...].astype(BF16), hn, _NT, preferred_element_type=F32)
    _gate_scans(gt, bi_ref[...], bf_ref[...], seg_len, gc_ref, ut_ref)
    cos2 = cos_ref[...]
    sin2 = sin_ref[...]
    key_scale = D_HEAD ** -0.5
    for j in range(8):
        cols = slice(j * D_GROUP, (j + 1) * D_GROUP)
        p = lax.dot_general(hn, win_ref[cols, :], _NT, preferred_element_type=F32)
        if j in (4, 5):
            p = _rope_heads(p, cos2, sin2)
        if j in (1, 5):
            p = p * key_scale
        qkv_ref[:, cols] = p.astype(BF16)


def _const_spec(shape):
    nd = len(shape)
    return pl.BlockSpec(shape, lambda *_: (0,) * nd, pipeline_mode=pl.Buffered(1))


def _ffn_in(x, g1, wg, wu, wd, gmix, win, wgate, bi, bf, cos2, sin2, seg_len):
    n_tok = x.shape[0]
    tm = TOKEN_TILE
    pos_tiles = cos2.shape[0] // tm
    row = lambda i: (i, 0)
    blk = lambda i: (i, 0, 0)
    pos = lambda i: (i % pos_tiles, 0)
    return pl.pallas_call(
        functools.partial(_ffn_in_kernel, seg_len=seg_len),
        grid=(n_tok // tm,),
        in_specs=[
            pl.BlockSpec((tm, D_MODEL), row),
            _const_spec((1, D_MODEL)),
            _const_spec(wg.shape),
            _const_spec(wu.shape),
            _const_spec(wd.shape),
            _const_spec((1, D_MODEL)),
            _const_spec(win.shape),
            _const_spec(wgate.shape),
            _const_spec(bi.shape),
            _const_spec(bf.shape),
            pl.BlockSpec((tm, LANES), pos),
            pl.BlockSpec((tm, LANES), pos),
        ],
        out_specs=[
            pl.BlockSpec((tm, D_MODEL), row),
            pl.BlockSpec((tm, N_QKV), row),
            pl.BlockSpec((tm // LANES, LANES, LANES), blk),
            pl.BlockSpec((tm // LANES, 8, LANES), blk),
        ],
        out_shape=[
            jax.ShapeDtypeStruct((n_tok, D_MODEL), F32),
            jax.ShapeDtypeStruct((n_tok, N_QKV), BF16),
            jax.ShapeDtypeStruct((n_tok // LANES, LANES, LANES), F32),
            jax.ShapeDtypeStruct((n_tok // LANES, 8, LANES), F32),
        ],
        scratch_shapes=[pltpu.VMEM((tm, D_MODEL), F32)],
        compiler_params=pltpu.CompilerParams(
            dimension_semantics=("arbitrary",), vmem_limit_bytes=VMEM_LIMIT),
        name="ffn_in",
    )(x, g1, wg, wu, wd, gmix, win, wgate, bi, bf, cos2, sin2)


def _group_norm(h, g):
    mu = jnp.mean(h, axis=-1, keepdims=True)
    d = h - mu
    var = jnp.mean(d * d, axis=-1, keepdims=True)
    return d * lax.rsqrt(var + GN_EPS) * g


def _chunk_masks(frames):
    row_idx = lax.broadcasted_iota(jnp.int32, (frames, frames), 0)
    col_idx = lax.broadcasted_iota(jnp.int32, (frames, frames), 1)
    return row_idx >= col_idx, jnp.ones((frames, D_HEAD), BF16)


def _mixer_chunk(rows, qkv_ref, gcc, utc, s_ref, m_ref, r_ref, y_ref, consts, masks, ret_total, after_scores=None):
    gnm_ref, gnr_ref, decay_ref, cross_ref, tail_ref = consts
    causal, ones_blk = masks
    L = causal.shape[0]

    def scan_col(q, h):
        j = q * N_HEADS + h
        return jnp.broadcast_to(gcc[:, j:j + 1], (L, LANES))

    def head_slices(group, h):
        base = group * 4 * D_GROUP + h * D_HEAD
        return [qkv_ref[rows, base + j * D_GROUP:base + j * D_GROUP + D_HEAD] for j in range(4)]

    def scores(group):
        if L < LANES:
            return [lax.dot_general(*head_slices(group, h)[:2], _NT, preferred_element_type=F32)
                    for h in range(N_HEADS)]
        out = []
        zero_blk = jnp.zeros((L, D_HEAD), BF16)
        for h in range(0, N_HEADS, 2):
            base = group * 4 * D_GROUP + h * D_HEAD
            q2 = qkv_ref[rows, base:base + 2 * D_HEAD]
            k_a, k_b = head_slices(group, h)[1], head_slices(group, h + 1)[1]
            k_bd = jnp.concatenate([jnp.concatenate([k_a, zero_blk], axis=1),
                                    jnp.concatenate([zero_blk, k_b], axis=1)], axis=0)
            qk2 = lax.dot_general(q2, k_bd, _NT, preferred_element_type=F32)
            out += [qk2[:, :L], qk2[:, L:]]
        return out

    qk_m, qk_r = scores(0), scores(1)
    if after_scores is not None:
        after_scores()

    lhs_m, kw_m, gate_m, lhs_r, kt_r = [], [], [], [], []
    for h in range(N_HEADS):
        cm_b, b_b, u_b = scan_col(0, h), scan_col(1, h), scan_col(2, h)
        m_prev = m_ref[h:h + 1, :]
        big_m = jnp.maximum(cm_b, m_prev)
        a_b = jnp.exp(m_prev - big_m)
        m_t = b_b + big_m
        m_last = big_m[L - 1:L, :]
        m_ref[h:h + 1, :] = m_t[L - 1:L, :]
        w = jnp.where(causal, jnp.exp(utc[h:h + 1, :] - big_m[:, :L]), 0.0)
        q, k, _, _ = head_slices(0, h)
        lhs_m.append(jnp.concatenate([(q.astype(F32) * a_b).astype(BF16), (qk_m[h] * w).astype(BF16)], axis=1))
        kw_m.append((k.astype(F32) * jnp.exp(u_b - m_last)).astype(BF16))
        gate_m.append((a_b[L - 1:L, :], m_t))
    for h in range(N_HEADS):
        q, k, _, _ = head_slices(1, h)
        lhs_r.append(jnp.concatenate([(q.astype(F32) * cross_ref[h]).astype(BF16),
                                      (qk_r[h] * decay_ref[h]).astype(BF16)], axis=1))
        kt_r.append((k.astype(F32) * tail_ref[h]).astype(BF16))

    res_m, upd_m, res_r, upd_r = [], [], [], []
    for h in range(N_HEADS):
        v = head_slices(0, h)[2]
        v_aug = jnp.concatenate([v, ones_blk], axis=1)
        rhs = jnp.concatenate([s_ref[h].astype(BF16), v_aug], axis=0)
        res_m.append(jnp.dot(lhs_m[h], rhs, preferred_element_type=F32))
        upd_m.append(lax.dot_general(kw_m[h], v_aug, _TN, preferred_element_type=F32))
    for h in range(N_HEADS):
        v = head_slices(1, h)[2]
        rhs = jnp.concatenate([r_ref[h].astype(BF16), v], axis=0)
        res_r.append(jnp.dot(lhs_r[h], rhs, preferred_element_type=F32))
        upd_r.append(lax.dot_general(kt_r[h], v, _TN, preferred_element_type=F32))

    for h in range(N_HEADS):
        hc = slice(h * D_HEAD, (h + 1) * D_HEAD)
        a_last, m_t = gate_m[h]
        num = res_m[h][:, :D_HEAD]
        den = res_m[h][:, D_HEAD:]
        hh = num / jnp.maximum(jnp.abs(den), jnp.exp(-m_t))
        s_ref[h] = jnp.concatenate([a_last, a_last], axis=1) * s_ref[h] + upd_m[h]
        o = head_slices(0, h)[3]
        y = jax.nn.sigmoid(o.astype(F32)) * _group_norm(hh, gnm_ref[:, hc])
        y_ref[rows, hc] = y.astype(BF16)
    for h in range(N_HEADS):
        hc = slice(h * D_HEAD, (h + 1) * D_HEAD)
        r_ref[h] = ret_total[h] * r_ref[h] + upd_r[h]
        gf = head_slices(1, h)[3].astype(F32)
        y = gf * jax.nn.sigmoid(gf) * _group_norm(res_r[h], gnr_ref[:, hc])
        y_ref[rows, D_GROUP + h * D_HEAD:D_GROUP + (h + 1) * D_HEAD] = y.astype(BF16)


def _load_state(s_scr, c_ref, n_ref):
    s_scr[:, :, 0:D_HEAD] = c_ref[...]
    n_cols = jnp.concatenate([n_ref[...], jnp.zeros((LANES - 8, D_HEAD), F32)], axis=0).T
    for h in range(N_HEADS):
        s_scr[h, :, D_HEAD:2 * D_HEAD] = jnp.broadcast_to(n_cols[:, h:h + 1], (D_HEAD, LANES))


def _store_state(s_scr, c_ref, n_ref):
    c_ref[...] = s_scr[:, :, 0:D_HEAD]
    lane = lax.broadcasted_iota(jnp.int32, (D_HEAD, LANES), 1)
    n_cols = s_scr[0, :, D_HEAD:2 * D_HEAD]
    for h in range(1, N_HEADS):
        n_cols = jnp.where(lane == h, s_scr[h, :, D_HEAD:2 * D_HEAD], n_cols)
    n_ref[...] = n_cols.T[0:N_HEADS, :]


def _mixer_kernel(qkv_ref, gc_ref, ut_ref, c0_ref, n0_ref, m0_ref, r0_ref, gnm_ref, gnr_ref,
                  decay_ref, cross_ref, tail_ref,
                  y_ref, c_ref, n_ref, m_ref, r_ref, s_scr, *, frames, ret_total):
    m_ref[...] = m0_ref[...]
    r_ref[...] = r0_ref[...]
    consts = (gnm_ref, gnr_ref, decay_ref, cross_ref, tail_ref)
    masks = _chunk_masks(frames)
    streams = range(LANES // frames)
    for i in streams:
        _load_state(s_scr.at[i], c0_ref.at[i], n0_ref.at[i])
    for i in streams:
        seg = slice(i * frames, (i + 1) * frames)
        _mixer_chunk(seg, qkv_ref, gc_ref[0][seg, :], ut_ref[0][:, seg],
                     s_scr.at[i], m_ref.at[i], r_ref.at[i], y_ref, consts, masks, ret_total)
    for i in streams:
        _store_state(s_scr.at[i], c_ref.at[i], n_ref.at[i])


def _mix_out_kernel(qkv_ref, gc_ref, ut_ref, x1_ref, x1s_ref, ys_ref,
                    gnm_ref, gnr_ref, decay_ref, cross_ref, tail_ref,
                    wout_ref, g2_ref, wg_ref, wu_ref, wd_ref, gfin_ref,
                    out_ref, outs_ref, c_out, n_out, m_out, r_out,
                    y_scr, acc_ref, x2_ref, s_scr, m_scr, r_scr, *, tiles_per_stream, n_tiles, ret_total):
    j = pl.program_id(0)
    n_chunks = TOKEN_TILE // CHUNK

    @pl.when(j == 0)
    def _():
        y_scr[...] = ys_ref[...]

    @pl.when(j % tiles_per_stream == 0)
    def _():
        s_scr[...] = jnp.zeros_like(s_scr)
        m_scr[...] = jnp.zeros_like(m_scr)
        r_scr[...] = jnp.zeros_like(r_scr)

    consts = (gnm_ref, gnr_ref, decay_ref, cross_ref, tail_ref)
    masks = _chunk_masks(CHUNK)

    x1 = jnp.where(j == 0, x1s_ref[...], x1_ref[...])
    x2 = x1 + jnp.dot(y_scr[...], wout_ref[...], preferred_element_type=F32)
    x2_ref[...] = x2
    xn = _rms_norm(x2, g2_ref[...]).astype(BF16)

    pending = list(range(N_FF_CHUNKS))

    def ffn(k):
        for _ in range(min(k, len(pending))):
            _swiglu_chunk(acc_ref, xn, wg_ref, wu_ref, wd_ref, pending.pop(0))

    for c in range(n_chunks):
        _mixer_chunk(slice(c * CHUNK, (c + 1) * CHUNK), qkv_ref, gc_ref[c], ut_ref[c],
                     s_scr, m_scr, r_scr, y_scr, consts, masks, ret_total,
                     after_scores=functools.partial(ffn, 1))
        ffn(1)
    ffn(N_FF_CHUNKS)

    x3 = x2_ref[...] + 0.5 * acc_ref[...]
    out_ref[...] = _rms_norm(x3, gfin_ref[...])

    @pl.when(j == 0)
    def _():
        outs_ref[...] = out_ref[...]

    @pl.when((j % tiles_per_stream == tiles_per_stream - 1) & (j < n_tiles))
    def _():
        _store_state(s_scr, c_out.at[0], n_out.at[0])
        m_out[0] = m_scr[0:N_HEADS, :]
        r_out[0] = r_scr[...]


def _mixer_consts(frames):
    idx = np.arange(frames, dtype=np.float32)
    diff = idx[:, None] - idx[None, :]
    lg = LOG_GAMMA
    decay = np.where(diff >= 0, np.exp(lg[:, None, None] * np.maximum(diff, 0.0)), 0.0).astype(np.float32)
    rep = lambda col: np.ascontiguousarray(np.broadcast_to(col.T[:, :, None], (N_HEADS, frames, LANES))).astype(np.float32)
    cross = rep(np.exp(lg[None, :] * (idx[:, None] + 1.0)))
    tail = rep(np.exp(lg[None, :] * (frames - 1.0 - idx[:, None])))
    ret_total = tuple(float(np.exp(np.float32(lg_h) * np.float32(frames))) for lg_h in lg)
    return (jnp.asarray(decay), jnp.asarray(cross), jnp.asarray(tail)), ret_total


def _mixer(qkv, gc, ut, c0, n0, m0, r0, gnm, gnr, frames):
    n_tok = qkv.shape[0]
    per_blk = LANES // frames
    (decay, cross, tail), ret_total = _mixer_consts(frames)
    row = lambda i: (i, 0)
    blk3 = lambda i: (i, 0, 0)
    blk4 = lambda i: (i, 0, 0, 0)
    return pl.pallas_call(
        functools.partial(_mixer_kernel, frames=frames, ret_total=ret_total),
        grid=(n_tok // LANES,),
        in_specs=[
            pl.BlockSpec((LANES, N_QKV), row),
            pl.BlockSpec((1, LANES, LANES), blk3),
            pl.BlockSpec((1, 8, LANES), blk3),
            pl.BlockSpec((per_blk, N_HEADS, D_HEAD, D_HEAD), blk4),
            pl.BlockSpec((per_blk, 8, D_HEAD), blk3),
            pl.BlockSpec((per_blk, N_HEADS, LANES), blk3),
            pl.BlockSpec((per_blk, N_HEADS, D_HEAD, D_HEAD), blk4),
            _const_spec((1, D_GROUP)),
            _const_spec((1, D_GROUP)),
            _const_spec(decay.shape),
            _const_spec(cross.shape),
            _const_spec(tail.shape),
        ],
        out_specs=[
            pl.BlockSpec((LANES, D_MODEL), row),
            pl.BlockSpec((per_blk, N_HEADS, D_HEAD, D_HEAD), blk4),
            pl.BlockSpec((per_blk, N_HEADS, D_HEAD), blk3),
            pl.BlockSpec((per_blk, N_HEADS, LANES), blk3),
            pl.BlockSpec((per_blk, N_HEADS, D_HEAD, D_HEAD), blk4),
        ],
        out_shape=[
            jax.ShapeDtypeStruct((n_tok, D_MODEL), BF16),
            jax.ShapeDtypeStruct(c0.shape, F32),
            jax.ShapeDtypeStruct((n0.shape[0], N_HEADS, D_HEAD), F32),
            jax.ShapeDtypeStruct(m0.shape, F32),
            jax.ShapeDtypeStruct(r0.shape, F32),
        ],
        scratch_shapes=[pltpu.VMEM((per_blk, N_HEADS, D_HEAD, 2 * D_HEAD), F32)],
        compiler_params=pltpu.CompilerParams(
            dimension_semantics=("arbitrary",), vmem_limit_bytes=VMEM_LIMIT),
        name="mixer",
    )(qkv, gc, ut, c0, n0, m0, r0, gnm, gnr, decay, cross, tail)


def _mix_out(qkv, gc, ut, x1, x1_short, y_short, bsz, gnm, gnr, wout, g2, wg, wu, wd, gfin):
    n_tok = x1.shape[0]
    tm = TOKEN_TILE
    assert x1_short.shape == (tm, D_MODEL) and y_short.shape == (tm, D_MODEL)
    n_tiles = n_tok // tm
    tiles_per_stream = n_tiles // bsz
    n_chunks = tm // CHUNK
    (decay, cross, tail), ret_total = _mixer_consts(CHUNK)

    mix_row = lambda j: (jnp.minimum(j, n_tiles - 1), 0)
    mix_blk = lambda j: (jnp.minimum(j, n_tiles - 1), 0, 0)
    dense_row = lambda j: (jnp.maximum(j - 1, 0), 0)
    stream4 = lambda j: (jnp.minimum(j, n_tiles - 1) // tiles_per_stream, 0, 0, 0)
    stream3 = lambda j: (jnp.minimum(j, n_tiles - 1) // tiles_per_stream, 0, 0)
    kern = functools.partial(_mix_out_kernel, tiles_per_stream=tiles_per_stream, n_tiles=n_tiles,
                             ret_total=ret_total)
    return pl.pallas_call(
        kern,
        grid=(n_tiles + 1,),
        in_specs=[
            pl.BlockSpec((tm, N_QKV), mix_row),
            pl.BlockSpec((n_chunks, LANES, LANES), mix_blk),
            pl.BlockSpec((n_chunks, 8, LANES), mix_blk),
            pl.BlockSpec((tm, D_MODEL), dense_row),
            _const_spec((tm, D_MODEL)),
            _const_spec((tm, D_MODEL)),
            _const_spec((1, D_GROUP)),
            _const_spec((1, D_GROUP)),
            _const_spec(decay.shape),
            _const_spec(cross.shape),
            _const_spec(tail.shape),
            _const_spec(wout.shape),
            _const_spec((1, D_MODEL)),
            _const_spec(wg.shape),
            _const_spec(wu.shape),
            _const_spec(wd.shape),
            _const_spec((1, D_MODEL)),
        ],
        out_specs=[
            pl.BlockSpec((tm, D_MODEL), dense_row),
            pl.BlockSpec((tm, D_MODEL), lambda j: (0, 0)),
            pl.BlockSpec((1, N_HEADS, D_HEAD, D_HEAD), stream4),
            pl.BlockSpec((1, N_HEADS, D_HEAD), stream3),
            pl.BlockSpec((1, N_HEADS, LANES), stream3),
            pl.BlockSpec((1, N_HEADS, D_HEAD, D_HEAD), stream4),
        ],
        out_shape=[
            jax.ShapeDtypeStruct((n_tok, D_MODEL), F32),
            jax.ShapeDtypeStruct((tm, D_MODEL), F32),
            jax.ShapeDtypeStruct((bsz, N_HEADS, D_HEAD, D_HEAD), F32),
            jax.ShapeDtypeStruct((bsz, N_HEADS, D_HEAD), F32),
            jax.ShapeDtypeStruct((bsz, N_HEADS, LANES), F32),
            jax.ShapeDtypeStruct((bsz, N_HEADS, D_HEAD, D_HEAD), F32),
        ],
        scratch_shapes=[
            pltpu.VMEM((tm, D_MODEL), BF16),
            pltpu.VMEM((tm, D_MODEL), F32),
            pltpu.VMEM((tm, D_MODEL), F32),
            pltpu.VMEM((N_HEADS, D_HEAD, 2 * D_HEAD), F32),
            pltpu.VMEM((8, LANES), F32),
            pltpu.VMEM((N_HEADS, D_HEAD, D_HEAD), F32),
        ],
        compiler_params=pltpu.CompilerParams(
            dimension_semantics=("arbitrary",), vmem_limit_bytes=VMEM_LIMIT),
        name="mix_out",
    )(qkv, gc, ut, x1, x1_short, y_short, gnm, gnr, decay, cross, tail, wout, g2, wg, wu, wd, gfin)


def _rope_tables(pos):
    half = D_HEAD // 2
    inv = ROPE_BASE ** (-np.arange(half, dtype=np.float64) / half)
    ang = np.asarray(pos, np.float64)[:, None] * inv[None, :]
    cos, sin = np.cos(ang), np.sin(ang)
    return (jnp.asarray(np.concatenate([cos, cos], axis=-1), F32),
            jnp.asarray(np.concatenate([-sin, sin], axis=-1), F32))


def kernel(x_prompt, x_sample, state_mlstm_C, state_mlstm_n, state_mlstm_m, state_ret_R,
           g_ffn1, w_ffn1_gate, w_ffn1_up, w_ffn1_down, g_mix, w_in, b_gates, g_gn_mlstm, g_gn_ret,
           w_out, g_ffn2, w_ffn2_gate, w_ffn2_up, w_ffn2_down, g_final):
    depth = g_ffn1.shape[0]
    bp, sp, _ = x_prompt.shape
    bs, ss, _ = x_sample.shape
    assert depth == 1, "the final norm is fused into the layer's last kernel"
    assert ss < CHUNK and LANES % ss == 0 and bs * ss == TOKEN_TILE and sp % TOKEN_TILE == 0
    pos_p = _rope_tables(np.arange(sp))
    pos_s = _rope_tables(np.tile(PAST_LEN + np.arange(ss), bs))

    new_p, new_s = [], []
    for l in range(depth):
        cast = lambda w: w.astype(BF16)
        w_in_t = w_in[l].T
        wi_t = w_in_t[N_QKV:N_QKV + N_HEADS]
        wf_t = w_in_t[N_QKV + N_HEADS:]
        wgate = jnp.concatenate([wi_t, wi_t, wf_t, wf_t], axis=0)
        rows8 = lambda b: jnp.broadcast_to(jnp.tile(b.astype(F32), 2)[:, None], (8, LANES))
        in_w = (g_ffn1[l][None, :], cast(w_ffn1_gate[l]), cast(w_ffn1_up[l]), cast(w_ffn1_down[l]),
                g_mix[l][None, :], cast(w_in_t), wgate,
                rows8(b_gates[l][:N_HEADS]), rows8(b_gates[l][N_HEADS:]))
        gnm, gnr = g_gn_mlstm[l][None, :].astype(F32), g_gn_ret[l][None, :].astype(F32)
        out_w = (cast(w_out[l]), g_ffn2[l][None, :], cast(w_ffn2_gate[l]), cast(w_ffn2_up[l]),
                 cast(w_ffn2_down[l]), g_final[None, :])

        x1s, qkv_s, gc_s, ut_s = _ffn_in(x_sample.reshape(bs * ss, D_MODEL), *in_w, *pos_s, ss)
        m0 = jnp.broadcast_to(state_mlstm_m[l].astype(F32)[..., None], (bs, N_HEADS, LANES))
        n0 = jnp.pad(state_mlstm_n[l].astype(F32), ((0, 0), (0, 8 - N_HEADS), (0, 0)))
        ys, c_s, n_s, m_s, r_s = _mixer(qkv_s, gc_s, ut_s, state_mlstm_C[l].astype(F32), n0, m0,
                                        state_ret_R[l].astype(F32), gnm, gnr, ss)
        x1p, qkv_p, gc_p, ut_p = _ffn_in(x_prompt.reshape(bp * sp, D_MODEL), *in_w, *pos_p, CHUNK)
        out_p, out_s, c_p, n_p, m_p, r_p = _mix_out(qkv_p, gc_p, ut_p, x1p, x1s, ys, bp, gnm, gnr, *out_w)
        new_p.append((c_p, n_p, m_p[..., 0], r_p))
        new_s.append((c_s, n_s, m_s[..., 0], r_s))
    hp, hs = out_p.reshape(bp, sp, D_MODEL), out_s.reshape(bs, ss, D_MODEL)
    stack = lambda sts, i: jnp.stack([s[i] for s in sts])
    return (hp, hs,
            stack(new_p, 0), stack(new_p, 1), stack(new_p, 2), stack(new_p, 3),
            stack(new_s, 0), stack(new_s, 1), stack(new_s, 2), stack(new_s, 3))
```

```python
import functools

import numpy as np
import jax
import jax.numpy as jnp
from jax import lax
from jax.experimental import pallas as pl
from jax.experimental.pallas import tpu as pltpu

F32 = jnp.float32
BF16 = jnp.bfloat16

D_MODEL = 1024
N_HEADS = 4
D_HEAD = 128
D_GROUP = N_HEADS * D_HEAD
D_FF = 2816
PAST_LEN = 4096
ROPE_BASE = 10000.0
EPS = 1e-6
GN_EPS = 1e-5
LOG_GAMMA = np.log(1.0 - 2.0 ** (-5.0 - np.arange(N_HEADS))).astype(np.float32)

LANES = 128
MXU_DIM = 256
FF_CHUNK = MXU_DIM
N_FF_CHUNKS = D_FF // FF_CHUNK
CHUNK = 128
TOKEN_TILE = 512
N_QKV = 8 * D_GROUP
GATE_ROWS = 16
SCAN_COLS = 12
VMEM_LIMIT = 58 * 1024 * 1024

_NT = (((1,), (1,)), ((), ()))
_TN = (((0,), (0,)), ((), ()))


def _rms_norm(x, g):
    ms = jnp.mean(x * x, axis=-1, keepdims=True)
    return x * lax.rsqrt(ms + EPS) * g


def _swiglu_chunk(acc_ref, xn, wg_ref, wu_ref, wd_ref, c):
    cols = slice(c * FF_CHUNK, (c + 1) * FF_CHUNK)
    g = jnp.dot(xn, wg_ref[:, cols], preferred_element_type=F32)
    u = jnp.dot(xn, wu_ref[:, cols], preferred_element_type=F32)
    a = (g * jax.nn.sigmoid(g) * u).astype(BF16)
    d = jnp.dot(a, wd_ref[cols, :], preferred_element_type=F32)
    if c == 0:
        acc_ref[...] = d
    else:
        acc_ref[...] += d


def _swiglu_into(acc_ref, xn, wg_ref, wu_ref, wd_ref):
    for c in range(N_FF_CHUNKS):
        _swiglu_chunk(acc_ref, xn, wg_ref, wu_ref, wd_ref, c)


def _rope_heads(p, cos2, sin2):
    outs = []
    for h in range(N_HEADS):
        xh = p[:, h * D_HEAD:(h + 1) * D_HEAD]
        outs.append(xh * cos2 + pltpu.roll(xh, D_HEAD // 2, axis=1) * sin2)
    return jnp.concatenate(outs, axis=1)


def _log_sigmoid(x):
    return jnp.minimum(x, 0.0) - jnp.log1p(jnp.exp(-jnp.abs(x)))


def _scan_lanes(x, op, fill, pos):
    d = 1
    while d < LANES:
        x = op(x, jnp.where(pos >= d, pltpu.roll(x, d, axis=1), fill))
        d *= 2
    return x


def _gate_scans(gt, bi, bf, seg_len, gc_ref, ut_ref):
    pad_rows = jnp.zeros((LANES - 16, LANES), F32)
    lane = lax.broadcasted_iota(jnp.int32, (8, LANES), 1)
    sub = lax.broadcasted_iota(jnp.int32, (8, LANES), 0)
    pos = lane & (seg_len - 1)
    for c in range(gt.shape[1] // LANES):
        lanes = slice(c * LANES, (c + 1) * LANES)
        ig = gt[0:8, lanes] + bi
        lf = _log_sigmoid(gt[8:16, lanes] + bf)
        b = _scan_lanes(lf, jnp.add, 0.0, pos)
        u = ig - b
        cm = _scan_lanes(u, jnp.maximum, -jnp.inf, pos)
        packed = jnp.concatenate([jnp.where(sub < N_HEADS, cm, b), jnp.where(sub < N_HEADS, u, 0.0)], axis=0)
        gc_ref[c] = jnp.concatenate([packed, pad_rows], axis=0).T
        ut_ref[c] = u


def _ffn_in_kernel(x_ref, xs_ref, g1_ref, wg_ref, wu_ref, wd_ref, gmix_ref, win_ref, wgate_ref, bi_ref, bf_ref,
                   cos_ref, sin_ref, x1_ref, qkv_ref, gc_ref, ut_ref, acc_ref, *, short_len):
    is_short = pl.program_id(0) == pl.num_programs(0) - 1
    seg_len = jnp.where(is_short, short_len, CHUNK)
    x = jnp.where(is_short, xs_ref[...], x_ref[...])
    xn = _rms_norm(x, g1_ref[...]).astype(BF16)
    _swiglu_into(acc_ref, xn, wg_ref, wu_ref, wd_ref)
    x1 = x + 0.5 * acc_ref[...]
    x1_ref[...] = x1
    hn = _rms_norm(x1, gmix_ref[...]).astype(BF16)
    gt = lax.dot_general(wgate_ref[...].astype(BF16), hn, _NT, preferred_element_type=F32)
    _gate_scans(gt, bi_ref[...], bf_ref[...], seg_len, gc_ref, ut_ref)
    cos2 = cos_ref[...]
    sin2 = sin_ref[...]
    key_scale = D_HEAD ** -0.5
    for j in range(8):
        cols = slice(j * D_GROUP, (j + 1) * D_GROUP)
        p = lax.dot_general(hn, win_ref[cols, :], _NT, preferred_element_type=F32)
        if j in (4, 5):
            p = _rope_heads(p, cos2, sin2)
        if j in (1, 5):
            p = p * key_scale
        qkv_ref[:, cols] = p.astype(BF16)


def _const_spec(shape):
    nd = len(shape)
    return pl.BlockSpec(shape, lambda *_: (0,) * nd, pipeline_mode=pl.Buffered(1))


def _ffn_in(x, x_short, g1, wg, wu, wd, gmix, win, wgate, bi, bf, cos2, sin2, short_len):
    tm = TOKEN_TILE
    assert x_short.shape == (tm, D_MODEL)
    n_long = x.shape[0] // tm
    n_tok = x.shape[0] + tm
    pos_tiles = cos2.shape[0] // tm - 1
    row = lambda i: (i, 0)
    blk = lambda i: (i, 0, 0)
    long_row = lambda i: (jnp.minimum(i, n_long - 1), 0)
    pos = lambda i: (jnp.where(i == n_long, pos_tiles, i % pos_tiles), 0)
    return pl.pallas_call(
        functools.partial(_ffn_in_kernel, short_len=short_len),
        grid=(n_long + 1,),
        in_specs=[
            pl.BlockSpec((tm, D_MODEL), long_row),
            _const_spec((tm, D_MODEL)),
            _const_spec((1, D_MODEL)),
            _const_spec(wg.shape),
            _const_spec(wu.shape),
            _const_spec(wd.shape),
            _const_spec((1, D_MODEL)),
            _const_spec(win.shape),
            _const_spec(wgate.shape),
            _const_spec(bi.shape),
            _const_spec(bf.shape),
            pl.BlockSpec((tm, LANES), pos),
            pl.BlockSpec((tm, LANES), pos),
        ],
        out_specs=[
            pl.BlockSpec((tm, D_MODEL), row),
            pl.BlockSpec((tm, N_QKV), row),
            pl.BlockSpec((tm // LANES, LANES, LANES), blk),
            pl.BlockSpec((tm // LANES, 8, LANES), blk),
        ],
        out_shape=[
            jax.ShapeDtypeStruct((n_tok, D_MODEL), F32),
            jax.ShapeDtypeStruct((n_tok, N_QKV), BF16),
            jax.ShapeDtypeStruct((n_tok // LANES, LANES, LANES), F32),
            jax.ShapeDtypeStruct((n_tok // LANES, 8, LANES), F32),
        ],
        scratch_shapes=[pltpu.VMEM((tm, D_MODEL), F32)],
        compiler_params=pltpu.CompilerParams(
            dimension_semantics=("arbitrary",), vmem_limit_bytes=VMEM_LIMIT),
        name="ffn_in",
    )(x, x_short, g1, wg, wu, wd, gmix, win, wgate, bi, bf, cos2, sin2)


def _group_norm(h, g):
    mu = jnp.mean(h, axis=-1, keepdims=True)
    d = h - mu
    var = jnp.mean(d * d, axis=-1, keepdims=True)
    return d * lax.rsqrt(var + GN_EPS) * g


def _chunk_masks(frames):
    row_idx = lax.broadcasted_iota(jnp.int32, (frames, frames), 0)
    col_idx = lax.broadcasted_iota(jnp.int32, (frames, frames), 1)
    return row_idx >= col_idx, jnp.ones((frames, D_HEAD), BF16)


def _mixer_chunk(rows, qkv_ref, gcc, utc, s_ref, m_ref, r_ref, y_ref, consts, masks, ret_total, after_scores=None):
    gnm_ref, gnr_ref, decay_ref, cross_ref, tail_ref = consts
    causal, ones_blk = masks
    L = causal.shape[0]

    def scan_col(q, h):
        j = q * N_HEADS + h
        return jnp.broadcast_to(gcc[:, j:j + 1], (L, LANES))

    def head_slices(group, h):
        base = group * 4 * D_GROUP + h * D_HEAD
        return [qkv_ref[rows, base + j * D_GROUP:base + j * D_GROUP + D_HEAD] for j in range(4)]

    def scores(group):
        if L < LANES:
            return [lax.dot_general(*head_slices(group, h)[:2], _NT, preferred_element_type=F32)
                    for h in range(N_HEADS)]
        out = []
        zero_blk = jnp.zeros((L, D_HEAD), BF16)
        for h in range(0, N_HEADS, 2):
            base = group * 4 * D_GROUP + h * D_HEAD
            q2 = qkv_ref[rows, base:base + 2 * D_HEAD]
            k_a, k_b = head_slices(group, h)[1], head_slices(group, h + 1)[1]
            k_bd = jnp.concatenate([jnp.concatenate([k_a, zero_blk], axis=1),
                                    jnp.concatenate([zero_blk, k_b], axis=1)], axis=0)
            qk2 = lax.dot_general(q2, k_bd, _NT, preferred_element_type=F32)
            out += [qk2[:, :L], qk2[:, L:]]
        return out

    qk_m, qk_r = scores(0), scores(1)
    if after_scores is not None:
        after_scores()

    lhs_m, kw_m, gate_m, lhs_r, kt_r = [], [], [], [], []
    for h in range(N_HEADS):
        cm_b, b_b, u_b = scan_col(0, h), scan_col(1, h), scan_col(2, h)
        m_prev = m_ref[h:h + 1, :]
        big_m = jnp.maximum(cm_b, m_prev)
        a_b = jnp.exp(m_prev - big_m)
        m_t = b_b + big_m
        m_last = big_m[L - 1:L, :]
        m_ref[h:h + 1, :] = m_t[L - 1:L, :]
        w = jnp.where(causal, jnp.exp(utc[h:h + 1, :] - big_m[:, :L]), 0.0)
        q, k, _, _ = head_slices(0, h)
        lhs_m.append(jnp.concatenate([(q.astype(F32) * a_b).astype(BF16), (qk_m[h] * w).astype(BF16)], axis=1))
        kw_m.append((k.astype(F32) * jnp.exp(u_b - m_last)).astype(BF16))
        gate_m.append((a_b[L - 1:L, :], m_t))
    for h in range(N_HEADS):
        q, k, _, _ = head_slices(1, h)
        lhs_r.append(jnp.concatenate([(q.astype(F32) * cross_ref[h]).astype(BF16),
                                      (qk_r[h] * decay_ref[h]).astype(BF16)], axis=1))
        kt_r.append((k.astype(F32) * tail_ref[h]).astype(BF16))

    res_m, upd_m, res_r, upd_r = [], [], [], []
    for h in range(N_HEADS):
        v = head_slices(0, h)[2]
        v_aug = jnp.concatenate([v, ones_blk], axis=1)
        rhs = jnp.concatenate([s_ref[h].astype(BF16), v_aug], axis=0)
        res_m.append(jnp.dot(lhs_m[h], rhs, preferred_element_type=F32))
        upd_m.append(lax.dot_general(kw_m[h], v_aug, _TN, preferred_element_type=F32))
    for h in range(N_HEADS):
        v = head_slices(1, h)[2]
        rhs = jnp.concatenate([r_ref[h].astype(BF16), v], axis=0)
        res_r.append(jnp.dot(lhs_r[h], rhs, preferred_element_type=F32))
        upd_r.append(lax.dot_general(kt_r[h], v, _TN, preferred_element_type=F32))

    for h in range(N_HEADS):
        hc = slice(h * D_HEAD, (h + 1) * D_HEAD)
        a_last, m_t = gate_m[h]
        num = res_m[h][:, :D_HEAD]
        den = res_m[h][:, D_HEAD:]
        hh = num / jnp.maximum(jnp.abs(den), jnp.exp(-m_t))
        s_ref[h] = jnp.concatenate([a_last, a_last], axis=1) * s_ref[h] + upd_m[h]
        o = head_slices(0, h)[3]
        y = jax.nn.sigmoid(o.astype(F32)) * _group_norm(hh, gnm_ref[:, hc])
        y_ref[rows, hc] = y.astype(BF16)
    for h in range(N_HEADS):
        hc = slice(h * D_HEAD, (h + 1) * D_HEAD)
        r_ref[h] = ret_total[h] * r_ref[h] + upd_r[h]
        gf = head_slices(1, h)[3].astype(F32)
        y = gf * jax.nn.sigmoid(gf) * _group_norm(res_r[h], gnr_ref[:, hc])
        y_ref[rows, D_GROUP + h * D_HEAD:D_GROUP + (h + 1) * D_HEAD] = y.astype(BF16)


def _load_state(s_scr, c_ref, n_ref):
    s_scr[:, :, 0:D_HEAD] = c_ref[...]
    n_cols = jnp.concatenate([n_ref[...], jnp.zeros((LANES - 8, D_HEAD), F32)], axis=0).T
    for h in range(N_HEADS):
        s_scr[h, :, D_HEAD:2 * D_HEAD] = jnp.broadcast_to(n_cols[:, h:h + 1], (D_HEAD, LANES))


def _store_state(s_scr, c_ref, n_ref):
    c_ref[...] = s_scr[:, :, 0:D_HEAD]
    lane = lax.broadcasted_iota(jnp.int32, (D_HEAD, LANES), 1)
    n_cols = s_scr[0, :, D_HEAD:2 * D_HEAD]
    for h in range(1, N_HEADS):
        n_cols = jnp.where(lane == h, s_scr[h, :, D_HEAD:2 * D_HEAD], n_cols)
    n_ref[...] = n_cols.T[0:N_HEADS, :]


def _mixer_kernel(qkv_ref, gc_ref, ut_ref, c0_ref, n0_ref, m0_ref, r0_ref, gnm_ref, gnr_ref,
                  decay_ref, cross_ref, tail_ref,
                  y_ref, c_ref, n_ref, m_ref, r_ref, s_scr, *, frames, ret_total):
    m_ref[...] = m0_ref[...]
    r_ref[...] = r0_ref[...]
    consts = (gnm_ref, gnr_ref, decay_ref, cross_ref, tail_ref)
    masks = _chunk_masks(frames)
    streams = range(LANES // frames)
    for i in streams:
        _load_state(s_scr.at[i], c0_ref.at[i], n0_ref.at[i])
    for i in streams:
        seg = slice(i * frames, (i + 1) * frames)
        _mixer_chunk(seg, qkv_ref, gc_ref[0][seg, :], ut_ref[0][:, seg],
                     s_scr.at[i], m_ref.at[i], r_ref.at[i], y_ref, consts, masks, ret_total)
    for i in streams:
        _store_state(s_scr.at[i], c_ref.at[i], n_ref.at[i])


def _mix_out_kernel(qkv_ref, gc_ref, ut_ref, x1_ref, x1s_ref, ys_ref,
                    gnm_ref, gnr_ref, decay_ref, cross_ref, tail_ref,
                    wout_ref, g2_ref, wg_ref, wu_ref, wd_ref, gfin_ref,
                    out_ref, outs_ref, c_out, n_out, m_out, r_out,
                    y_scr, acc_ref, x2_ref, s_scr, m_scr, r_scr, *, tiles_per_stream, n_tiles, ret_total):
    j = pl.program_id(0)
    n_chunks = TOKEN_TILE // CHUNK

    @pl.when(j == 0)
    def _():
        y_scr[...] = ys_ref[...]

    @pl.when(j % tiles_per_stream == 0)
    def _():
        s_scr[...] = jnp.zeros_like(s_scr)
        m_scr[...] = jnp.zeros_like(m_scr)
        r_scr[...] = jnp.zeros_like(r_scr)

    consts = (gnm_ref, gnr_ref, decay_ref, cross_ref, tail_ref)
    masks = _chunk_masks(CHUNK)

    x1 = jnp.where(j == 0, x1s_ref[...], x1_ref[...])
    x2 = x1 + jnp.dot(y_scr[...], wout_ref[...], preferred_element_type=F32)
    x2_ref[...] = x2
    xn = _rms_norm(x2, g2_ref[...]).astype(BF16)

    pending = list(range(N_FF_CHUNKS))

    def ffn(k):
        for _ in range(min(k, len(pending))):
            _swiglu_chunk(acc_ref, xn, wg_ref, wu_ref, wd_ref, pending.pop(0))

    for c in range(n_chunks):
        _mixer_chunk(slice(c * CHUNK, (c + 1) * CHUNK), qkv_ref, gc_ref[c], ut_ref[c],
                     s_scr, m_scr, r_scr, y_scr, consts, masks, ret_total,
                     after_scores=functools.partial(ffn, 1))
        ffn(1)
    ffn(N_FF_CHUNKS)

    x3 = x2_ref[...] + 0.5 * acc_ref[...]
    out_ref[...] = _rms_norm(x3, gfin_ref[...])

    @pl.when(j == 0)
    def _():
        outs_ref[...] = out_ref[...]

    @pl.when((j % tiles_per_stream == tiles_per_stream - 1) & (j < n_tiles))
    def _():
        _store_state(s_scr, c_out.at[0], n_out.at[0])
        m_out[0] = m_scr[0:N_HEADS, :]
        r_out[0] = r_scr[...]


def _mixer_consts(frames):
    idx = np.arange(frames, dtype=np.float32)
    diff = idx[:, None] - idx[None, :]
    lg = LOG_GAMMA
    decay = np.where(diff >= 0, np.exp(lg[:, None, None] * np.maximum(diff, 0.0)), 0.0).astype(np.float32)
    rep = lambda col: np.ascontiguousarray(np.broadcast_to(col.T[:, :, None], (N_HEADS, frames, LANES))).astype(np.float32)
    cross = rep(np.exp(lg[None, :] * (idx[:, None] + 1.0)))
    tail = rep(np.exp(lg[None, :] * (frames - 1.0 - idx[:, None])))
    ret_total = tuple(float(np.exp(np.float32(lg_h) * np.float32(frames))) for lg_h in lg)
    return (jnp.asarray(decay), jnp.asarray(cross), jnp.asarray(tail)), ret_total


def _mixer(qkv, gc, ut, first_blk, c0, n0, m0, r0, gnm, gnr, frames):
    n_tok = c0.shape[0] * frames
    per_blk = LANES // frames
    (decay, cross, tail), ret_total = _mixer_consts(frames)
    row = lambda i: (i, 0)
    blk3 = lambda i: (i, 0, 0)
    blk4 = lambda i: (i, 0, 0, 0)
    in_row = lambda i: (first_blk + i, 0)
    in_blk3 = lambda i: (first_blk + i, 0, 0)
    return pl.pallas_call(
        functools.partial(_mixer_kernel, frames=frames, ret_total=ret_total),
        grid=(n_tok // LANES,),
        in_specs=[
            pl.BlockSpec((LANES, N_QKV), in_row),
            pl.BlockSpec((1, LANES, LANES), in_blk3),
            pl.BlockSpec((1, 8, LANES), in_blk3),
            pl.BlockSpec((per_blk, N_HEADS, D_HEAD, D_HEAD), blk4),
            pl.BlockSpec((per_blk, 8, D_HEAD), blk3),
            pl.BlockSpec((per_blk, N_HEADS, LANES), blk3),
            pl.BlockSpec((per_blk, N_HEADS, D_HEAD, D_HEAD), blk4),
            _const_spec((1, D_GROUP)),
            _const_spec((1, D_GROUP)),
            _const_spec(decay.shape),
            _const_spec(cross.shape),
            _const_spec(tail.shape),
        ],
        out_specs=[
            pl.BlockSpec((LANES, D_MODEL), row),
            pl.BlockSpec((per_blk, N_HEADS, D_HEAD, D_HEAD), blk4),
            pl.BlockSpec((per_blk, N_HEADS, D_HEAD), blk3),
            pl.BlockSpec((per_blk, N_HEADS, LANES), blk3),
            pl.BlockSpec((per_blk, N_HEADS, D_HEAD, D_HEAD), blk4),
        ],
        out_shape=[
            jax.ShapeDtypeStruct((n_tok, D_MODEL), BF16),
            jax.ShapeDtypeStruct(c0.shape, F32),
            jax.ShapeDtypeStruct((n0.shape[0], N_HEADS, D_HEAD), F32),
            jax.ShapeDtypeStruct(m0.shape, F32),
            jax.ShapeDtypeStruct(r0.shape, F32),
        ],
        scratch_shapes=[pltpu.VMEM((per_blk, N_HEADS, D_HEAD, 2 * D_HEAD), F32)],
        compiler_params=pltpu.CompilerParams(
            dimension_semantics=("arbitrary",), vmem_limit_bytes=VMEM_LIMIT),
        name="mixer",
    )(qkv, gc, ut, c0, n0, m0, r0, gnm, gnr, decay, cross, tail)


def _mix_out(qkv, gc, ut, x1, y_short, bsz, gnm, gnr, wout, g2, wg, wu, wd, gfin):
    tm = TOKEN_TILE
    assert y_short.shape == (tm, D_MODEL)
    n_tok = x1.shape[0] - tm
    n_tiles = n_tok // tm
    tiles_per_stream = n_tiles // bsz
    n_chunks = tm // CHUNK
    (decay, cross, tail), ret_total = _mixer_consts(CHUNK)

    mix_row = lambda j: (jnp.minimum(j, n_tiles - 1), 0)
    mix_blk = lambda j: (jnp.minimum(j, n_tiles - 1), 0, 0)
    dense_row = lambda j: (jnp.maximum(j - 1, 0), 0)
    stream4 = lambda j: (jnp.minimum(j, n_tiles - 1) // tiles_per_stream, 0, 0, 0)
    stream3 = lambda j: (jnp.minimum(j, n_tiles - 1) // tiles_per_stream, 0, 0)
    kern = functools.partial(_mix_out_kernel, tiles_per_stream=tiles_per_stream, n_tiles=n_tiles,
                             ret_total=ret_total)
    return pl.pallas_call(
        kern,
        grid=(n_tiles + 1,),
        in_specs=[
            pl.BlockSpec((tm, N_QKV), mix_row),
            pl.BlockSpec((n_chunks, LANES, LANES), mix_blk),
            pl.BlockSpec((n_chunks, 8, LANES), mix_blk),
            pl.BlockSpec((tm, D_MODEL), dense_row),
            pl.BlockSpec((tm, D_MODEL), lambda j: (n_tiles, 0), pipeline_mode=pl.Buffered(1)),
            _const_spec((tm, D_MODEL)),
            _const_spec((1, D_GROUP)),
            _const_spec((1, D_GROUP)),
            _const_spec(decay.shape),
            _const_spec(cross.shape),
            _const_spec(tail.shape),
            _const_spec(wout.shape),
            _const_spec((1, D_MODEL)),
            _const_spec(wg.shape),
            _const_spec(wu.shape),
            _const_spec(wd.shape),
            _const_spec((1, D_MODEL)),
        ],
        out_specs=[
            pl.BlockSpec((tm, D_MODEL), dense_row),
            pl.BlockSpec((tm, D_MODEL), lambda j: (0, 0)),
            pl.BlockSpec((1, N_HEADS, D_HEAD, D_HEAD), stream4),
            pl.BlockSpec((1, N_HEADS, D_HEAD), stream3),
            pl.BlockSpec((1, N_HEADS, LANES), stream3),
            pl.BlockSpec((1, N_HEADS, D_HEAD, D_HEAD), stream4),
        ],
        out_shape=[
            jax.ShapeDtypeStruct((n_tok, D_MODEL), F32),
            jax.ShapeDtypeStruct((tm, D_MODEL), F32),
            jax.ShapeDtypeStruct((bsz, N_HEADS, D_HEAD, D_HEAD), F32),
            jax.ShapeDtypeStruct((bsz, N_HEADS, D_HEAD), F32),
            jax.ShapeDtypeStruct((bsz, N_HEADS, LANES), F32),
            jax.ShapeDtypeStruct((bsz, N_HEADS, D_HEAD, D_HEAD), F32),
        ],
        scratch_shapes=[
            pltpu.VMEM((tm, D_MODEL), BF16),
            pltpu.VMEM((tm, D_MODEL), F32),
            pltpu.VMEM((tm, D_MODEL), F32),
            pltpu.VMEM((N_HEADS, D_HEAD, 2 * D_HEAD), F32),
            pltpu.VMEM((8, LANES), F32),
            pltpu.VMEM((N_HEADS, D_HEAD, D_HEAD), F32),
        ],
        compiler_params=pltpu.CompilerParams(
            dimension_semantics=("arbitrary",), vmem_limit_bytes=VMEM_LIMIT),
        name="mix_out",
    )(qkv, gc, ut, x1, x1, y_short, gnm, gnr, decay, cross, tail, wout, g2, wg, wu, wd, gfin)


def _rope_tables(pos):
    half = D_HEAD // 2
    inv = ROPE_BASE ** (-np.arange(half, dtype=np.float64) / half)
    ang = np.asarray(pos, np.float64)[:, None] * inv[None, :]
    cos, sin = np.cos(ang), np.sin(ang)
    return (jnp.asarray(np.concatenate([cos, cos], axis=-1), F32),
            jnp.asarray(np.concatenate([-sin, sin], axis=-1), F32))


def kernel(x_prompt, x_sample, state_mlstm_C, state_mlstm_n, state_mlstm_m, state_ret_R,
           g_ffn1, w_ffn1_gate, w_ffn1_up, w_ffn1_down, g_mix, w_in, b_gates, g_gn_mlstm, g_gn_ret,
           w_out, g_ffn2, w_ffn2_gate, w_ffn2_up, w_ffn2_down, g_final):
    depth = g_ffn1.shape[0]
    bp, sp, _ = x_prompt.shape
    bs, ss, _ = x_sample.shape
    assert depth == 1, "the final norm is fused into the layer's last kernel"
    assert ss < CHUNK and LANES % ss == 0 and bs * ss == TOKEN_TILE and sp % TOKEN_TILE == 0
    cos2, sin2 = _rope_tables(np.concatenate([np.arange(sp), np.tile(PAST_LEN + np.arange(ss), bs)]))

    new_p, new_s = [], []
    for l in range(depth):
        cast = lambda w: w.astype(BF16)
        w_in_t = w_in[l].T
        wi_t = w_in_t[N_QKV:N_QKV + N_HEADS]
        wf_t = w_in_t[N_QKV + N_HEADS:]
        wgate = jnp.concatenate([wi_t, wi_t, wf_t, wf_t], axis=0)
        rows8 = lambda b: jnp.broadcast_to(jnp.tile(b.astype(F32), 2)[:, None], (8, LANES))
        in_w = (g_ffn1[l][None, :], cast(w_ffn1_gate[l]), cast(w_ffn1_up[l]), cast(w_ffn1_down[l]),
                g_mix[l][None, :], cast(w_in_t), wgate,
                rows8(b_gates[l][:N_HEADS]), rows8(b_gates[l][N_HEADS:]))
        gnm, gnr = g_gn_mlstm[l][None, :].astype(F32), g_gn_ret[l][None, :].astype(F32)
        out_w = (cast(w_out[l]), g_ffn2[l][None, :], cast(w_ffn2_gate[l]), cast(w_ffn2_up[l]),
                 cast(w_ffn2_down[l]), g_final[None, :])

        x1, qkv, gc, ut = _ffn_in(x_prompt.reshape(bp * sp, D_MODEL), x_sample.reshape(bs * ss, D_MODEL),
                                  *in_w, cos2, sin2, ss)
        m0 = jnp.broadcast_to(state_mlstm_m[l].astype(F32)[..., None], (bs, N_HEADS, LANES))
        n0 = jnp.pad(state_mlstm_n[l].astype(F32), ((0, 0), (0, 8 - N_HEADS), (0, 0)))
        ys, c_s, n_s, m_s, r_s = _mixer(qkv, gc, ut, bp * sp // LANES, state_mlstm_C[l].astype(F32), n0, m0,
                                        state_ret_R[l].astype(F32), gnm, gnr, ss)
        out_p, out_s, c_p, n_p, m_p, r_p = _mix_out(qkv, gc, ut, x1, ys, bp, gnm, gnr, *out_w)
        new_p.append((c_p, n_p, m_p[..., 0], r_p))
        new_s.append((c_s, n_s, m_s[..., 0], r_s))
    hp, hs = out_p.reshape(bp, sp, D_MODEL), out_s.reshape(bs, ss, D_MODEL)
    stack = lambda sts, i: jnp.stack([s[i] for s in sts])
    return (hp, hs,
            stack(new_p, 0), stack(new_p, 1), stack(new_p, 2), stack(new_p, 3),
            stack(new_s, 0), stack(new_s, 1), stack(new_s, 2), stack(new_s, 3))
```

```python
import functools

import numpy as np
import jax
import jax.numpy as jnp
from jax import lax
from jax.experimental import pallas as pl
from jax.experimental.pallas import tpu as pltpu

F32 = jnp.float32
BF16 = jnp.bfloat16

D_MODEL = 1024
N_HEADS = 4
D_HEAD = 128
D_GROUP = N_HEADS * D_HEAD
D_FF = 2816
PAST_LEN = 4096
ROPE_BASE = 10000.0
EPS = 1e-6
GN_EPS = 1e-5
LOG_GAMMA = np.log(1.0 - 2.0 ** (-5.0 - np.arange(N_HEADS))).astype(np.float32)

LANES = 128
MXU_DIM = 256
FF_CHUNK = MXU_DIM
N_FF_CHUNKS = D_FF // FF_CHUNK
CHUNK = 128
TOKEN_TILE = 512
N_QKV = 8 * D_GROUP
GATE_ROWS = 16
SCAN_COLS = 12
VMEM_LIMIT = 58 * 1024 * 1024

_NT = (((1,), (1,)), ((), ()))
_TN = (((0,), (0,)), ((), ()))


def _rms_norm(x, g):
    ms = jnp.mean(x * x, axis=-1, keepdims=True)
    return x * lax.rsqrt(ms + EPS) * g


def _swiglu_chunk(acc_ref, xn, wg_ref, wu_ref, wd_ref, c):
    cols = slice(c * FF_CHUNK, (c + 1) * FF_CHUNK)
    g = jnp.dot(xn, wg_ref[:, cols], preferred_element_type=F32)
    u = jnp.dot(xn, wu_ref[:, cols], preferred_element_type=F32)
    a = (g * jax.nn.sigmoid(g) * u).astype(BF16)
    d = jnp.dot(a, wd_ref[cols, :], preferred_element_type=F32)
    if c == 0:
        acc_ref[...] = d
    else:
        acc_ref[...] += d


def _swiglu_into(acc_ref, xn, wg_ref, wu_ref, wd_ref):
    for c in range(N_FF_CHUNKS):
        _swiglu_chunk(acc_ref, xn, wg_ref, wu_ref, wd_ref, c)


def _rope_heads(p, cos2, sin2):
    outs = []
    for h in range(N_HEADS):
        xh = p[:, h * D_HEAD:(h + 1) * D_HEAD]
        outs.append(xh * cos2 + pltpu.roll(xh, D_HEAD // 2, axis=1) * sin2)
    return jnp.concatenate(outs, axis=1)


def _log_sigmoid(x):
    return jnp.minimum(x, 0.0) - jnp.log1p(jnp.exp(-jnp.abs(x)))


def _scan_lanes(x, op, fill, pos):
    d = 1
    while d < LANES:
        x = op(x, jnp.where(pos >= d, pltpu.roll(x, d, axis=1), fill))
        d *= 2
    return x


def _gate_scans(gt, bi, bf, seg_len, gc_ref, ut_ref):
    pad_rows = jnp.zeros((LANES - 16, LANES), F32)
    lane = lax.broadcasted_iota(jnp.int32, (8, LANES), 1)
    sub = lax.broadcasted_iota(jnp.int32, (8, LANES), 0)
    pos = lane & (seg_len - 1)
    for c in range(gt.shape[1] // LANES):
        lanes = slice(c * LANES, (c + 1) * LANES)
        ig = gt[0:8, lanes] + bi
        lf = _log_sigmoid(gt[8:16, lanes] + bf)
        b = _scan_lanes(lf, jnp.add, 0.0, pos)
        u = ig - b
        cm = _scan_lanes(u, jnp.maximum, -jnp.inf, pos)
        packed = jnp.concatenate([jnp.where(sub < N_HEADS, cm, b), jnp.where(sub < N_HEADS, u, 0.0)], axis=0)
        gc_ref[c] = jnp.concatenate([packed, pad_rows], axis=0).T
        ut_ref[c] = u


def _ffn_in_kernel(x_ref, xs_ref, g1_ref, wg_ref, wu_ref, wd_ref, gmix_ref, win_ref, wgate_ref, bi_ref, bf_ref,
                   cos_ref, sin_ref, x1_ref, qkv_ref, gc_ref, ut_ref, acc_ref, *, short_len):
    is_short = pl.program_id(0) == pl.num_programs(0) - 1
    seg_len = jnp.where(is_short, short_len, CHUNK)
    x = jnp.where(is_short, xs_ref[...], x_ref[...])
    xn = _rms_norm(x, g1_ref[...]).astype(BF16)
    _swiglu_into(acc_ref, xn, wg_ref, wu_ref, wd_ref)
    x1 = x + 0.5 * acc_ref[...]
    x1_ref[...] = x1
    hn = _rms_norm(x1, gmix_ref[...]).astype(BF16)
    gt = lax.dot_general(wgate_ref[...].astype(BF16), hn, _NT, preferred_element_type=F32)
    _gate_scans(gt, bi_ref[...], bf_ref[...], seg_len, gc_ref, ut_ref)
    cos2 = cos_ref[...]
    sin2 = sin_ref[...]
    key_scale = D_HEAD ** -0.5
    for j in range(8):
        cols = slice(j * D_GROUP, (j + 1) * D_GROUP)
        p = lax.dot_general(hn, win_ref[cols, :], _NT, preferred_element_type=F32)
        if j in (4, 5):
            p = _rope_heads(p, cos2, sin2)
        if j in (1, 5):
            p = p * key_scale
        qkv_ref[:, cols] = p.astype(BF16)


def _const_spec(shape):
    nd = len(shape)
    return pl.BlockSpec(shape, lambda *_: (0,) * nd, pipeline_mode=pl.Buffered(1))


def _ffn_in(x, x_short, g1, wg, wu, wd, gmix, win, wgate, bi, bf, cos2, sin2, short_len):
    tm = TOKEN_TILE
    assert x_short.shape == (tm, D_MODEL)
    n_long = x.shape[0] // tm
    n_tok = x.shape[0] + tm
    pos_tiles = cos2.shape[0] // tm - 1
    row = lambda i: (i, 0)
    blk = lambda i: (i, 0, 0)
    long_row = lambda i: (jnp.minimum(i, n_long - 1), 0)
    pos = lambda i: (jnp.where(i == n_long, pos_tiles, i % pos_tiles), 0)
    return pl.pallas_call(
        functools.partial(_ffn_in_kernel, short_len=short_len),
        grid=(n_long + 1,),
        in_specs=[
            pl.BlockSpec((tm, D_MODEL), long_row),
            _const_spec((tm, D_MODEL)),
            _const_spec((1, D_MODEL)),
            _const_spec(wg.shape),
            _const_spec(wu.shape),
            _const_spec(wd.shape),
            _const_spec((1, D_MODEL)),
            _const_spec(win.shape),
            _const_spec(wgate.shape),
            _const_spec(bi.shape),
            _const_spec(bf.shape),
            pl.BlockSpec((tm, LANES), pos),
            pl.BlockSpec((tm, LANES), pos),
        ],
        out_specs=[
            pl.BlockSpec((tm, D_MODEL), row),
            pl.BlockSpec((tm, N_QKV), row),
            pl.BlockSpec((tm // LANES, LANES, LANES), blk),
            pl.BlockSpec((tm // LANES, 8, LANES), blk),
        ],
        out_shape=[
            jax.ShapeDtypeStruct((n_tok, D_MODEL), F32),
            jax.ShapeDtypeStruct((n_tok, N_QKV), BF16),
            jax.ShapeDtypeStruct((n_tok // LANES, LANES, LANES), F32),
            jax.ShapeDtypeStruct((n_tok // LANES, 8, LANES), F32),
        ],
        scratch_shapes=[pltpu.VMEM((tm, D_MODEL), F32)],
        compiler_params=pltpu.CompilerParams(
            dimension_semantics=("arbitrary",), vmem_limit_bytes=VMEM_LIMIT),
        name="ffn_in",
    )(x, x_short, g1, wg, wu, wd, gmix, win, wgate, bi, bf, cos2, sin2)


def _group_norm(h, g):
    mu = jnp.mean(h, axis=-1, keepdims=True)
    d = h - mu
    var = jnp.mean(d * d, axis=-1, keepdims=True)
    return d * lax.rsqrt(var + GN_EPS) * g


def _chunk_masks(frames):
    row_idx = lax.broadcasted_iota(jnp.int32, (frames, frames), 0)
    col_idx = lax.broadcasted_iota(jnp.int32, (frames, frames), 1)
    return row_idx >= col_idx, jnp.ones((frames, D_HEAD), BF16)


def _mixer_chunk(rows, qkv_ref, gcc, utc, s_ref, m_ref, r_ref, y_ref, consts, masks, ret_total, after_scores=None):
    gnm_ref, gnr_ref, decay_ref, cross_ref, tail_ref = consts
    causal, ones_blk = masks
    L = causal.shape[0]

    def scan_col(q, h):
        j = q * N_HEADS + h
        return jnp.broadcast_to(gcc[:, j:j + 1], (L, LANES))

    def head_slices(group, h):
        base = group * 4 * D_GROUP + h * D_HEAD
        return [qkv_ref[rows, base + j * D_GROUP:base + j * D_GROUP + D_HEAD] for j in range(4)]

    def scores(group):
        if L < LANES:
            return [lax.dot_general(*head_slices(group, h)[:2], _NT, preferred_element_type=F32)
                    for h in range(N_HEADS)]
        out = []
        zero_blk = jnp.zeros((L, D_HEAD), BF16)
        for h in range(0, N_HEADS, 2):
            base = group * 4 * D_GROUP + h * D_HEAD
            q2 = qkv_ref[rows, base:base + 2 * D_HEAD]
            k_a, k_b = head_slices(group, h)[1], head_slices(group, h + 1)[1]
            k_bd = jnp.concatenate([jnp.concatenate([k_a, zero_blk], axis=1),
                                    jnp.concatenate([zero_blk, k_b], axis=1)], axis=0)
            qk2 = lax.dot_general(q2, k_bd, _NT, preferred_element_type=F32)
            out += [qk2[:, :L], qk2[:, L:]]
        return out

    qk_m, qk_r = scores(0), scores(1)
    if after_scores is not None:
        after_scores()

    lhs_m, kw_m, gate_m, lhs_r, kt_r = [], [], [], [], []
    for h in range(N_HEADS):
        cm_b, b_b, u_b = scan_col(0, h), scan_col(1, h), scan_col(2, h)
        m_prev = m_ref[h:h + 1, :]
        big_m = jnp.maximum(cm_b, m_prev)
        a_b = jnp.exp(m_prev - big_m)
        m_t = b_b + big_m
        m_last = big_m[L - 1:L, :]
        m_ref[h:h + 1, :] = m_t[L - 1:L, :]
        w = jnp.where(causal, jnp.exp(utc[h:h + 1, :] - big_m[:, :L]), 0.0)
        q, k, _, _ = head_slices(0, h)
        lhs_m.append(jnp.concatenate([(q.astype(F32) * a_b).astype(BF16), (qk_m[h] * w).astype(BF16)], axis=1))
        kw_m.append((k.astype(F32) * jnp.exp(u_b - m_last)).astype(BF16))
        gate_m.append((a_b[L - 1:L, :], m_t))
    for h in range(N_HEADS):
        q, k, _, _ = head_slices(1, h)
        lhs_r.append(jnp.concatenate([(q.astype(F32) * cross_ref[h]).astype(BF16),
                                      (qk_r[h] * decay_ref[h]).astype(BF16)], axis=1))
        kt_r.append((k.astype(F32) * tail_ref[h]).astype(BF16))

    res_m, upd_m, res_r, upd_r = [], [], [], []
    for h in range(N_HEADS):
        v = head_slices(0, h)[2]
        v_aug = jnp.concatenate([v, ones_blk], axis=1)
        rhs = jnp.concatenate([s_ref[h].astype(BF16), v_aug], axis=0)
        res_m.append(jnp.dot(lhs_m[h], rhs, preferred_element_type=F32))
        upd_m.append(lax.dot_general(kw_m[h], v_aug, _TN, preferred_element_type=F32))
    for h in range(N_HEADS):
        v = head_slices(1, h)[2]
        rhs = jnp.concatenate([r_ref[h].astype(BF16), v], axis=0)
        res_r.append(jnp.dot(lhs_r[h], rhs, preferred_element_type=F32))
        upd_r.append(lax.dot_general(kt_r[h], v, _TN, preferred_element_type=F32))

    for h in range(N_HEADS):
        hc = slice(h * D_HEAD, (h + 1) * D_HEAD)
        a_last, m_t = gate_m[h]
        num = res_m[h][:, :D_HEAD]
        den = res_m[h][:, D_HEAD:]
        hh = num / jnp.maximum(jnp.abs(den), jnp.exp(-m_t))
        s_ref[h] = jnp.concatenate([a_last, a_last], axis=1) * s_ref[h] + upd_m[h]
        o = head_slices(0, h)[3]
        y = jax.nn.sigmoid(o.astype(F32)) * _group_norm(hh, gnm_ref[:, hc])
        y_ref[rows, hc] = y.astype(BF16)
    for h in range(N_HEADS):
        hc = slice(h * D_HEAD, (h + 1) * D_HEAD)
        r_ref[h] = ret_total[h] * r_ref[h] + upd_r[h]
        gf = head_slices(1, h)[3].astype(F32)
        y = gf * jax.nn.sigmoid(gf) * _group_norm(res_r[h], gnr_ref[:, hc])
        y_ref[rows, D_GROUP + h * D_HEAD:D_GROUP + (h + 1) * D_HEAD] = y.astype(BF16)


def _load_state(s_scr, c_ref, n_ref):
    s_scr[:, :, 0:D_HEAD] = c_ref[...]
    n_cols = jnp.concatenate([n_ref[...], jnp.zeros((LANES - 8, D_HEAD), F32)], axis=0).T
    for h in range(N_HEADS):
        s_scr[h, :, D_HEAD:2 * D_HEAD] = jnp.broadcast_to(n_cols[:, h:h + 1], (D_HEAD, LANES))


def _store_state(s_scr, c_ref, n_ref):
    c_ref[...] = s_scr[:, :, 0:D_HEAD]
    lane = lax.broadcasted_iota(jnp.int32, (D_HEAD, LANES), 1)
    n_cols = s_scr[0, :, D_HEAD:2 * D_HEAD]
    for h in range(1, N_HEADS):
        n_cols = jnp.where(lane == h, s_scr[h, :, D_HEAD:2 * D_HEAD], n_cols)
    n_ref[...] = n_cols.T[0:N_HEADS, :]


def _mixer_kernel(qkv_ref, gc_ref, ut_ref, c0_ref, n0_ref, m0_ref, r0_ref, gnm_ref, gnr_ref,
                  decay_ref, cross_ref, tail_ref,
                  y_ref, c_ref, n_ref, m_ref, r_ref, s_scr, *, frames, ret_total):
    m_ref[...] = m0_ref[...]
    r_ref[...] = r0_ref[...]
    consts = (gnm_ref, gnr_ref, decay_ref, cross_ref, tail_ref)
    masks = _chunk_masks(frames)
    streams = range(LANES // frames)
    for i in streams:
        _load_state(s_scr.at[i], c0_ref.at[i], n0_ref.at[i])
    for i in streams:
        seg = slice(i * frames, (i + 1) * frames)
        _mixer_chunk(seg, qkv_ref, gc_ref[0][seg, :], ut_ref[0][:, seg],
                     s_scr.at[i], m_ref.at[i], r_ref.at[i], y_ref, consts, masks, ret_total)
    for i in streams:
        _store_state(s_scr.at[i], c_ref.at[i], n_ref.at[i])


def _mix_out_kernel(qkv_ref, gc_ref, ut_ref, x1_ref, ys_ref,
                    gnm_ref, gnr_ref, decay_ref, cross_ref, tail_ref,
                    wout_ref, g2_ref, wg_ref, wu_ref, wd_ref, gfin_ref,
                    out_ref, outs_ref, c_out, n_out, m_out, r_out,
                    y_scr, acc_ref, x2_ref, s_scr, m_scr, r_scr, *, tiles_per_stream, n_tiles, ret_total):
    j = pl.program_id(0)
    n_chunks = TOKEN_TILE // CHUNK

    @pl.when(j == 0)
    def _():
        y_scr[...] = ys_ref[...]

    @pl.when(j % tiles_per_stream == 0)
    def _():
        s_scr[...] = jnp.zeros_like(s_scr)
        m_scr[...] = jnp.zeros_like(m_scr)
        r_scr[...] = jnp.zeros_like(r_scr)

    consts = (gnm_ref, gnr_ref, decay_ref, cross_ref, tail_ref)
    masks = _chunk_masks(CHUNK)

    x2 = x1_ref[...] + jnp.dot(y_scr[...], wout_ref[...], preferred_element_type=F32)
    x2_ref[...] = x2
    xn = _rms_norm(x2, g2_ref[...]).astype(BF16)

    pending = list(range(N_FF_CHUNKS))

    def ffn(k):
        for _ in range(min(k, len(pending))):
            _swiglu_chunk(acc_ref, xn, wg_ref, wu_ref, wd_ref, pending.pop(0))

    for c in range(n_chunks):
        _mixer_chunk(slice(c * CHUNK, (c + 1) * CHUNK), qkv_ref, gc_ref[c], ut_ref[c],
                     s_scr, m_scr, r_scr, y_scr, consts, masks, ret_total,
                     after_scores=functools.partial(ffn, 1))
        ffn(1)
    ffn(N_FF_CHUNKS)

    x3 = x2_ref[...] + 0.5 * acc_ref[...]
    out_ref[...] = _rms_norm(x3, gfin_ref[...])

    @pl.when(j == 0)
    def _():
        outs_ref[...] = out_ref[...]

    @pl.when((j % tiles_per_stream == tiles_per_stream - 1) & (j < n_tiles))
    def _():
        _store_state(s_scr, c_out.at[0], n_out.at[0])
        m_out[0] = m_scr[0:N_HEADS, :]
        r_out[0] = r_scr[...]


def _mixer_consts(frames):
    idx = np.arange(frames, dtype=np.float32)
    diff = idx[:, None] - idx[None, :]
    lg = LOG_GAMMA
    decay = np.where(diff >= 0, np.exp(lg[:, None, None] * np.maximum(diff, 0.0)), 0.0).astype(np.float32)
    rep = lambda col: np.ascontiguousarray(np.broadcast_to(col.T[:, :, None], (N_HEADS, frames, LANES))).astype(np.float32)
    cross = rep(np.exp(lg[None, :] * (idx[:, None] + 1.0)))
    tail = rep(np.exp(lg[None, :] * (frames - 1.0 - idx[:, None])))
    ret_total = tuple(float(np.exp(np.float32(lg_h) * np.float32(frames))) for lg_h in lg)
    return (jnp.asarray(decay), jnp.asarray(cross), jnp.asarray(tail)), ret_total


def _mixer(qkv, gc, ut, first_blk, c0, n0, m0, r0, gnm, gnr, frames):
    n_tok = c0.shape[0] * frames
    per_blk = LANES // frames
    (decay, cross, tail), ret_total = _mixer_consts(frames)
    row = lambda i: (i, 0)
    blk3 = lambda i: (i, 0, 0)
    blk4 = lambda i: (i, 0, 0, 0)
    in_row = lambda i: (first_blk + i, 0)
    in_blk3 = lambda i: (first_blk + i, 0, 0)
    return pl.pallas_call(
        functools.partial(_mixer_kernel, frames=frames, ret_total=ret_total),
        grid=(n_tok // LANES,),
        in_specs=[
            pl.BlockSpec((LANES, N_QKV), in_row),
            pl.BlockSpec((1, LANES, LANES), in_blk3),
            pl.BlockSpec((1, 8, LANES), in_blk3),
            pl.BlockSpec((per_blk, N_HEADS, D_HEAD, D_HEAD), blk4),
            pl.BlockSpec((per_blk, 8, D_HEAD), blk3),
            pl.BlockSpec((per_blk, N_HEADS, LANES), blk3),
            pl.BlockSpec((per_blk, N_HEADS, D_HEAD, D_HEAD), blk4),
            _const_spec((1, D_GROUP)),
            _const_spec((1, D_GROUP)),
            _const_spec(decay.shape),
            _const_spec(cross.shape),
            _const_spec(tail.shape),
        ],
        out_specs=[
            pl.BlockSpec((LANES, D_MODEL), row),
            pl.BlockSpec((per_blk, N_HEADS, D_HEAD, D_HEAD), blk4),
            pl.BlockSpec((per_blk, N_HEADS, D_HEAD), blk3),
            pl.BlockSpec((per_blk, N_HEADS, LANES), blk3),
            pl.BlockSpec((per_blk, N_HEADS, D_HEAD, D_HEAD), blk4),
        ],
        out_shape=[
            jax.ShapeDtypeStruct((n_tok, D_MODEL), BF16),
            jax.ShapeDtypeStruct(c0.shape, F32),
            jax.ShapeDtypeStruct((n0.shape[0], N_HEADS, D_HEAD), F32),
            jax.ShapeDtypeStruct(m0.shape, F32),
            jax.ShapeDtypeStruct(r0.shape, F32),
        ],
        scratch_shapes=[pltpu.VMEM((per_blk, N_HEADS, D_HEAD, 2 * D_HEAD), F32)],
        compiler_params=pltpu.CompilerParams(
            dimension_semantics=("arbitrary",), vmem_limit_bytes=VMEM_LIMIT),
        name="mixer",
    )(qkv, gc, ut, c0, n0, m0, r0, gnm, gnr, decay, cross, tail)


def _mix_out(qkv, gc, ut, x1, y_short, bsz, gnm, gnr, wout, g2, wg, wu, wd, gfin):
    tm = TOKEN_TILE
    assert y_short.shape == (tm, D_MODEL)
    n_tok = x1.shape[0] - tm
    n_tiles = n_tok // tm
    tiles_per_stream = n_tiles // bsz
    n_chunks = tm // CHUNK
    (decay, cross, tail), ret_total = _mixer_consts(CHUNK)

    mix_row = lambda j: (jnp.minimum(j, n_tiles - 1), 0)
    mix_blk = lambda j: (jnp.minimum(j, n_tiles - 1), 0, 0)
    dense_row = lambda j: (jnp.maximum(j - 1, 0), 0)
    stream4 = lambda j: (jnp.minimum(j, n_tiles - 1) // tiles_per_stream, 0, 0, 0)
    stream3 = lambda j: (jnp.minimum(j, n_tiles - 1) // tiles_per_stream, 0, 0)
    kern = functools.partial(_mix_out_kernel, tiles_per_stream=tiles_per_stream, n_tiles=n_tiles,
                             ret_total=ret_total)
    return pl.pallas_call(
        kern,
        grid=(n_tiles + 1,),
        in_specs=[
            pl.BlockSpec((tm, N_QKV), mix_row),
            pl.BlockSpec((n_chunks, LANES, LANES), mix_blk),
            pl.BlockSpec((n_chunks, 8, LANES), mix_blk),
            pl.BlockSpec((tm, D_MODEL), lambda j: (jnp.where(j == 0, n_tiles, j - 1), 0)),
            _const_spec((tm, D_MODEL)),
            _const_spec((1, D_GROUP)),
            _const_spec((1, D_GROUP)),
            _const_spec(decay.shape),
            _const_spec(cross.shape),
            _const_spec(tail.shape),
            _const_spec(wout.shape),
            _const_spec((1, D_MODEL)),
            _const_spec(wg.shape),
            _const_spec(wu.shape),
            _const_spec(wd.shape),
            _const_spec((1, D_MODEL)),
        ],
        out_specs=[
            pl.BlockSpec((tm, D_MODEL), dense_row),
            pl.BlockSpec((tm, D_MODEL), lambda j: (0, 0)),
            pl.BlockSpec((1, N_HEADS, D_HEAD, D_HEAD), stream4),
            pl.BlockSpec((1, N_HEADS, D_HEAD), stream3),
            pl.BlockSpec((1, N_HEADS, LANES), stream3),
            pl.BlockSpec((1, N_HEADS, D_HEAD, D_HEAD), stream4),
        ],
        out_shape=[
            jax.ShapeDtypeStruct((n_tok, D_MODEL), F32),
            jax.ShapeDtypeStruct((tm, D_MODEL), F32),
            jax.ShapeDtypeStruct((bsz, N_HEADS, D_HEAD, D_HEAD), F32),
            jax.ShapeDtypeStruct((bsz, N_HEADS, D_HEAD), F32),
            jax.ShapeDtypeStruct((bsz, N_HEADS, LANES), F32),
            jax.ShapeDtypeStruct((bsz, N_HEADS, D_HEAD, D_HEAD), F32),
        ],
        scratch_shapes=[
            pltpu.VMEM((tm, D_MODEL), BF16),
            pltpu.VMEM((tm, D_MODEL), F32),
            pltpu.VMEM((tm, D_MODEL), F32),
            pltpu.VMEM((N_HEADS, D_HEAD, 2 * D_HEAD), F32),
            pltpu.VMEM((8, LANES), F32),
            pltpu.VMEM((N_HEADS, D_HEAD, D_HEAD), F32),
        ],
        compiler_params=pltpu.CompilerParams(
            dimension_semantics=("arbitrary",), vmem_limit_bytes=VMEM_LIMIT),
        name="mix_out",
    )(qkv, gc, ut, x1, y_short, gnm, gnr, decay, cross, tail, wout, g2, wg, wu, wd, gfin)


def _rope_tables(pos):
    half = D_HEAD // 2
    inv = ROPE_BASE ** (-np.arange(half, dtype=np.float64) / half)
    ang = np.asarray(pos, np.float64)[:, None] * inv[None, :]
    cos, sin = np.cos(ang), np.sin(ang)
    return (jnp.asarray(np.concatenate([cos, cos], axis=-1), F32),
            jnp.asarray(np.concatenate([-sin, sin], axis=-1), F32))


def kernel(x_prompt, x_sample, state_mlstm_C, state_mlstm_n, state_mlstm_m, state_ret_R,
           g_ffn1, w_ffn1_gate, w_ffn1_up, w_ffn1_down, g_mix, w_in, b_gates, g_gn_mlstm, g_gn_ret,
           w_out, g_ffn2, w_ffn2_gate, w_ffn2_up, w_ffn2_down, g_final):
    depth = g_ffn1.shape[0]
    bp, sp, _ = x_prompt.shape
    bs, ss, _ = x_sample.shape
    assert depth == 1, "the final norm is fused into the layer's last kernel"
    assert ss < CHUNK and LANES % ss == 0 and bs * ss == TOKEN_TILE and sp % TOKEN_TILE == 0
    cos2, sin2 = _rope_tables(np.concatenate([np.arange(sp), np.tile(PAST_LEN + np.arange(ss), bs)]))

    new_p, new_s = [], []
    for l in range(depth):
        cast = lambda w: w.astype(BF16)
        w_in_t = w_in[l].T
        wi_t = w_in_t[N_QKV:N_QKV + N_HEADS]
        wf_t = w_in_t[N_QKV + N_HEADS:]
        wgate = jnp.concatenate([wi_t, wi_t, wf_t, wf_t], axis=0)
        rows8 = lambda b: jnp.broadcast_to(jnp.tile(b.astype(F32), 2)[:, None], (8, LANES))
        in_w = (g_ffn1[l][None, :], cast(w_ffn1_gate[l]), cast(w_ffn1_up[l]), cast(w_ffn1_down[l]),
                g_mix[l][None, :], cast(w_in_t), wgate,
                rows8(b_gates[l][:N_HEADS]), rows8(b_gates[l][N_HEADS:]))
        gnm, gnr = g_gn_mlstm[l][None, :].astype(F32), g_gn_ret[l][None, :].astype(F32)
        out_w = (cast(w_out[l]), g_ffn2[l][None, :], cast(w_ffn2_gate[l]), cast(w_ffn2_up[l]),
                 cast(w_ffn2_down[l]), g_final[None, :])

        x1, qkv, gc, ut = _ffn_in(x_prompt.reshape(bp * sp, D_MODEL), x_sample.reshape(bs * ss, D_MODEL),
                                  *in_w, cos2, sin2, ss)
        m0 = jnp.broadcast_to(state_mlstm_m[l].astype(F32)[..., None], (bs, N_HEADS, LANES))
        n0 = jnp.pad(state_mlstm_n[l].astype(F32), ((0, 0), (0, 8 - N_HEADS), (0, 0)))
        ys, c_s, n_s, m_s, r_s = _mixer(qkv, gc, ut, bp * sp // LANES, state_mlstm_C[l].astype(F32), n0, m0,
                                        state_ret_R[l].astype(F32), gnm, gnr, ss)
        out_p, out_s, c_p, n_p, m_p, r_p = _mix_out(qkv, gc, ut, x1, ys, bp, gnm, gnr, *out_w)
        new_p.append((c_p, n_p, m_p[..., 0], r_p))
        new_s.append((c_s, n_s, m_s[..., 0], r_s))
    hp, hs = out_p.reshape(bp, sp, D_MODEL), out_s.reshape(bs, ss, D_MODEL)
    stack = lambda sts, i: jnp.stack([s[i] for s in sts])
    return (hp, hs,
            stack(new_p, 0), stack(new_p, 1), stack(new_p, 2), stack(new_p, 3),
            stack(new_s, 0), stack(new_s, 1), stack(new_s, 2), stack(new_s, 3))
```

```python
import functools

import numpy as np
import jax
import jax.numpy as jnp
from jax import lax
from jax.experimental import pallas as pl
from jax.experimental.pallas import tpu as pltpu

F32 = jnp.float32
BF16 = jnp.bfloat16

D_MODEL = 1024
N_HEADS = 4
D_HEAD = 128
D_GROUP = N_HEADS * D_HEAD
D_FF = 2816
PAST_LEN = 4096
ROPE_BASE = 10000.0
EPS = 1e-6
GN_EPS = 1e-5
LOG_GAMMA = np.log(1.0 - 2.0 ** (-5.0 - np.arange(N_HEADS))).astype(np.float32)

LANES = 128
MXU_DIM = 256
FF_CHUNK = MXU_DIM
N_FF_CHUNKS = D_FF // FF_CHUNK
CHUNK = 128
TOKEN_TILE = 512
N_QKV = 8 * D_GROUP
GATE_ROWS = 16
SCAN_COLS = 12
VMEM_LIMIT = 58 * 1024 * 1024

_NT = (((1,), (1,)), ((), ()))
_TN = (((0,), (0,)), ((), ()))


def _rms_norm(x, g):
    ms = jnp.mean(x * x, axis=-1, keepdims=True)
    return x * lax.rsqrt(ms + EPS) * g


def _swiglu_chunk(acc_ref, xn, wg_ref, wu_ref, wd_ref, c):
    cols = slice(c * FF_CHUNK, (c + 1) * FF_CHUNK)
    g = jnp.dot(xn, wg_ref[:, cols], preferred_element_type=F32)
    u = jnp.dot(xn, wu_ref[:, cols], preferred_element_type=F32)
    a = (g * jax.nn.sigmoid(g) * u).astype(BF16)
    d = jnp.dot(a, wd_ref[cols, :], preferred_element_type=F32)
    if c == 0:
        acc_ref[...] = d
    else:
        acc_ref[...] += d


def _swiglu_into(acc_ref, xn, wg_ref, wu_ref, wd_ref):
    for c in range(N_FF_CHUNKS):
        _swiglu_chunk(acc_ref, xn, wg_ref, wu_ref, wd_ref, c)


def _rope_heads(p, cos2, sin2):
    outs = []
    for h in range(N_HEADS):
        xh = p[:, h * D_HEAD:(h + 1) * D_HEAD]
        outs.append(xh * cos2 + pltpu.roll(xh, D_HEAD // 2, axis=1) * sin2)
    return jnp.concatenate(outs, axis=1)


def _log_sigmoid(x):
    return jnp.minimum(x, 0.0) - jnp.log1p(jnp.exp(-jnp.abs(x)))


def _scan_lanes(x, op, fill, pos):
    d = 1
    while d < LANES:
        x = op(x, jnp.where(pos >= d, pltpu.roll(x, d, axis=1), fill))
        d *= 2
    return x


def _gate_scans(gt, bi, bf, seg_len, gc_ref, ut_ref):
    pad_rows = jnp.zeros((LANES - 16, LANES), F32)
    lane = lax.broadcasted_iota(jnp.int32, (8, LANES), 1)
    sub = lax.broadcasted_iota(jnp.int32, (8, LANES), 0)
    pos = lane & (seg_len - 1)
    for c in range(gt.shape[1] // LANES):
        lanes = slice(c * LANES, (c + 1) * LANES)
        ig = gt[0:8, lanes] + bi
        lf = _log_sigmoid(gt[8:16, lanes] + bf)
        b = _scan_lanes(lf, jnp.add, 0.0, pos)
        u = ig - b
        cm = _scan_lanes(u, jnp.maximum, -jnp.inf, pos)
        packed = jnp.concatenate([jnp.where(sub < N_HEADS, cm, b), jnp.where(sub < N_HEADS, u, 0.0)], axis=0)
        gc_ref[c] = jnp.concatenate([packed, pad_rows], axis=0).T
        ut_ref[c] = u


def _ffn_in_kernel(x_ref, xs_ref, g1_ref, wg_ref, wu_ref, wd_ref, gmix_ref, win_ref, wgate_ref, bi_ref, bf_ref,
                   rope_ref, x1_ref, qkv_ref, gc_ref, ut_ref, acc_ref, *, short_len):
    is_short = pl.program_id(0) == pl.num_programs(0) - 1
    seg_len = jnp.where(is_short, short_len, CHUNK)
    x = jnp.where(is_short, xs_ref[...], x_ref[...])
    xn = _rms_norm(x, g1_ref[...]).astype(BF16)
    _swiglu_into(acc_ref, xn, wg_ref, wu_ref, wd_ref)
    x1 = x + 0.5 * acc_ref[...]
    x1_ref[...] = x1
    hn = _rms_norm(x1, gmix_ref[...]).astype(BF16)
    gt = lax.dot_general(wgate_ref[...].astype(BF16), hn, _NT, preferred_element_type=F32)
    _gate_scans(gt, bi_ref[...], bf_ref[...], seg_len, gc_ref, ut_ref)
    cos2 = rope_ref[:, 0:LANES]
    sin2 = rope_ref[:, LANES:2 * LANES]
    key_scale = D_HEAD ** -0.5
    for j in range(8):
        cols = slice(j * D_GROUP, (j + 1) * D_GROUP)
        p = lax.dot_general(hn, win_ref[cols, :], _NT, preferred_element_type=F32)
        if j in (4, 5):
            p = _rope_heads(p, cos2, sin2)
        if j in (1, 5):
            p = p * key_scale
        qkv_ref[:, cols] = p.astype(BF16)


def _const_spec(shape):
    nd = len(shape)
    return pl.BlockSpec(shape, lambda *_: (0,) * nd, pipeline_mode=pl.Buffered(1))


def _ffn_in(x, x_short, g1, wg, wu, wd, gmix, win, wgate, bi, bf, rope, short_len):
    tm = TOKEN_TILE
    assert x_short.shape == (tm, D_MODEL)
    n_long = x.shape[0] // tm
    n_tok = x.shape[0] + tm
    pos_tiles = rope.shape[0] // tm - 1
    row = lambda i: (i, 0)
    blk = lambda i: (i, 0, 0)
    long_row = lambda i: (jnp.minimum(i, n_long - 1), 0)
    pos = lambda i: (jnp.where(i == n_long, pos_tiles, i % pos_tiles), 0)
    return pl.pallas_call(
        functools.partial(_ffn_in_kernel, short_len=short_len),
        grid=(n_long + 1,),
        in_specs=[
            pl.BlockSpec((tm, D_MODEL), long_row),
            _const_spec((tm, D_MODEL)),
            _const_spec((1, D_MODEL)),
            _const_spec(wg.shape),
            _const_spec(wu.shape),
            _const_spec(wd.shape),
            _const_spec((1, D_MODEL)),
            _const_spec(win.shape),
            _const_spec(wgate.shape),
            _const_spec(bi.shape),
            _const_spec(bf.shape),
            pl.BlockSpec((tm, 2 * LANES), pos),
        ],
        out_specs=[
            pl.BlockSpec((tm, D_MODEL), row),
            pl.BlockSpec((tm, N_QKV), row),
            pl.BlockSpec((tm // LANES, LANES, LANES), blk),
            pl.BlockSpec((tm // LANES, 8, LANES), blk),
        ],
        out_shape=[
            jax.ShapeDtypeStruct((n_tok, D_MODEL), F32),
            jax.ShapeDtypeStruct((n_tok, N_QKV), BF16),
            jax.ShapeDtypeStruct((n_tok // LANES, LANES, LANES), F32),
            jax.ShapeDtypeStruct((n_tok // LANES, 8, LANES), F32),
        ],
        scratch_shapes=[pltpu.VMEM((tm, D_MODEL), F32)],
        compiler_params=pltpu.CompilerParams(
            dimension_semantics=("arbitrary",), vmem_limit_bytes=VMEM_LIMIT),
        name="ffn_in",
    )(x, x_short, g1, wg, wu, wd, gmix, win, wgate, bi, bf, rope)


def _group_norm(h, g):
    mu = jnp.mean(h, axis=-1, keepdims=True)
    d = h - mu
    var = jnp.mean(d * d, axis=-1, keepdims=True)
    return d * lax.rsqrt(var + GN_EPS) * g


def _chunk_masks(frames):
    row_idx = lax.broadcasted_iota(jnp.int32, (frames, frames), 0)
    col_idx = lax.broadcasted_iota(jnp.int32, (frames, frames), 1)
    return row_idx >= col_idx, jnp.ones((frames, D_HEAD), BF16)


def _mixer_chunk(rows, qkv_ref, gcc, utc, s_ref, m_ref, r_ref, y_ref, consts, masks, ret_total, after_scores=None):
    gnm_ref, gnr_ref, decay_ref, cross_ref, tail_ref = consts
    causal, ones_blk = masks
    L = causal.shape[0]

    def scan_col(q, h):
        j = q * N_HEADS + h
        return jnp.broadcast_to(gcc[:, j:j + 1], (L, LANES))

    def head_slices(group, h):
        base = group * 4 * D_GROUP + h * D_HEAD
        return [qkv_ref[rows, base + j * D_GROUP:base + j * D_GROUP + D_HEAD] for j in range(4)]

    def scores(group):
        if L < LANES:
            return [lax.dot_general(*head_slices(group, h)[:2], _NT, preferred_element_type=F32)
                    for h in range(N_HEADS)]
        out = []
        zero_blk = jnp.zeros((L, D_HEAD), BF16)
        for h in range(0, N_HEADS, 2):
            base = group * 4 * D_GROUP + h * D_HEAD
            q2 = qkv_ref[rows, base:base + 2 * D_HEAD]
            k_a, k_b = head_slices(group, h)[1], head_slices(group, h + 1)[1]
            k_bd = jnp.concatenate([jnp.concatenate([k_a, zero_blk], axis=1),
                                    jnp.concatenate([zero_blk, k_b], axis=1)], axis=0)
            qk2 = lax.dot_general(q2, k_bd, _NT, preferred_element_type=F32)
            out += [qk2[:, :L], qk2[:, L:]]
        return out

    qk_m, qk_r = scores(0), scores(1)
    if after_scores is not None:
        after_scores()

    lhs_m, kw_m, gate_m, lhs_r, kt_r = [], [], [], [], []
    for h in range(N_HEADS):
        cm_b, b_b, u_b = scan_col(0, h), scan_col(1, h), scan_col(2, h)
        m_prev = m_ref[h:h + 1, :]
        big_m = jnp.maximum(cm_b, m_prev)
        a_b = jnp.exp(m_prev - big_m)
        m_t = b_b + big_m
        m_last = big_m[L - 1:L, :]
        m_ref[h:h + 1, :] = m_t[L - 1:L, :]
        w = jnp.where(causal, jnp.exp(utc[h:h + 1, :] - big_m[:, :L]), 0.0)
        q, k, _, _ = head_slices(0, h)
        lhs_m.append(jnp.concatenate([(q.astype(F32) * a_b).astype(BF16), (qk_m[h] * w).astype(BF16)], axis=1))
        kw_m.append((k.astype(F32) * jnp.exp(u_b - m_last)).astype(BF16))
        gate_m.append((a_b[L - 1:L, :], m_t))
    for h in range(N_HEADS):
        q, k, _, _ = head_slices(1, h)
        lhs_r.append(jnp.concatenate([(q.astype(F32) * cross_ref[h]).astype(BF16),
                                      (qk_r[h] * decay_ref[h]).astype(BF16)], axis=1))
        kt_r.append((k.astype(F32) * tail_ref[h]).astype(BF16))

    res_m, upd_m, res_r, upd_r = [], [], [], []
    for h in range(N_HEADS):
        v = head_slices(0, h)[2]
        v_aug = jnp.concatenate([v, ones_blk], axis=1)
        rhs = jnp.concatenate([s_ref[h].astype(BF16), v_aug], axis=0)
        res_m.append(jnp.dot(lhs_m[h], rhs, preferred_element_type=F32))
        upd_m.append(lax.dot_general(kw_m[h], v_aug, _TN, preferred_element_type=F32))
    for h in range(N_HEADS):
        v = head_slices(1, h)[2]
        rhs = jnp.concatenate([r_ref[h].astype(BF16), v], axis=0)
        res_r.append(jnp.dot(lhs_r[h], rhs, preferred_element_type=F32))
        upd_r.append(lax.dot_general(kt_r[h], v, _TN, preferred_element_type=F32))

    for h in range(N_HEADS):
        hc = slice(h * D_HEAD, (h + 1) * D_HEAD)
        a_last, m_t = gate_m[h]
        num = res_m[h][:, :D_HEAD]
        den = res_m[h][:, D_HEAD:]
        hh = num / jnp.maximum(jnp.abs(den), jnp.exp(-m_t))
        s_ref[h] = jnp.concatenate([a_last, a_last], axis=1) * s_ref[h] + upd_m[h]
        o = head_slices(0, h)[3]
        y = jax.nn.sigmoid(o.astype(F32)) * _group_norm(hh, gnm_ref[:, hc])
        y_ref[rows, hc] = y.astype(BF16)
    for h in range(N_HEADS):
        hc = slice(h * D_HEAD, (h + 1) * D_HEAD)
        r_ref[h] = ret_total[h] * r_ref[h] + upd_r[h]
        gf = head_slices(1, h)[3].astype(F32)
        y = gf * jax.nn.sigmoid(gf) * _group_norm(res_r[h], gnr_ref[:, hc])
        y_ref[rows, D_GROUP + h * D_HEAD:D_GROUP + (h + 1) * D_HEAD] = y.astype(BF16)


def _load_state(s_scr, c_ref, n_ref):
    s_scr[:, :, 0:D_HEAD] = c_ref[...]
    n_cols = jnp.concatenate([n_ref[...], jnp.zeros((LANES - 8, D_HEAD), F32)], axis=0).T
    for h in range(N_HEADS):
        s_scr[h, :, D_HEAD:2 * D_HEAD] = jnp.broadcast_to(n_cols[:, h:h + 1], (D_HEAD, LANES))


def _store_state(s_scr, c_ref, n_ref):
    c_ref[...] = s_scr[:, :, 0:D_HEAD]
    lane = lax.broadcasted_iota(jnp.int32, (D_HEAD, LANES), 1)
    n_cols = s_scr[0, :, D_HEAD:2 * D_HEAD]
    for h in range(1, N_HEADS):
        n_cols = jnp.where(lane == h, s_scr[h, :, D_HEAD:2 * D_HEAD], n_cols)
    n_ref[...] = n_cols.T[0:N_HEADS, :]


def _mixer_kernel(qkv_ref, gc_ref, ut_ref, c0_ref, n0_ref, m0_ref, r0_ref, gnm_ref, gnr_ref,
                  decay_ref, cross_ref, tail_ref,
                  y_ref, c_ref, n_ref, m_ref, r_ref, s_scr, *, frames, ret_total):
    m_ref[...] = m0_ref[...]
    r_ref[...] = r0_ref[...]
    consts = (gnm_ref, gnr_ref, decay_ref, cross_ref, tail_ref)
    masks = _chunk_masks(frames)
    streams = range(LANES // frames)
    for i in streams:
        _load_state(s_scr.at[i], c0_ref.at[i], n0_ref.at[i])
    for i in streams:
        seg = slice(i * frames, (i + 1) * frames)
        _mixer_chunk(seg, qkv_ref, gc_ref[0][seg, :], ut_ref[0][:, seg],
                     s_scr.at[i], m_ref.at[i], r_ref.at[i], y_ref, consts, masks, ret_total)
    for i in streams:
        _store_state(s_scr.at[i], c_ref.at[i], n_ref.at[i])


def _mix_out_kernel(qkv_ref, gc_ref, ut_ref, x1_ref, ys_ref,
                    gnm_ref, gnr_ref, decay_ref, cross_ref, tail_ref,
                    wout_ref, g2_ref, wg_ref, wu_ref, wd_ref, gfin_ref,
                    out_ref, outs_ref, c_out, n_out, m_out, r_out,
                    y_scr, acc_ref, x2_ref, s_scr, m_scr, r_scr, *, tiles_per_stream, n_tiles, ret_total):
    j = pl.program_id(0)
    n_chunks = TOKEN_TILE // CHUNK

    @pl.when(j == 0)
    def _():
        y_scr[...] = ys_ref[...]

    @pl.when(j % tiles_per_stream == 0)
    def _():
        s_scr[...] = jnp.zeros_like(s_scr)
        m_scr[...] = jnp.zeros_like(m_scr)
        r_scr[...] = jnp.zeros_like(r_scr)

    consts = (gnm_ref, gnr_ref, decay_ref, cross_ref, tail_ref)
    masks = _chunk_masks(CHUNK)

    x2 = x1_ref[...] + jnp.dot(y_scr[...], wout_ref[...], preferred_element_type=F32)
    x2_ref[...] = x2
    xn = _rms_norm(x2, g2_ref[...]).astype(BF16)

    pending = list(range(N_FF_CHUNKS))

    def ffn(k):
        for _ in range(min(k, len(pending))):
            _swiglu_chunk(acc_ref, xn, wg_ref, wu_ref, wd_ref, pending.pop(0))

    for c in range(n_chunks):
        _mixer_chunk(slice(c * CHUNK, (c + 1) * CHUNK), qkv_ref, gc_ref[c], ut_ref[c],
                     s_scr, m_scr, r_scr, y_scr, consts, masks, ret_total,
                     after_scores=functools.partial(ffn, 1))
        ffn(1)
    ffn(N_FF_CHUNKS)

    x3 = x2_ref[...] + 0.5 * acc_ref[...]
    out_ref[...] = _rms_norm(x3, gfin_ref[...])

    @pl.when(j == 0)
    def _():
        outs_ref[...] = out_ref[...]

    @pl.when((j % tiles_per_stream == tiles_per_stream - 1) & (j < n_tiles))
    def _():
        _store_state(s_scr, c_out.at[0], n_out.at[0])
        m_out[0] = m_scr[0:N_HEADS, :]
        r_out[0] = r_scr[...]


def _mixer_consts(frames):
    idx = np.arange(frames, dtype=np.float32)
    diff = idx[:, None] - idx[None, :]
    lg = LOG_GAMMA
    decay = np.where(diff >= 0, np.exp(lg[:, None, None] * np.maximum(diff, 0.0)), 0.0).astype(np.float32)
    rep = lambda col: np.ascontiguousarray(np.broadcast_to(col.T[:, :, None], (N_HEADS, frames, LANES))).astype(np.float32)
    cross = rep(np.exp(lg[None, :] * (idx[:, None] + 1.0)))
    tail = rep(np.exp(lg[None, :] * (frames - 1.0 - idx[:, None])))
    ret_total = tuple(float(np.exp(np.float32(lg_h) * np.float32(frames))) for lg_h in lg)
    return (jnp.asarray(decay), jnp.asarray(cross), jnp.asarray(tail)), ret_total


def _mixer(qkv, gc, ut, first_blk, c0, n0, m0, r0, gnm, gnr, frames):
    n_tok = c0.shape[0] * frames
    per_blk = LANES // frames
    (decay, cross, tail), ret_total = _mixer_consts(frames)
    row = lambda i: (i, 0)
    blk3 = lambda i: (i, 0, 0)
    blk4 = lambda i: (i, 0, 0, 0)
    in_row = lambda i: (first_blk + i, 0)
    in_blk3 = lambda i: (first_blk + i, 0, 0)
    return pl.pallas_call(
        functools.partial(_mixer_kernel, frames=frames, ret_total=ret_total),
        grid=(n_tok // LANES,),
        in_specs=[
            pl.BlockSpec((LANES, N_QKV), in_row),
            pl.BlockSpec((1, LANES, LANES), in_blk3),
            pl.BlockSpec((1, 8, LANES), in_blk3),
            pl.BlockSpec((per_blk, N_HEADS, D_HEAD, D_HEAD), blk4),
            pl.BlockSpec((per_blk, 8, D_HEAD), blk3),
            pl.BlockSpec((per_blk, N_HEADS, LANES), blk3),
            pl.BlockSpec((per_blk, N_HEADS, D_HEAD, D_HEAD), blk4),
            _const_spec((1, D_GROUP)),
            _const_spec((1, D_GROUP)),
            _const_spec(decay.shape),
            _const_spec(cross.shape),
            _const_spec(tail.shape),
        ],
        out_specs=[
            pl.BlockSpec((LANES, D_MODEL), row),
            pl.BlockSpec((per_blk, N_HEADS, D_HEAD, D_HEAD), blk4),
            pl.BlockSpec((per_blk, N_HEADS, D_HEAD), blk3),
            pl.BlockSpec((per_blk, N_HEADS, LANES), blk3),
            pl.BlockSpec((per_blk, N_HEADS, D_HEAD, D_HEAD), blk4),
        ],
        out_shape=[
            jax.ShapeDtypeStruct((n_tok, D_MODEL), BF16),
            jax.ShapeDtypeStruct(c0.shape, F32),
            jax.ShapeDtypeStruct((n0.shape[0], N_HEADS, D_HEAD), F32),
            jax.ShapeDtypeStruct(m0.shape, F32),
            jax.ShapeDtypeStruct(r0.shape, F32),
        ],
        scratch_shapes=[pltpu.VMEM((per_blk, N_HEADS, D_HEAD, 2 * D_HEAD), F32)],
        compiler_params=pltpu.CompilerParams(
            dimension_semantics=("arbitrary",), vmem_limit_bytes=VMEM_LIMIT),
        name="mixer",
    )(qkv, gc, ut, c0, n0, m0, r0, gnm, gnr, decay, cross, tail)


def _mix_out(qkv, gc, ut, x1, y_short, bsz, gnm, gnr, wout, g2, wg, wu, wd, gfin):
    tm = TOKEN_TILE
    assert y_short.shape == (tm, D_MODEL)
    n_tok = x1.shape[0] - tm
    n_tiles = n_tok // tm
    tiles_per_stream = n_tiles // bsz
    n_chunks = tm // CHUNK
    (decay, cross, tail), ret_total = _mixer_consts(CHUNK)

    mix_row = lambda j: (jnp.minimum(j, n_tiles - 1), 0)
    mix_blk = lambda j: (jnp.minimum(j, n_tiles - 1), 0, 0)
    dense_row = lambda j: (jnp.maximum(j - 1, 0), 0)
    stream4 = lambda j: (jnp.minimum(j, n_tiles - 1) // tiles_per_stream, 0, 0, 0)
    stream3 = lambda j: (jnp.minimum(j, n_tiles - 1) // tiles_per_stream, 0, 0)
    kern = functools.partial(_mix_out_kernel, tiles_per_stream=tiles_per_stream, n_tiles=n_tiles,
                             ret_total=ret_total)
    return pl.pallas_call(
        kern,
        grid=(n_tiles + 1,),
        in_specs=[
            pl.BlockSpec((tm, N_QKV), mix_row),
            pl.BlockSpec((n_chunks, LANES, LANES), mix_blk),
            pl.BlockSpec((n_chunks, 8, LANES), mix_blk),
            pl.BlockSpec((tm, D_MODEL), lambda j: (jnp.where(j == 0, n_tiles, j - 1), 0)),
            _const_spec((tm, D_MODEL)),
            _const_spec((1, D_GROUP)),
            _const_spec((1, D_GROUP)),
            _const_spec(decay.shape),
            _const_spec(cross.shape),
            _const_spec(tail.shape),
            _const_spec(wout.shape),
            _const_spec((1, D_MODEL)),
            _const_spec(wg.shape),
            _const_spec(wu.shape),
            _const_spec(wd.shape),
            _const_spec((1, D_MODEL)),
        ],
        out_specs=[
            pl.BlockSpec((tm, D_MODEL), dense_row),
            pl.BlockSpec((tm, D_MODEL), lambda j: (0, 0)),
            pl.BlockSpec((1, N_HEADS, D_HEAD, D_HEAD), stream4),
            pl.BlockSpec((1, N_HEADS, D_HEAD), stream3),
            pl.BlockSpec((1, N_HEADS, LANES), stream3),
            pl.BlockSpec((1, N_HEADS, D_HEAD, D_HEAD), stream4),
        ],
        out_shape=[
            jax.ShapeDtypeStruct((n_tok, D_MODEL), F32),
            jax.ShapeDtypeStruct((tm, D_MODEL), F32),
            jax.ShapeDtypeStruct((bsz, N_HEADS, D_HEAD, D_HEAD), F32),
            jax.ShapeDtypeStruct((bsz, N_HEADS, D_HEAD), F32),
            jax.ShapeDtypeStruct((bsz, N_HEADS, LANES), F32),
            jax.ShapeDtypeStruct((bsz, N_HEADS, D_HEAD, D_HEAD), F32),
        ],
        scratch_shapes=[
            pltpu.VMEM((tm, D_MODEL), BF16),
            pltpu.VMEM((tm, D_MODEL), F32),
            pltpu.VMEM((tm, D_MODEL), F32),
            pltpu.VMEM((N_HEADS, D_HEAD, 2 * D_HEAD), F32),
            pltpu.VMEM((8, LANES), F32),
            pltpu.VMEM((N_HEADS, D_HEAD, D_HEAD), F32),
        ],
        compiler_params=pltpu.CompilerParams(
            dimension_semantics=("arbitrary",), vmem_limit_bytes=VMEM_LIMIT),
        name="mix_out",
    )(qkv, gc, ut, x1, y_short, gnm, gnr, decay, cross, tail, wout, g2, wg, wu, wd, gfin)


def _rope_tables(pos):
    half = D_HEAD // 2
    inv = ROPE_BASE ** (-np.arange(half, dtype=np.float64) / half)
    ang = np.asarray(pos, np.float64)[:, None] * inv[None, :]
    cos, sin = np.cos(ang), np.sin(ang)
    return jnp.asarray(np.concatenate([cos, cos, -sin, sin], axis=-1), F32)


def kernel(x_prompt, x_sample, state_mlstm_C, state_mlstm_n, state_mlstm_m, state_ret_R,
           g_ffn1, w_ffn1_gate, w_ffn1_up, w_ffn1_down, g_mix, w_in, b_gates, g_gn_mlstm, g_gn_ret,
           w_out, g_ffn2, w_ffn2_gate, w_ffn2_up, w_ffn2_down, g_final):
    depth = g_ffn1.shape[0]
    bp, sp, _ = x_prompt.shape
    bs, ss, _ = x_sample.shape
    assert depth == 1, "the final norm is fused into the layer's last kernel"
    assert ss < CHUNK and LANES % ss == 0 and bs * ss == TOKEN_TILE and sp % TOKEN_TILE == 0
    rope = _rope_tables(np.concatenate([np.arange(sp), np.tile(PAST_LEN + np.arange(ss), bs)]))

    new_p, new_s = [], []
    for l in range(depth):
        cast = lambda w: w.astype(BF16)
        w_in_t = w_in[l].T
        wi_t = w_in_t[N_QKV:N_QKV + N_HEADS]
        wf_t = w_in_t[N_QKV + N_HEADS:]
        wgate = jnp.concatenate([wi_t, wi_t, wf_t, wf_t], axis=0)
        rows8 = lambda b: jnp.broadcast_to(jnp.tile(b.astype(F32), 2)[:, None], (8, LANES))
        in_w = (g_ffn1[l][None, :], cast(w_ffn1_gate[l]), cast(w_ffn1_up[l]), cast(w_ffn1_down[l]),
                g_mix[l][None, :], cast(w_in_t), wgate,
                rows8(b_gates[l][:N_HEADS]), rows8(b_gates[l][N_HEADS:]))
        gnm, gnr = g_gn_mlstm[l][None, :].astype(F32), g_gn_ret[l][None, :].astype(F32)
        out_w = (cast(w_out[l]), g_ffn2[l][None, :], cast(w_ffn2_gate[l]), cast(w_ffn2_up[l]),
                 cast(w_ffn2_down[l]), g_final[None, :])

        x1, qkv, gc, ut = _ffn_in(x_prompt.reshape(bp * sp, D_MODEL), x_sample.reshape(bs * ss, D_MODEL),
                                  *in_w, rope, ss)
        m0 = jnp.broadcast_to(state_mlstm_m[l].astype(F32)[..., None], (bs, N_HEADS, LANES))
        n0 = jnp.pad(state_mlstm_n[l].astype(F32), ((0, 0), (0, 8 - N_HEADS), (0, 0)))
        ys, c_s, n_s, m_s, r_s = _mixer(qkv, gc, ut, bp * sp // LANES, state_mlstm_C[l].astype(F32), n0, m0,
                                        state_ret_R[l].astype(F32), gnm, gnr, ss)
        out_p, out_s, c_p, n_p, m_p, r_p = _mix_out(qkv, gc, ut, x1, ys, bp, gnm, gnr, *out_w)
        new_p.append((c_p, n_p, m_p[..., 0], r_p))
        new_s.append((c_s, n_s, m_s[..., 0], r_s))
    hp, hs = out_p.reshape(bp, sp, D_MODEL), out_s.reshape(bs, ss, D_MODEL)
    stack = lambda sts, i: jnp.stack([s[i] for s in sts])
    return (hp, hs,
            stack(new_p, 0), stack(new_p, 1), stack(new_p, 2), stack(new_p, 3),
            stack(new_s, 0), stack(new_s, 1), stack(new_s, 2), stack(new_s, 3))
```
